```python
import jax, jax.numpy as jnp
from jax import lax
import numpy as np

D_MODEL = 1024
BATCH = 8
SEQ = 2048
DEPTH = 1

A_HEADS = 16
A_KV_HEADS = 2
A_HEAD_DIM = 64
A_GROUP = A_HEADS // A_KV_HEADS
A_WIDTH = A_HEADS * A_HEAD_DIM
A_KV_WIDTH = A_KV_HEADS * A_HEAD_DIM
WINDOW = 128
BLOCK = 128
ROPE_THETA = 10000.0

B_HEADS = 4
B_QK_WIDTH = D_MODEL // 2
B_V_WIDTH = D_MODEL
B_KEY_DIM = B_QK_WIDTH // B_HEADS
B_VAL_DIM = B_V_WIDTH // B_HEADS
GATE_RANK = 16
GATE_TAU = 16.0
CHUNK = 64

EPS = 1e-5
NEG_INF = -1e30

IN_SPLITS = (A_WIDTH, A_KV_WIDTH, A_KV_WIDTH, A_WIDTH,
             B_QK_WIDTH, B_QK_WIDTH, B_V_WIDTH, B_V_WIDTH,
             GATE_RANK,
             D_MODEL, D_MODEL)
IN_WIDTH = (2 * A_WIDTH + 2 * A_KV_WIDTH + 2 * B_QK_WIDTH + 2 * B_V_WIDTH
            + GATE_RANK + 2 * D_MODEL)

kernel_name = 'hybrid_swa_sink_gla_gated_block'


def rms_norm(x, w):
    xf = x.astype(jnp.float32)
    y = xf * lax.rsqrt(jnp.mean(xf * xf, axis=-1, keepdims=True) + EPS)
    return (y * w.astype(jnp.float32)).astype(x.dtype)


def rope(x, positions):
    half = A_HEAD_DIM // 2
    inv_freq = ROPE_THETA ** (-jnp.arange(half, dtype=jnp.float32) / half)
    ang = positions.astype(jnp.float32)[..., None] * inv_freq
    cos = jnp.cos(ang)[:, :, None, :]
    sin = jnp.sin(ang)[:, :, None, :]
    xf = x.astype(jnp.float32)
    x1, x2 = xf[..., :half], xf[..., half:]
    return jnp.concatenate([x1 * cos - x2 * sin, x2 * cos + x1 * sin], axis=-1).astype(x.dtype)


def sliding_window_attention(q, k, v, sinks):
    bsz, t = q.shape[0], q.shape[1]
    nb = t // BLOCK
    qb = q.reshape(bsz, nb, BLOCK, A_KV_HEADS, A_GROUP, A_HEAD_DIM)
    kb = k.reshape(bsz, nb, BLOCK, A_KV_HEADS, A_HEAD_DIM)
    vb = v.reshape(bsz, nb, BLOCK, A_KV_HEADS, A_HEAD_DIM)
    pad = ((0, 0), (1, 0), (0, 0), (0, 0), (0, 0))
    keys = jnp.concatenate([jnp.pad(kb, pad)[:, :-1], kb], axis=2)
    vals = jnp.concatenate([jnp.pad(vb, pad)[:, :-1], vb], axis=2)
    s = jnp.einsum('bnqhgd,bnkhd->bnhgqk', qb, keys).astype(jnp.float32) * (A_HEAD_DIM ** -0.5)
    qi = jnp.arange(BLOCK)[:, None]
    ki = jnp.arange(2 * BLOCK)[None, :]
    rel = qi + BLOCK - ki
    band = (rel >= 0) & (rel < WINDOW)
    blk = jnp.arange(nb)[:, None, None]
    valid = band[None] & ((blk > 0) | (ki >= BLOCK)[None])
    s = jnp.where(valid[None, :, None, None], s, NEG_INF)
    sink = jnp.broadcast_to(
        sinks.astype(jnp.float32).reshape(A_KV_HEADS, A_GROUP)[None, None, :, :, None, None],
        s.shape[:-1] + (1,))
    p = jax.nn.softmax(jnp.concatenate([s, sink], axis=-1), axis=-1)[..., :-1]
    o = jnp.einsum('bnhgqk,bnkhd->bnqhgd', p.astype(v.dtype), vals)
    return o.reshape(bsz, t, A_WIDTH)


def gated_linear_attention(q, k, v, log_a):
    bsz, t = q.shape[0], q.shape[1]
    n = t // CHUNK

    def chunks(a):
        return a.astype(jnp.float32).reshape(bsz, n, CHUNK, B_HEADS, -1).transpose(0, 1, 3, 2, 4)

    qc = chunks(q) * (B_KEY_DIM ** -0.5)
    kc, vc = chunks(k), chunks(v)
    b = jnp.cumsum(chunks(log_a), axis=3)
    b_last = b[:, :, :, -1:, :]
    q_e = qc * jnp.exp(b)
    k_e = kc * jnp.exp(-b)
    k_s = kc * jnp.exp(b_last - b)
    causal = jnp.tril(jnp.ones((CHUNK, CHUNK), dtype=bool))
    att = jnp.where(causal, jnp.einsum('bnhid,bnhjd->bnhij', q_e, k_e), 0.0)
    o_intra = jnp.einsum('bnhij,bnhjv->bnhiv', att, vc)
    inc = jnp.einsum('bnhjd,bnhjv->bnhdv', k_s, vc)
    decay = jnp.exp(b_last[:, :, :, 0, :])

    def step(state, inp):
        dec, dS = inp
        return dec[..., None] * state + dS, state

    s0 = jnp.zeros((bsz, B_HEADS, B_KEY_DIM, B_VAL_DIM), jnp.float32)
    _, states = lax.scan(step, s0, (decay.transpose(1, 0, 2, 3), inc.transpose(1, 0, 2, 3, 4)))
    states = states.transpose(1, 0, 2, 3, 4)
    o = o_intra + jnp.einsum('bnhid,bnhdv->bnhiv', q_e, states)
    return o.transpose(0, 1, 3, 2, 4).reshape(bsz, t, B_HEADS, B_VAL_DIM)


def setup_inputs(seed: int = 0) -> dict:
    key = jax.random.key(seed)
    ks = jax.random.split(key, 13)
    f32 = jnp.float32

    def lin(k, shape, fan_in):
        return jax.random.normal(k, shape, f32) * (fan_in ** -0.5)

    x = jax.random.normal(ks[0], (BATCH, SEQ, D_MODEL), f32)
    positions = jnp.broadcast_to(jnp.arange(SEQ, dtype=jnp.int32)[None, :], (BATCH, SEQ))
    norm_w = 1.0 + 0.02 * jax.random.normal(ks[1], (DEPTH, D_MODEL), f32)
    w_in = lin(ks[2], (DEPTH, D_MODEL, IN_WIDTH), D_MODEL)
    a_sinks = 0.5 * jax.random.normal(ks[3], (DEPTH, A_HEADS), f32)
    b_gate_up = lin(ks[4], (DEPTH, GATE_RANK, B_QK_WIDTH), GATE_RANK)
    b_gate_bias = 0.1 * jax.random.normal(ks[5], (DEPTH, B_QK_WIDTH), f32)
    b_out_norm_w = 1.0 + 0.02 * jax.random.normal(ks[6], (DEPTH, B_VAL_DIM), f32)
    w_a_proj = lin(ks[7], (DEPTH, A_WIDTH, D_MODEL), A_WIDTH)
    w_b_proj = lin(ks[8], (DEPTH, B_V_WIDTH, D_MODEL), B_V_WIDTH)
    w_out = lin(ks[9], (DEPTH, D_MODEL, D_MODEL), D_MODEL)
    final_norm_w = 1.0 + 0.02 * jax.random.normal(ks[10], (D_MODEL,), f32)
    return {'x': x, 'positions': positions, 'norm_w': norm_w, 'w_in': w_in,
            'a_sinks': a_sinks, 'b_gate_up': b_gate_up, 'b_gate_bias': b_gate_bias,
            'b_out_norm_w': b_out_norm_w, 'w_a_proj': w_a_proj, 'w_b_proj': w_b_proj,
            'w_out': w_out, 'final_norm_w': final_norm_w}


def reference(x, positions, norm_w, w_in, a_sinks, b_gate_up, b_gate_bias,
              b_out_norm_w, w_a_proj, w_b_proj, w_out, final_norm_w):
    bsz, t = x.shape[0], x.shape[1]
    offsets = [int(o) for o in np.cumsum(IN_SPLITS)[:-1]]
    for layer in range(DEPTH):
        h = rms_norm(x, norm_w[layer])
        proj = jnp.einsum('btd,de->bte', h, w_in[layer])
        (a_q, a_k, a_v, a_gate, b_q, b_k, b_v, b_gate, b_low,
         m_a, m_b) = jnp.split(proj, offsets, axis=-1)

        q = rope(a_q.reshape(bsz, t, A_HEADS, A_HEAD_DIM), positions)
        k = rope(a_k.reshape(bsz, t, A_KV_HEADS, A_HEAD_DIM), positions)
        v = a_v.reshape(bsz, t, A_KV_HEADS, A_HEAD_DIM)
        o_a = sliding_window_attention(q, k, v, a_sinks[layer]) * jax.nn.silu(a_gate)
        y_a = jnp.einsum('bte,ed->btd', o_a, w_a_proj[layer])

        gk = jnp.einsum('btr,re->bte', b_low, b_gate_up[layer]) + b_gate_bias[layer]
        log_a = jax.nn.log_sigmoid(gk.astype(jnp.float32)) / GATE_TAU
        o_b = gated_linear_attention(b_q.reshape(bsz, t, B_HEADS, B_KEY_DIM),
                                     b_k.reshape(bsz, t, B_HEADS, B_KEY_DIM),
                                     b_v.reshape(bsz, t, B_HEADS, B_VAL_DIM),
                                     log_a.reshape(bsz, t, B_HEADS, B_KEY_DIM))
        o_b = rms_norm(o_b.astype(x.dtype), b_out_norm_w[layer]).reshape(bsz, t, B_V_WIDTH)
        o_b = o_b * jax.nn.silu(b_gate)
        y_b = jnp.einsum('bte,ed->btd', o_b, w_b_proj[layer])

        merged = jax.nn.sigmoid(m_a) * y_a + jax.nn.sigmoid(m_b) * y_b
        x = x + jnp.einsum('btd,de->bte', merged, w_out[layer])
    return rms_norm(x, final_norm_w)
```

```python
import functools

import jax
import jax.numpy as jnp
from jax import lax
from jax.experimental import pallas as pl
from jax.experimental.pallas import tpu as pltpu

D_MODEL = 1024
A_HEADS = 16
A_KV_HEADS = 2
A_HEAD_DIM = 64
A_GROUP = A_HEADS // A_KV_HEADS
A_WIDTH = A_HEADS * A_HEAD_DIM
A_KV_WIDTH = A_KV_HEADS * A_HEAD_DIM
WINDOW = 128
BLOCK = 128
ROPE_THETA = 10000.0

B_HEADS = 4
B_QK_WIDTH = D_MODEL // 2
B_V_WIDTH = D_MODEL
B_KEY_DIM = B_QK_WIDTH // B_HEADS
B_VAL_DIM = B_V_WIDTH // B_HEADS
GATE_RANK = 16
GATE_TAU = 16.0
CHUNK = 64

EPS = 1e-5
NEG_INF = -1e30

LANES = 128

COL_A_Q = 0
COL_A_GATE = 1024
COL_B_V = 2048
COL_B_GATE = 3072
COL_M_A = 4096
COL_M_B = 5120
COL_B_Q = 6144
COL_B_K = 6656
COL_A_K = 7168
COL_A_V = 7296
COL_B_LOW = 7424
PROJ_WIDTH = 7552

VMEM_LIMIT_BYTES = 60 * 1024 * 1024


def _permute_w_in(w_in):
    sizes = (A_WIDTH, A_KV_WIDTH, A_KV_WIDTH, A_WIDTH, B_QK_WIDTH, B_QK_WIDTH, B_V_WIDTH,
             B_V_WIDTH, GATE_RANK, D_MODEL, D_MODEL)
    offs = [0]
    for s in sizes:
        offs.append(offs[-1] + s)
    a_q, a_k, a_v, a_gate, b_q, b_k, b_v, b_gate, b_low, m_a, m_b = [
        w_in[:, offs[i]:offs[i + 1]] for i in range(len(sizes))]
    pad = jnp.zeros((w_in.shape[0], PROJ_WIDTH - COL_B_LOW - GATE_RANK), w_in.dtype)
    return jnp.concatenate([a_q, a_gate, b_v, b_gate, m_a, m_b, b_q, b_k, a_k, a_v, b_low, pad],
                           axis=1)


IN_TM = 512
IN_CHUNK = 512


def _rope_slab(xs, cos, sin_signed, first_half):
    partner = jnp.where(first_half, pltpu.roll(xs, LANES - A_HEAD_DIM // 2, 1),
                        pltpu.roll(xs, A_HEAD_DIM // 2, 1))
    return xs * cos + partner * sin_signed


def _in_proj_kernel(x_ref, pos_ref, nw_ref, invf_ref, w_ref, o_ref):
    x = x_ref[...]
    var = jnp.mean(x * x, axis=-1, keepdims=True)
    h = (x * lax.rsqrt(var + EPS) * nw_ref[...]).astype(jnp.bfloat16)

    ang = pos_ref[...].astype(jnp.float32) * invf_ref[...]
    cos = jnp.cos(ang)
    sin = jnp.sin(ang)
    lane = lax.broadcasted_iota(jnp.int32, (1, LANES), 1)
    first_half = (lane % A_HEAD_DIM) < (A_HEAD_DIM // 2)
    sin_signed = jnp.where(first_half, -sin, sin)
    q_scale = A_HEAD_DIM ** -0.5

    for c0 in range(0, PROJ_WIDTH, IN_CHUNK):
        cw = min(IN_CHUNK, PROJ_WIDTH - c0)
        acc = jnp.dot(h, w_ref[:, c0:c0 + cw], preferred_element_type=jnp.float32)
        for s0 in range(0, cw, LANES):
            col = c0 + s0
            slab = acc[:, s0:s0 + LANES]
            if COL_A_Q <= col < COL_A_Q + A_WIDTH:
                slab = _rope_slab(slab, cos, sin_signed, first_half) * q_scale
            elif COL_A_K <= col < COL_A_K + A_KV_WIDTH:
                slab = _rope_slab(slab, cos, sin_signed, first_half)
            o_ref[:, col:col + LANES] = slab.astype(o_ref.dtype)


def _in_proj(x2, pos2, norm_w, inv_freq_row, w_perm):
    m = x2.shape[0]
    return pl.pallas_call(
        _in_proj_kernel,
        grid=(m // IN_TM,),
        in_specs=[
            pl.BlockSpec((IN_TM, D_MODEL), lambda i: (i, 0)),
            pl.BlockSpec((IN_TM, 1), lambda i: (i, 0)),
            pl.BlockSpec((1, D_MODEL), lambda i: (0, 0)),
            pl.BlockSpec((1, LANES), lambda i: (0, 0)),
            pl.BlockSpec((D_MODEL, PROJ_WIDTH), lambda i: (0, 0), pipeline_mode=pl.Buffered(1)),
        ],
        out_specs=pl.BlockSpec((IN_TM, PROJ_WIDTH), lambda i: (i, 0)),
        out_shape=jax.ShapeDtypeStruct((m, PROJ_WIDTH), jnp.bfloat16),
        compiler_params=pltpu.CompilerParams(dimension_semantics=("arbitrary",),
                                             vmem_limit_bytes=VMEM_LIMIT_BYTES),
        name="in_proj",
    )(x2, pos2, norm_w, inv_freq_row, w_perm)


def _swa_kernel(sink_ref, q_ref, kp_ref, kc_ref, vp_ref, vc_ref, g_ref, o_ref):
    n = pl.program_id(1)
    q = q_ref[...]
    k = jnp.concatenate([kp_ref[...], kc_ref[...]], axis=0)
    v = jnp.concatenate([vp_ref[...], vc_ref[...]], axis=0)
    qi = lax.broadcasted_iota(jnp.int32, (BLOCK, 2 * BLOCK), 0)
    ki = lax.broadcasted_iota(jnp.int32, (BLOCK, 2 * BLOCK), 1)
    rel = qi + BLOCK - ki
    valid = (rel >= 0) & (rel < WINDOW) & ((n > 0) | (ki >= BLOCK))

    outs = []
    for head in range(A_HEADS):
        g = head // A_GROUP
        q_h = q[:, head * A_HEAD_DIM:(head + 1) * A_HEAD_DIM]
        k_g = k[:, g * A_HEAD_DIM:(g + 1) * A_HEAD_DIM]
        v_g = v[:, g * A_HEAD_DIM:(g + 1) * A_HEAD_DIM]
        s = lax.dot_general(q_h, k_g, (((1,), (1,)), ((), ())),
                            preferred_element_type=jnp.float32)
        s = jnp.where(valid, s, NEG_INF)
        sink = sink_ref[head]
        mx = jnp.maximum(jnp.max(s, axis=-1, keepdims=True), sink)
        p = jnp.exp(s - mx)
        denom = jnp.sum(p, axis=-1, keepdims=True) + jnp.exp(sink - mx)
        o_h = jnp.dot(p.astype(jnp.bfloat16), v_g, preferred_element_type=jnp.float32)
        outs.append(o_h * (1.0 / denom))
    o = jnp.concatenate(outs, axis=1)
    gate = g_ref[...].astype(jnp.float32)
    o_ref[...] = (o * (gate * jax.nn.sigmoid(gate))).astype(o_ref.dtype)


def _swa(proj, sinks, bsz, t):
    nb = t // BLOCK
    kcol, vcol = COL_A_K // A_KV_WIDTH, COL_A_V // A_KV_WIDTH

    def cur(col):
        return lambda b, n: (b * nb + n, col)

    def prev(col):
        return lambda b, n: (b * nb + jnp.maximum(n - 1, 0), col)

    return pl.pallas_call(
        _swa_kernel,
        grid=(bsz, nb),
        in_specs=[
            pl.BlockSpec(memory_space=pltpu.SMEM),
            pl.BlockSpec((BLOCK, A_WIDTH), cur(COL_A_Q // A_WIDTH)),
            pl.BlockSpec((BLOCK, A_KV_WIDTH), prev(kcol)),
            pl.BlockSpec((BLOCK, A_KV_WIDTH), cur(kcol)),
            pl.BlockSpec((BLOCK, A_KV_WIDTH), prev(vcol)),
            pl.BlockSpec((BLOCK, A_KV_WIDTH), cur(vcol)),
            pl.BlockSpec((BLOCK, A_WIDTH), cur(COL_A_GATE // A_WIDTH)),
        ],
        out_specs=pl.BlockSpec((BLOCK, A_WIDTH), lambda b, n: (b * nb + n, 0)),
        out_shape=jax.ShapeDtypeStruct((bsz * t, A_WIDTH), jnp.bfloat16),
        compiler_params=pltpu.CompilerParams(dimension_semantics=("arbitrary", "arbitrary"),
                                             vmem_limit_bytes=VMEM_LIMIT_BYTES),
        name="swa",
    )(sinks, proj, proj, proj, proj, proj, proj)


GLA_TG = 256


def _cumsum_rows(x):
    rows = x.shape[0]
    row = lax.broadcasted_iota(jnp.int32, x.shape, 0)
    s = 1
    while s < rows:
        x = x + jnp.where(row >= s, pltpu.roll(x, s, 0), 0.0)
        s *= 2
    return x


def _gla_kernel(q_ref, k_ref, v_ref, g_ref, low_ref, up_ref, bias_ref, nw_ref, o_ref, state_ref):
    @pl.when(pl.program_id(1) == 0)
    def _():
        state_ref[...] = jnp.zeros_like(state_ref)

    ci = lax.broadcasted_iota(jnp.int32, (CHUNK, CHUNK), 0)
    cj = lax.broadcasted_iota(jnp.int32, (CHUNK, CHUNK), 1)
    causal = ci >= cj
    eye = (lax.broadcasted_iota(jnp.int32, (B_KEY_DIM, B_KEY_DIM), 0)
           == lax.broadcasted_iota(jnp.int32, (B_KEY_DIM, B_KEY_DIM), 1))
    q_scale = B_KEY_DIM ** -0.5

    for c in range(GLA_TG // CHUNK):
        rows = slice(c * CHUNK, (c + 1) * CHUNK)
        gk = jnp.dot(low_ref[rows, :], up_ref[...], preferred_element_type=jnp.float32) + bias_ref[...]
        log_a = (jnp.minimum(gk, 0.0) - jnp.log1p(jnp.exp(-jnp.abs(gk)))) / GATE_TAU
        b = _cumsum_rows(log_a)
        b_last = b[CHUNK - 1:CHUNK, :]
        qf = q_ref[rows, :].astype(jnp.float32) * q_scale
        kf = k_ref[rows, :].astype(jnp.float32)
        q_e = (qf * jnp.exp(b)).astype(jnp.bfloat16)
        k_e = (kf * jnp.exp(-b)).astype(jnp.bfloat16)
        k_s = (kf * jnp.exp(b_last - b)).astype(jnp.bfloat16)
        decay_row = jnp.exp(b_last)
        for h in range(B_HEADS):
            ks = slice(h * B_KEY_DIM, (h + 1) * B_KEY_DIM)
            vs = slice(h * B_VAL_DIM, (h + 1) * B_VAL_DIM)
            v_h = v_ref[rows, vs]
            st = state_ref[h]
            att = lax.dot_general(q_e[:, ks], k_e[:, ks], (((1,), (1,)), ((), ())),
                                  preferred_element_type=jnp.float32)
            att = jnp.where(causal, att, 0.0).astype(jnp.bfloat16)
            o_h = (jnp.dot(att, v_h, preferred_element_type=jnp.float32)
                   + jnp.dot(q_e[:, ks], st.astype(jnp.bfloat16), preferred_element_type=jnp.float32))
            inc = lax.dot_general(k_s[:, ks], v_h, (((0,), (0,)), ((), ())),
                                  preferred_element_type=jnp.float32)
            decay_col = jnp.sum(jnp.where(eye, decay_row[:, ks], 0.0), axis=1, keepdims=True)
            state_ref[h] = decay_col * st + inc
            var = jnp.mean(o_h * o_h, axis=-1, keepdims=True)
            o_n = o_h * lax.rsqrt(var + EPS) * nw_ref[...]
            gate = g_ref[rows, vs].astype(jnp.float32)
            o_ref[rows, vs] = (o_n * (gate * jax.nn.sigmoid(gate))).astype(o_ref.dtype)


def _gla(proj, up_pad, bias, out_norm_w, bsz, t):
    nt = t // GLA_TG

    def col(c):
        return lambda b, n: (b * nt + n, c)

    return pl.pallas_call(
        _gla_kernel,
        grid=(bsz, nt),
        in_specs=[
            pl.BlockSpec((GLA_TG, B_QK_WIDTH), col(COL_B_Q // B_QK_WIDTH)),
            pl.BlockSpec((GLA_TG, B_QK_WIDTH), col(COL_B_K // B_QK_WIDTH)),
            pl.BlockSpec((GLA_TG, B_V_WIDTH), col(COL_B_V // B_V_WIDTH)),
            pl.BlockSpec((GLA_TG, B_V_WIDTH), col(COL_B_GATE // B_V_WIDTH)),
            pl.BlockSpec((GLA_TG, LANES), col(COL_B_LOW // LANES)),
            pl.BlockSpec((LANES, B_QK_WIDTH), lambda b, n: (0, 0)),
            pl.BlockSpec((1, B_QK_WIDTH), lambda b, n: (0, 0)),
            pl.BlockSpec((1, B_VAL_DIM), lambda b, n: (0, 0)),
        ],
        out_specs=pl.BlockSpec((GLA_TG, B_V_WIDTH), lambda b, n: (b * nt + n, 0)),
        out_shape=jax.ShapeDtypeStruct((bsz * t, B_V_WIDTH), jnp.bfloat16),
        scratch_shapes=[pltpu.VMEM((B_HEADS, B_KEY_DIM, B_VAL_DIM), jnp.float32)],
        compiler_params=pltpu.CompilerParams(dimension_semantics=("arbitrary", "arbitrary"),
                                             vmem_limit_bytes=VMEM_LIMIT_BYTES),
        name="gla",
    )(proj, proj, proj, proj, proj, up_pad, bias, out_norm_w)


OUT_TM = 512


def _out_proj_kernel(x_ref, oa_ref, ob_ref, ma_ref, mb_ref, wa_ref, wb_ref, wo_ref, fw_ref, o_ref):
    y_a = jnp.dot(oa_ref[...], wa_ref[...], preferred_element_type=jnp.float32)
    y_b = jnp.dot(ob_ref[...], wb_ref[...], preferred_element_type=jnp.float32)
    merged = (jax.nn.sigmoid(ma_ref[...].astype(jnp.float32)) * y_a
              + jax.nn.sigmoid(mb_ref[...].astype(jnp.float32)) * y_b)
    z = x_ref[...] + jnp.dot(merged.astype(jnp.bfloat16), wo_ref[...],
                             preferred_element_type=jnp.float32)
    var = jnp.mean(z * z, axis=-1, keepdims=True)
    o_ref[...] = z * lax.rsqrt(var + EPS) * fw_ref[...]


def _out_proj(x2, o_a, o_b, proj, wa, wb, wo, final_w):
    m = x2.shape[0]
    row = lambda i: (i, 0)
    const = lambda i: (0, 0)
    return pl.pallas_call(
        _out_proj_kernel,
        grid=(m // OUT_TM,),
        in_specs=[
            pl.BlockSpec((OUT_TM, D_MODEL), row),
            pl.BlockSpec((OUT_TM, A_WIDTH), row),
            pl.BlockSpec((OUT_TM, B_V_WIDTH), row),
            pl.BlockSpec((OUT_TM, D_MODEL), lambda i: (i, COL_M_A // D_MODEL)),
            pl.BlockSpec((OUT_TM, D_MODEL), lambda i: (i, COL_M_B // D_MODEL)),
            pl.BlockSpec((A_WIDTH, D_MODEL), const),
            pl.BlockSpec((B_V_WIDTH, D_MODEL), const),
            pl.BlockSpec((D_MODEL, D_MODEL), const),
            pl.BlockSpec((1, D_MODEL), const),
        ],
        out_specs=pl.BlockSpec((OUT_TM, D_MODEL), row),
        out_shape=jax.ShapeDtypeStruct((m, D_MODEL), jnp.float32),
        compiler_params=pltpu.CompilerParams(dimension_semantics=("arbitrary",),
                                             vmem_limit_bytes=VMEM_LIMIT_BYTES),
        name="out_proj",
    )(x2, o_a, o_b, proj, proj, wa, wb, wo, final_w)


def kernel(x, positions, norm_w, w_in, a_sinks, b_gate_up, b_gate_bias, b_out_norm_w,
           w_a_proj, w_b_proj, w_out, final_norm_w):
    bsz, t, d = x.shape
    assert d == D_MODEL and t % GLA_TG == 0 and t % BLOCK == 0 and (bsz * t) % IN_TM == 0
    depth = norm_w.shape[0]
    bf16 = jnp.bfloat16

    half = A_HEAD_DIM // 2
    inv_freq = ROPE_THETA ** (-jnp.arange(half, dtype=jnp.float32) / half)
    inv_freq_row = jnp.tile(inv_freq, LANES // half)[None, :]
    pos2 = positions.reshape(bsz * t, 1)

    x2 = x.reshape(bsz * t, d)
    for layer in range(depth):
        w_perm = _permute_w_in(w_in[layer]).astype(bf16)
        up_pad = jnp.zeros((LANES, B_QK_WIDTH), bf16).at[:GATE_RANK].set(b_gate_up[layer].astype(bf16))
        proj = _in_proj(x2, pos2, norm_w[layer][None, :], inv_freq_row, w_perm)
        o_a = _swa(proj, a_sinks[layer], bsz, t)
        o_b = _gla(proj, up_pad, b_gate_bias[layer][None, :], b_out_norm_w[layer][None, :], bsz, t)
        z = _out_proj(x2, o_a, o_b, proj, w_a_proj[layer].astype(bf16), w_b_proj[layer].astype(bf16),
                      w_out[layer].astype(bf16), final_norm_w[None, :])
        assert depth == 1
        x2 = z
    return x2.reshape(bsz, t, d)
```

```python
import functools

import jax
import jax.numpy as jnp
from jax import lax
from jax.experimental import pallas as pl
from jax.experimental.pallas import tpu as pltpu

D_MODEL = 1024
A_HEADS = 16
A_KV_HEADS = 2
A_HEAD_DIM = 64
A_GROUP = A_HEADS // A_KV_HEADS
A_WIDTH = A_HEADS * A_HEAD_DIM
A_KV_WIDTH = A_KV_HEADS * A_HEAD_DIM
WINDOW = 128
BLOCK = 128
ROPE_THETA = 10000.0

B_HEADS = 4
B_QK_WIDTH = D_MODEL // 2
B_V_WIDTH = D_MODEL
B_KEY_DIM = B_QK_WIDTH // B_HEADS
B_VAL_DIM = B_V_WIDTH // B_HEADS
GATE_RANK = 16
GATE_TAU = 16.0
CHUNK = 64

EPS = 1e-5
NEG_INF = -1e30

LANES = 128

COL_A_Q = 0
COL_A_GATE = 1024
COL_B_V = 2048
COL_B_GATE = 3072
COL_M_A = 4096
COL_M_B = 5120
COL_B_Q = 6144
COL_B_K = 6656
COL_A_K = 7168
COL_A_V = 7296
COL_B_LOW = 7424
PROJ_WIDTH = 7552

VMEM_LIMIT_BYTES = 60 * 1024 * 1024


def _permute_w_in(w_in):
    sizes = (A_WIDTH, A_KV_WIDTH, A_KV_WIDTH, A_WIDTH, B_QK_WIDTH, B_QK_WIDTH, B_V_WIDTH,
             B_V_WIDTH, GATE_RANK, D_MODEL, D_MODEL)
    offs = [0]
    for s in sizes:
        offs.append(offs[-1] + s)
    a_q, a_k, a_v, a_gate, b_q, b_k, b_v, b_gate, b_low, m_a, m_b = [
        w_in[:, offs[i]:offs[i + 1]] for i in range(len(sizes))]
    pad = jnp.zeros((w_in.shape[0], PROJ_WIDTH - COL_B_LOW - GATE_RANK), w_in.dtype)
    return jnp.concatenate([a_q, a_gate, b_v, b_gate, m_a, m_b, b_q, b_k, a_k, a_v, b_low, pad],
                           axis=1)


IN_TM = 512
IN_CHUNK = 512


def _rope_slab(xs, cos, sin_signed, first_half):
    partner = jnp.where(first_half, pltpu.roll(xs, LANES - A_HEAD_DIM // 2, 1),
                        pltpu.roll(xs, A_HEAD_DIM // 2, 1))
    return xs * cos + partner * sin_signed


def _in_proj_kernel(x_ref, pos_ref, nw_ref, invf_ref, w_ref, o_ref):
    x = x_ref[...]
    var = jnp.mean(x * x, axis=-1, keepdims=True)
    h = (x * lax.rsqrt(var + EPS) * nw_ref[...]).astype(jnp.bfloat16)

    ang = pos_ref[...].astype(jnp.float32) * invf_ref[...]
    cos = jnp.cos(ang)
    sin = jnp.sin(ang)
    lane = lax.broadcasted_iota(jnp.int32, (1, LANES), 1)
    first_half = (lane % A_HEAD_DIM) < (A_HEAD_DIM // 2)
    sin_signed = jnp.where(first_half, -sin, sin)
    q_scale = A_HEAD_DIM ** -0.5

    for c0 in range(0, PROJ_WIDTH, IN_CHUNK):
        cw = min(IN_CHUNK, PROJ_WIDTH - c0)
        acc = jnp.dot(h, w_ref[:, c0:c0 + cw], preferred_element_type=jnp.float32)
        for s0 in range(0, cw, LANES):
            col = c0 + s0
            slab = acc[:, s0:s0 + LANES]
            if COL_A_Q <= col < COL_A_Q + A_WIDTH:
                slab = _rope_slab(slab, cos, sin_signed, first_half) * q_scale
            elif COL_A_K <= col < COL_A_K + A_KV_WIDTH:
                slab = _rope_slab(slab, cos, sin_signed, first_half)
            o_ref[:, col:col + LANES] = slab.astype(o_ref.dtype)


def _in_proj(x2, pos2, norm_w, inv_freq_row, w_perm):
    m = x2.shape[0]
    return pl.pallas_call(
        _in_proj_kernel,
        grid=(m // IN_TM,),
        in_specs=[
            pl.BlockSpec((IN_TM, D_MODEL), lambda i: (i, 0)),
            pl.BlockSpec((IN_TM, 1), lambda i: (i, 0)),
            pl.BlockSpec((1, D_MODEL), lambda i: (0, 0)),
            pl.BlockSpec((1, LANES), lambda i: (0, 0)),
            pl.BlockSpec((D_MODEL, PROJ_WIDTH), lambda i: (0, 0), pipeline_mode=pl.Buffered(1)),
        ],
        out_specs=pl.BlockSpec((IN_TM, PROJ_WIDTH), lambda i: (i, 0)),
        out_shape=jax.ShapeDtypeStruct((m, PROJ_WIDTH), jnp.bfloat16),
        compiler_params=pltpu.CompilerParams(dimension_semantics=("arbitrary",),
                                             vmem_limit_bytes=VMEM_LIMIT_BYTES),
        name="in_proj",
    )(x2, pos2, norm_w, inv_freq_row, w_perm)


SWA_ROWS = 2 * BLOCK
SWA_COLS = 4 * 2 * BLOCK


def _swa_bias(sink_ref, g, first_block):
    row = lax.broadcasted_iota(jnp.int32, (SWA_ROWS, SWA_COLS), 0)
    col = lax.broadcasted_iota(jnp.int32, (SWA_ROWS, SWA_COLS), 1)
    qi, sub = row % BLOCK, row // BLOCK
    ki, hi = col % (2 * BLOCK), col // (2 * BLOCK)
    rel = qi + BLOCK - ki
    valid = (rel >= 0) & (rel < WINDOW)
    if first_block:
        valid = valid & (ki >= BLOCK)
    sink = jnp.zeros((SWA_ROWS, SWA_COLS), jnp.float32)
    for s in range(2):
        for i in range(4):
            sink = jnp.where((sub == s) & (hi == i), sink_ref[g * A_GROUP + s * 4 + i], sink)
    return jnp.where(ki == 0, sink, jnp.where(valid, 0.0, NEG_INF))


def _swa_kernel(sink_ref, q_ref, kp_ref, kc_ref, vp_ref, vc_ref, g_ref, o_ref, bias_ref):
    n = pl.program_id(1)

    @pl.when((pl.program_id(0) == 0) & (n == 0))
    def _():
        for g in range(A_KV_HEADS):
            bias_ref[0, g] = _swa_bias(sink_ref, g, False)
            bias_ref[1, g] = _swa_bias(sink_ref, g, True)

    first = (n == 0).astype(jnp.int32)
    bf16 = jnp.bfloat16
    krow = lax.broadcasted_iota(jnp.int32, (2 * BLOCK, A_KV_WIDTH), 0)
    lane = lax.broadcasted_iota(jnp.int32, (2 * BLOCK, A_KV_WIDTH), 1)
    k = jnp.concatenate([kp_ref[...], kc_ref[...]], axis=0)
    v = jnp.concatenate([vp_ref[...], vc_ref[...]], axis=0)
    k = jnp.where(krow == 0, jnp.zeros_like(k), k)
    v = jnp.where(krow == 0, jnp.zeros_like(v), v)
    k_t = k.T
    v_swapped = pltpu.roll(v, A_HEAD_DIM, 1)
    lo = lane < A_HEAD_DIM
    ones_lo = jnp.where(lo, 1.0, 0.0).astype(bf16)
    ones_hi = jnp.where(lo, 0.0, 1.0).astype(bf16)
    zero_kt = jnp.zeros((A_HEAD_DIM, 2 * BLOCK), bf16)
    zero_v = jnp.zeros_like(v)

    for g in range(A_KV_HEADS):
        k_tg = k_t[g * A_HEAD_DIM:(g + 1) * A_HEAD_DIM, :]
        w_k = jnp.concatenate(
            [jnp.concatenate([k_tg if c == i else zero_kt for c in range(4)], axis=1) for i in range(4)],
            axis=0)
        c0 = g * A_GROUP * A_HEAD_DIM
        q_g = jnp.concatenate([q_ref[:, c0:c0 + 256], q_ref[:, c0 + 256:c0 + 512]], axis=0)
        s = jnp.dot(q_g, w_k, preferred_element_type=jnp.float32) + bias_ref[first, g]
        ps = []
        for i in range(4):
            s_i = s[:, i * 2 * BLOCK:(i + 1) * 2 * BLOCK]
            ps.append(jnp.exp(s_i - jnp.max(s_i, axis=-1, keepdims=True)).astype(bf16))
        v_lo = jnp.where(lo, v if g == 0 else v_swapped, zero_v)
        v_hi = jnp.where(lo, zero_v, v_swapped if g == 0 else v)
        w_v = jnp.concatenate([jnp.concatenate([v_lo, ones_lo], axis=1),
                               jnp.concatenate([v_hi, ones_hi], axis=1)], axis=0)
        p = jnp.concatenate([jnp.concatenate([ps[0], ps[1]], axis=1),
                             jnp.concatenate([ps[2], ps[3]], axis=1)], axis=0)
        r = jnp.dot(p, w_v, preferred_element_type=jnp.float32)
        o_n = r[:, :LANES] / r[:, LANES:]
        for pair in range(2):
            for sub in range(2):
                col = c0 + sub * 256 + pair * LANES
                piece = o_n[pair * SWA_ROWS + sub * BLOCK:pair * SWA_ROWS + (sub + 1) * BLOCK, :]
                gate = g_ref[:, col:col + LANES].astype(jnp.float32)
                silu = gate * (0.5 * jnp.tanh(0.5 * gate) + 0.5)
                o_ref[:, col:col + LANES] = (piece * silu).astype(o_ref.dtype)


def _swa(proj, sinks, bsz, t):
    nb = t // BLOCK
    kcol, vcol = COL_A_K // A_KV_WIDTH, COL_A_V // A_KV_WIDTH

    def cur(col):
        return lambda b, n: (b * nb + n, col)

    def prev(col):
        return lambda b, n: (b * nb + jnp.maximum(n - 1, 0), col)

    return pl.pallas_call(
        _swa_kernel,
        grid=(bsz, nb),
        in_specs=[
            pl.BlockSpec(memory_space=pltpu.SMEM),
            pl.BlockSpec((BLOCK, A_WIDTH), cur(COL_A_Q // A_WIDTH)),
            pl.BlockSpec((BLOCK, A_KV_WIDTH), prev(kcol)),
            pl.BlockSpec((BLOCK, A_KV_WIDTH), cur(kcol)),
            pl.BlockSpec((BLOCK, A_KV_WIDTH), prev(vcol)),
            pl.BlockSpec((BLOCK, A_KV_WIDTH), cur(vcol)),
            pl.BlockSpec((BLOCK, A_WIDTH), cur(COL_A_GATE // A_WIDTH)),
        ],
        out_specs=pl.BlockSpec((BLOCK, A_WIDTH), lambda b, n: (b * nb + n, 0)),
        out_shape=jax.ShapeDtypeStruct((bsz * t, A_WIDTH), jnp.bfloat16),
        scratch_shapes=[pltpu.VMEM((2, A_KV_HEADS, SWA_ROWS, SWA_COLS), jnp.float32)],
        compiler_params=pltpu.CompilerParams(dimension_semantics=("arbitrary", "arbitrary"),
                                             vmem_limit_bytes=VMEM_LIMIT_BYTES),
        name="swa",
    )(sinks, proj, proj, proj, proj, proj, proj)


GLA_TG = 256


def _chunk_cumsum_rows(x):
    pos = lax.broadcasted_iota(jnp.int32, x.shape, 0) % CHUNK
    s = 1
    while s < CHUNK:
        x = x + jnp.where(pos >= s, pltpu.roll(x, s, 0), 0.0)
        s *= 2
    return x


def _gla_kernel(q_ref, k_ref, v_ref, g_ref, low_ref, up_ref, bias_ref, nw_ref, o_ref,
                state_ref, qe_ref, ke_ref, ks_ref, dcol_ref, inc_ref, oacc_ref):
    @pl.when(pl.program_id(1) == 0)
    def _():
        state_ref[...] = jnp.zeros_like(state_ref)

    bf16 = jnp.bfloat16
    n_chunks = GLA_TG // CHUNK

    gk = jnp.dot(low_ref[...], up_ref[...], preferred_element_type=jnp.float32) + bias_ref[...]
    log_a = (jnp.minimum(gk, 0.0) - jnp.log1p(jnp.exp(-jnp.abs(gk)))) / GATE_TAU
    b = _chunk_cumsum_rows(log_a)
    b_last_rows = [b[(c + 1) * CHUNK - 1:(c + 1) * CHUNK, :] for c in range(n_chunks)]
    b_last = jnp.concatenate([jnp.broadcast_to(r, (CHUNK, B_QK_WIDTH)) for r in b_last_rows], axis=0)
    qf = q_ref[...].astype(jnp.float32) * (B_KEY_DIM ** -0.5)
    kf = k_ref[...].astype(jnp.float32)
    qe_ref[...] = (qf * jnp.exp(b)).astype(bf16)
    ke_ref[...] = (kf * jnp.exp(-b)).astype(bf16)
    ks_ref[...] = (kf * jnp.exp(b_last - b)).astype(bf16)
    pad = jnp.zeros((LANES - n_chunks, B_QK_WIDTH), jnp.float32)
    dcol_ref[...] = jnp.exp(jnp.concatenate(b_last_rows + [pad], axis=0)).T

    ti = lax.broadcasted_iota(jnp.int32, (GLA_TG, GLA_TG), 0)
    tj = lax.broadcasted_iota(jnp.int32, (GLA_TG, GLA_TG), 1)
    keep = (ti // CHUNK == tj // CHUNK) & (ti >= tj)
    for h in range(B_HEADS):
        ks = slice(h * B_KEY_DIM, (h + 1) * B_KEY_DIM)
        vs = slice(h * B_VAL_DIM, (h + 1) * B_VAL_DIM)
        att = lax.dot_general(qe_ref[:, ks], ke_ref[:, ks], (((1,), (1,)), ((), ())),
                              preferred_element_type=jnp.float32)
        att = jnp.where(keep, att, 0.0).astype(bf16)
        oacc_ref[:, vs] = jnp.dot(att, v_ref[:, vs], preferred_element_type=jnp.float32)
        for c in range(n_chunks):
            rows = slice(c * CHUNK, (c + 1) * CHUNK)
            inc_ref[c, h] = lax.dot_general(ks_ref[rows, ks], v_ref[rows, vs], (((0,), (0,)), ((), ())),
                                            preferred_element_type=jnp.float32)

    for h in range(B_HEADS):
        ks = slice(h * B_KEY_DIM, (h + 1) * B_KEY_DIM)
        vs = slice(h * B_VAL_DIM, (h + 1) * B_VAL_DIM)
        st = state_ref[h]
        for c in range(n_chunks):
            rows = slice(c * CHUNK, (c + 1) * CHUNK)
            oacc_ref[rows, vs] += jnp.dot(qe_ref[rows, ks], st.astype(bf16),
                                          preferred_element_type=jnp.float32)
            st = dcol_ref[ks, c:c + 1] * st + inc_ref[c, h]
        state_ref[h] = st

    for h in range(B_HEADS):
        vs = slice(h * B_VAL_DIM, (h + 1) * B_VAL_DIM)
        o_h = oacc_ref[:, vs]
        var = jnp.mean(o_h * o_h, axis=-1, keepdims=True)
        o_n = o_h * lax.rsqrt(var + EPS) * nw_ref[...]
        gate = g_ref[:, vs].astype(jnp.float32)
        silu = gate * (0.5 * jnp.tanh(0.5 * gate) + 0.5)
        o_ref[:, vs] = (o_n * silu).astype(o_ref.dtype)


def _gla(proj, up_pad, bias, out_norm_w, bsz, t):
    nt = t // GLA_TG

    def col(c):
        return lambda b, n: (b * nt + n, c)

    return pl.pallas_call(
        _gla_kernel,
        grid=(bsz, nt),
        in_specs=[
            pl.BlockSpec((GLA_TG, B_QK_WIDTH), col(COL_B_Q // B_QK_WIDTH)),
            pl.BlockSpec((GLA_TG, B_QK_WIDTH), col(COL_B_K // B_QK_WIDTH)),
            pl.BlockSpec((GLA_TG, B_V_WIDTH), col(COL_B_V // B_V_WIDTH)),
            pl.BlockSpec((GLA_TG, B_V_WIDTH), col(COL_B_GATE // B_V_WIDTH)),
            pl.BlockSpec((GLA_TG, LANES), col(COL_B_LOW // LANES)),
            pl.BlockSpec((LANES, B_QK_WIDTH), lambda b, n: (0, 0)),
            pl.BlockSpec((1, B_QK_WIDTH), lambda b, n: (0, 0)),
            pl.BlockSpec((1, B_VAL_DIM), lambda b, n: (0, 0)),
        ],
        out_specs=pl.BlockSpec((GLA_TG, B_V_WIDTH), lambda b, n: (b * nt + n, 0)),
        out_shape=jax.ShapeDtypeStruct((bsz * t, B_V_WIDTH), jnp.bfloat16),
        scratch_shapes=[pltpu.VMEM((B_HEADS, B_KEY_DIM, B_VAL_DIM), jnp.float32),
                        pltpu.VMEM((GLA_TG, B_QK_WIDTH), jnp.bfloat16),
                        pltpu.VMEM((GLA_TG, B_QK_WIDTH), jnp.bfloat16),
                        pltpu.VMEM((GLA_TG, B_QK_WIDTH), jnp.bfloat16),
                        pltpu.VMEM((B_QK_WIDTH, LANES), jnp.float32),
                        pltpu.VMEM((GLA_TG // CHUNK, B_HEADS, B_KEY_DIM, B_VAL_DIM), jnp.float32),
                        pltpu.VMEM((GLA_TG, B_V_WIDTH), jnp.float32)],
        compiler_params=pltpu.CompilerParams(dimension_semantics=("arbitrary", "arbitrary"),
                                             vmem_limit_bytes=VMEM_LIMIT_BYTES),
        name="gla",
    )(proj, proj, proj, proj, proj, up_pad, bias, out_norm_w)


OUT_TM = 512


def _out_proj_kernel(x_ref, oa_ref, ob_ref, ma_ref, mb_ref, wa_ref, wb_ref, wo_ref, fw_ref, o_ref):
    y_a = jnp.dot(oa_ref[...], wa_ref[...], preferred_element_type=jnp.float32)
    y_b = jnp.dot(ob_ref[...], wb_ref[...], preferred_element_type=jnp.float32)
    merged = (jax.nn.sigmoid(ma_ref[...].astype(jnp.float32)) * y_a
              + jax.nn.sigmoid(mb_ref[...].astype(jnp.float32)) * y_b)
    z = x_ref[...] + jnp.dot(merged.astype(jnp.bfloat16), wo_ref[...],
                             preferred_element_type=jnp.float32)
    var = jnp.mean(z * z, axis=-1, keepdims=True)
    o_ref[...] = z * lax.rsqrt(var + EPS) * fw_ref[...]


def _out_proj(x2, o_a, o_b, proj, wa, wb, wo, final_w):
    m = x2.shape[0]
    row = lambda i: (i, 0)
    const = lambda i: (0, 0)
    return pl.pallas_call(
        _out_proj_kernel,
        grid=(m // OUT_TM,),
        in_specs=[
            pl.BlockSpec((OUT_TM, D_MODEL), row),
            pl.BlockSpec((OUT_TM, A_WIDTH), row),
            pl.BlockSpec((OUT_TM, B_V_WIDTH), row),
            pl.BlockSpec((OUT_TM, D_MODEL), lambda i: (i, COL_M_A // D_MODEL)),
            pl.BlockSpec((OUT_TM, D_MODEL), lambda i: (i, COL_M_B // D_MODEL)),
            pl.BlockSpec((A_WIDTH, D_MODEL), const),
            pl.BlockSpec((B_V_WIDTH, D_MODEL), const),
            pl.BlockSpec((D_MODEL, D_MODEL), const),
            pl.BlockSpec((1, D_MODEL), const),
        ],
        out_specs=pl.BlockSpec((OUT_TM, D_MODEL), row),
        out_shape=jax.ShapeDtypeStruct((m, D_MODEL), jnp.float32),
        compiler_params=pltpu.CompilerParams(dimension_semantics=("arbitrary",),
                                             vmem_limit_bytes=VMEM_LIMIT_BYTES),
        name="out_proj",
    )(x2, o_a, o_b, proj, proj, wa, wb, wo, final_w)


def kernel(x, positions, norm_w, w_in, a_sinks, b_gate_up, b_gate_bias, b_out_norm_w,
           w_a_proj, w_b_proj, w_out, final_norm_w):
    bsz, t, d = x.shape
    assert d == D_MODEL and t % GLA_TG == 0 and t % BLOCK == 0 and (bsz * t) % IN_TM == 0
    depth = norm_w.shape[0]
    bf16 = jnp.bfloat16

    half = A_HEAD_DIM // 2
    inv_freq = ROPE_THETA ** (-jnp.arange(half, dtype=jnp.float32) / half)
    inv_freq_row = jnp.tile(inv_freq, LANES // half)[None, :]
    pos2 = positions.reshape(bsz * t, 1)

    x2 = x.reshape(bsz * t, d)
    for layer in range(depth):
        w_perm = _permute_w_in(w_in[layer]).astype(bf16)
        up_pad = jnp.zeros((LANES, B_QK_WIDTH), bf16).at[:GATE_RANK].set(b_gate_up[layer].astype(bf16))
        proj = _in_proj(x2, pos2, norm_w[layer][None, :], inv_freq_row, w_perm)
        o_a = _swa(proj, a_sinks[layer], bsz, t)
        o_b = _gla(proj, up_pad, b_gate_bias[layer][None, :], b_out_norm_w[layer][None, :], bsz, t)
        z = _out_proj(x2, o_a, o_b, proj, w_a_proj[layer].astype(bf16), w_b_proj[layer].astype(bf16),
                      w_out[layer].astype(bf16), final_norm_w[None, :])
        assert depth == 1
        x2 = z
    return x2.reshape(bsz, t, d)
```

```python
import functools

import jax
import jax.numpy as jnp
from jax import lax
from jax.experimental import pallas as pl
from jax.experimental.pallas import tpu as pltpu

D_MODEL = 1024
A_HEADS = 16
A_KV_HEADS = 2
A_HEAD_DIM = 64
A_GROUP = A_HEADS // A_KV_HEADS
A_WIDTH = A_HEADS * A_HEAD_DIM
A_KV_WIDTH = A_KV_HEADS * A_HEAD_DIM
WINDOW = 128
BLOCK = 128
ROPE_THETA = 10000.0

B_HEADS = 4
B_QK_WIDTH = D_MODEL // 2
B_V_WIDTH = D_MODEL
B_KEY_DIM = B_QK_WIDTH // B_HEADS
B_VAL_DIM = B_V_WIDTH // B_HEADS
GATE_RANK = 16
GATE_TAU = 16.0
CHUNK = 64

EPS = 1e-5
NEG_INF = -1e30

LANES = 128

COL_A_Q = 0
COL_A_GATE = 1024
COL_B_V = 2048
COL_B_GATE = 3072
COL_M_A = 4096
COL_M_B = 5120
COL_B_Q = 6144
COL_B_K = 6656
COL_A_K = 7168
COL_A_V = 7296
COL_B_LOW = 7424
PROJ_WIDTH = 7552

VMEM_LIMIT_BYTES = 60 * 1024 * 1024


IN_WIDTH = 2 * A_WIDTH + 2 * A_KV_WIDTH + 2 * B_QK_WIDTH + 2 * B_V_WIDTH + GATE_RANK + 2 * D_MODEL
_W_IN_PIECES = ((A_WIDTH, COL_A_Q), (A_KV_WIDTH, COL_A_K), (A_KV_WIDTH, COL_A_V), (A_WIDTH, COL_A_GATE),
                (B_QK_WIDTH, COL_B_Q), (B_QK_WIDTH, COL_B_K), (B_V_WIDTH, COL_B_V), (B_V_WIDTH, COL_B_GATE),
                (GATE_RANK, COL_B_LOW), (D_MODEL, COL_M_A), (D_MODEL, COL_M_B))
W_PREP_ROWS = 128


def _w_in_prep_kernel(w_ref, o_ref):
    src = 0
    for width, dst in _W_IN_PIECES:
        o_ref[:, dst:dst + width] = w_ref[:, src:src + width].astype(o_ref.dtype)
        src += width
    pad0 = COL_B_LOW + GATE_RANK
    o_ref[:, pad0:] = jnp.zeros((o_ref.shape[0], PROJ_WIDTH - pad0), o_ref.dtype)


def _w_in_prep(w_in):
    return pl.pallas_call(
        _w_in_prep_kernel,
        grid=(D_MODEL // W_PREP_ROWS,),
        in_specs=[pl.BlockSpec((W_PREP_ROWS, IN_WIDTH), lambda i: (i, 0))],
        out_specs=pl.BlockSpec((W_PREP_ROWS, PROJ_WIDTH), lambda i: (i, 0)),
        out_shape=jax.ShapeDtypeStruct((D_MODEL, PROJ_WIDTH), jnp.bfloat16),
        compiler_params=pltpu.CompilerParams(dimension_semantics=("arbitrary",)),
        name="w_in_prep",
    )(w_in)


def _w_out_prep_kernel(a_ref, b_ref, c_ref, oa_ref, ob_ref, oc_ref):
    oa_ref[...] = a_ref[...].astype(oa_ref.dtype)
    ob_ref[...] = b_ref[...].astype(ob_ref.dtype)
    oc_ref[...] = c_ref[...].astype(oc_ref.dtype)


def _w_out_prep(wa, wb, wo):
    spec = pl.BlockSpec((2 * W_PREP_ROWS, D_MODEL), lambda i: (i, 0))
    shape = jax.ShapeDtypeStruct((D_MODEL, D_MODEL), jnp.bfloat16)
    return pl.pallas_call(
        _w_out_prep_kernel,
        grid=(D_MODEL // (2 * W_PREP_ROWS),),
        in_specs=[spec, spec, spec],
        out_specs=[spec, spec, spec],
        out_shape=[shape, shape, shape],
        compiler_params=pltpu.CompilerParams(dimension_semantics=("arbitrary",)),
        name="w_out_prep",
    )(wa, wb, wo)


IN_TM = 512
IN_CHUNK = 512


def _rope_slab(xs, cos, sin_signed, first_half):
    partner = jnp.where(first_half, pltpu.roll(xs, LANES - A_HEAD_DIM // 2, 1),
                        pltpu.roll(xs, A_HEAD_DIM // 2, 1))
    return xs * cos + partner * sin_signed


def _in_proj_kernel(x_ref, pos_ref, nw_ref, invf_ref, w_ref, o_ref):
    x = x_ref[...]
    var = jnp.mean(x * x, axis=-1, keepdims=True)
    h = (x * lax.rsqrt(var + EPS) * nw_ref[...]).astype(jnp.bfloat16)

    half = A_HEAD_DIM // 2
    n_grp = LANES // half
    qr = x.shape[0] // n_grp
    lane = lax.broadcasted_iota(jnp.int32, (1, LANES), 1)
    grp = lane // half
    pos = pos_ref[...].astype(jnp.float32)
    pos_dense = pos[0:qr]
    for j in range(1, n_grp):
        pos_dense = jnp.where(grp == j, pos[j * qr:(j + 1) * qr], pos_dense)
    ang = pos_dense * invf_ref[...]
    cos_dense = jnp.cos(ang)
    sin_dense = jnp.sin(ang)

    def replicate(dense, j):
        only_j = jnp.where(grp == j, dense, 0.0)
        out = only_j
        for r in range(1, n_grp):
            out = out + pltpu.roll(only_j, r * half, 1)
        return out

    cos = jnp.concatenate([replicate(cos_dense, j) for j in range(n_grp)], axis=0)
    sin = jnp.concatenate([replicate(sin_dense, j) for j in range(n_grp)], axis=0)
    first_half = (lane % A_HEAD_DIM) < half
    sin_signed = jnp.where(first_half, -sin, sin)
    q_scale = A_HEAD_DIM ** -0.5

    for c0 in range(0, PROJ_WIDTH, IN_CHUNK):
        cw = min(IN_CHUNK, PROJ_WIDTH - c0)
        acc = jnp.dot(h, w_ref[:, c0:c0 + cw], preferred_element_type=jnp.float32)
        for s0 in range(0, cw, LANES):
            col = c0 + s0
            slab = acc[:, s0:s0 + LANES]
            if COL_A_Q <= col < COL_A_Q + A_WIDTH:
                slab = _rope_slab(slab, cos, sin_signed, first_half) * q_scale
            elif COL_A_K <= col < COL_A_K + A_KV_WIDTH:
                slab = _rope_slab(slab, cos, sin_signed, first_half)
            o_ref[:, col:col + LANES] = slab.astype(o_ref.dtype)


def _in_proj(x2, pos2, norm_w, inv_freq_row, w_perm):
    m = x2.shape[0]
    return pl.pallas_call(
        _in_proj_kernel,
        grid=(m // IN_TM,),
        in_specs=[
            pl.BlockSpec((IN_TM, D_MODEL), lambda i: (i, 0)),
            pl.BlockSpec((IN_TM, 1), lambda i: (i, 0)),
            pl.BlockSpec((1, D_MODEL), lambda i: (0, 0)),
            pl.BlockSpec((1, LANES), lambda i: (0, 0)),
            pl.BlockSpec((D_MODEL, PROJ_WIDTH), lambda i: (0, 0), pipeline_mode=pl.Buffered(1)),
        ],
        out_specs=pl.BlockSpec((IN_TM, PROJ_WIDTH), lambda i: (i, 0)),
        out_shape=jax.ShapeDtypeStruct((m, PROJ_WIDTH), jnp.bfloat16),
        compiler_params=pltpu.CompilerParams(dimension_semantics=("arbitrary",),
                                             vmem_limit_bytes=VMEM_LIMIT_BYTES),
        name="in_proj",
    )(x2, pos2, norm_w, inv_freq_row, w_perm)


SWA_ROWS = 2 * BLOCK
SWA_COLS = 4 * 2 * BLOCK


def _swa_bias(sink_ref, g, first_block):
    row = lax.broadcasted_iota(jnp.int32, (SWA_ROWS, SWA_COLS), 0)
    col = lax.broadcasted_iota(jnp.int32, (SWA_ROWS, SWA_COLS), 1)
    qi, sub = row % BLOCK, row // BLOCK
    ki, hi = col % (2 * BLOCK), col // (2 * BLOCK)
    rel = qi + BLOCK - ki
    valid = (rel >= 0) & (rel < WINDOW)
    if first_block:
        valid = valid & (ki >= BLOCK)
    sink = jnp.zeros((SWA_ROWS, SWA_COLS), jnp.float32)
    for s in range(2):
        for i in range(4):
            sink = jnp.where((sub == s) & (hi == i), sink_ref[g * A_GROUP + s * 4 + i], sink)
    return jnp.where(ki == 0, sink, jnp.where(valid, 0.0, NEG_INF))


def _swa_kernel(sink_ref, q_ref, kp_ref, kc_ref, vp_ref, vc_ref, g_ref, o_ref, bias_ref):
    n = pl.program_id(1)

    @pl.when((pl.program_id(0) == 0) & (n == 0))
    def _():
        for g in range(A_KV_HEADS):
            bias_ref[0, g] = _swa_bias(sink_ref, g, False)
            bias_ref[1, g] = _swa_bias(sink_ref, g, True)

    first = (n == 0).astype(jnp.int32)
    bf16 = jnp.bfloat16
    krow = lax.broadcasted_iota(jnp.int32, (2 * BLOCK, A_KV_WIDTH), 0)
    lane = lax.broadcasted_iota(jnp.int32, (2 * BLOCK, A_KV_WIDTH), 1)
    k = jnp.concatenate([kp_ref[...], kc_ref[...]], axis=0)
    v = jnp.concatenate([vp_ref[...], vc_ref[...]], axis=0)
    k = jnp.where(krow == 0, jnp.zeros_like(k), k)
    v = jnp.where(krow == 0, jnp.zeros_like(v), v)
    k_t = k.T
    v_swapped = pltpu.roll(v, A_HEAD_DIM, 1)
    lo = lane < A_HEAD_DIM
    ones_lo = jnp.where(lo, 1.0, 0.0).astype(bf16)
    ones_hi = jnp.where(lo, 0.0, 1.0).astype(bf16)
    zero_kt = jnp.zeros((A_HEAD_DIM, 2 * BLOCK), bf16)
    zero_v = jnp.zeros_like(v)

    for g in range(A_KV_HEADS):
        k_tg = k_t[g * A_HEAD_DIM:(g + 1) * A_HEAD_DIM, :]
        w_k = jnp.concatenate(
            [jnp.concatenate([k_tg if c == i else zero_kt for c in range(4)], axis=1) for i in range(4)],
            axis=0)
        c0 = g * A_GROUP * A_HEAD_DIM
        q_g = jnp.concatenate([q_ref[:, c0:c0 + 256], q_ref[:, c0 + 256:c0 + 512]], axis=0)
        s = jnp.dot(q_g, w_k, preferred_element_type=jnp.float32) + bias_ref[first, g]
        ps = []
        for i in range(4):
            s_i = s[:, i * 2 * BLOCK:(i + 1) * 2 * BLOCK]
            ps.append(jnp.exp(s_i - jnp.max(s_i, axis=-1, keepdims=True)).astype(bf16))
        v_lo = jnp.where(lo, v if g == 0 else v_swapped, zero_v)
        v_hi = jnp.where(lo, zero_v, v_swapped if g == 0 else v)
        w_v = jnp.concatenate([jnp.concatenate([v_lo, ones_lo], axis=1),
                               jnp.concatenate([v_hi, ones_hi], axis=1)], axis=0)
        p = jnp.concatenate([jnp.concatenate([ps[0], ps[1]], axis=1),
                             jnp.concatenate([ps[2], ps[3]], axis=1)], axis=0)
        r = jnp.dot(p, w_v, preferred_element_type=jnp.float32)
        o_n = r[:, :LANES] / r[:, LANES:]
        for pair in range(2):
            for sub in range(2):
                col = c0 + sub * 256 + pair * LANES
                piece = o_n[pair * SWA_ROWS + sub * BLOCK:pair * SWA_ROWS + (sub + 1) * BLOCK, :]
                gate = g_ref[:, col:col + LANES].astype(jnp.float32)
                silu = gate * (0.5 * jnp.tanh(0.5 * gate) + 0.5)
                o_ref[:, col:col + LANES] = (piece * silu).astype(o_ref.dtype)


def _swa(proj, sinks, bsz, t):
    nb = t // BLOCK
    kcol, vcol = COL_A_K // A_KV_WIDTH, COL_A_V // A_KV_WIDTH

    def cur(col):
        return lambda b, n: (b * nb + n, col)

    def prev(col):
        return lambda b, n: (b * nb + jnp.maximum(n - 1, 0), col)

    return pl.pallas_call(
        _swa_kernel,
        grid=(bsz, nb),
        in_specs=[
            pl.BlockSpec(memory_space=pltpu.SMEM),
            pl.BlockSpec((BLOCK, A_WIDTH), cur(COL_A_Q // A_WIDTH)),
            pl.BlockSpec((BLOCK, A_KV_WIDTH), prev(kcol)),
            pl.BlockSpec((BLOCK, A_KV_WIDTH), cur(kcol)),
            pl.BlockSpec((BLOCK, A_KV_WIDTH), prev(vcol)),
            pl.BlockSpec((BLOCK, A_KV_WIDTH), cur(vcol)),
            pl.BlockSpec((BLOCK, A_WIDTH), cur(COL_A_GATE // A_WIDTH)),
        ],
        out_specs=pl.BlockSpec((BLOCK, A_WIDTH), lambda b, n: (b * nb + n, 0)),
        out_shape=jax.ShapeDtypeStruct((bsz * t, A_WIDTH), jnp.bfloat16),
        scratch_shapes=[pltpu.VMEM((2, A_KV_HEADS, SWA_ROWS, SWA_COLS), jnp.float32)],
        compiler_params=pltpu.CompilerParams(dimension_semantics=("arbitrary", "arbitrary"),
                                             vmem_limit_bytes=VMEM_LIMIT_BYTES),
        name="swa",
    )(sinks, proj, proj, proj, proj, proj, proj)


GLA_TG = 256


def _chunk_cumsum_rows(x):
    pos = lax.broadcasted_iota(jnp.int32, x.shape, 0) % CHUNK
    s = 1
    while s < CHUNK:
        x = x + jnp.where(pos >= s, pltpu.roll(x, s, 0), 0.0)
        s *= 2
    return x


def _gla_kernel(q_ref, k_ref, v_ref, g_ref, low_ref, up_ref, bias_ref, nw_ref, o_ref,
                state_ref, qe_ref, ke_ref, ks_ref, dcol_ref, inc_ref, oacc_ref):
    @pl.when(pl.program_id(1) == 0)
    def _():
        state_ref[...] = jnp.zeros_like(state_ref)

    bf16 = jnp.bfloat16
    n_chunks = GLA_TG // CHUNK

    gk = jnp.dot(low_ref[...], up_ref[...], preferred_element_type=jnp.float32) + bias_ref[...]
    log_a = (jnp.minimum(gk, 0.0) - jnp.log1p(jnp.exp(-jnp.abs(gk)))) / GATE_TAU
    b = _chunk_cumsum_rows(log_a)
    b_last_rows = [b[(c + 1) * CHUNK - 1:(c + 1) * CHUNK, :] for c in range(n_chunks)]
    b_last = jnp.concatenate([jnp.broadcast_to(r, (CHUNK, B_QK_WIDTH)) for r in b_last_rows], axis=0)
    qf = q_ref[...].astype(jnp.float32) * (B_KEY_DIM ** -0.5)
    kf = k_ref[...].astype(jnp.float32)
    qe_ref[...] = (qf * jnp.exp(b)).astype(bf16)
    ke_ref[...] = (kf * jnp.exp(-b)).astype(bf16)
    ks_ref[...] = (kf * jnp.exp(b_last - b)).astype(bf16)
    pad = jnp.zeros((LANES - n_chunks, B_QK_WIDTH), jnp.float32)
    dcol_ref[...] = jnp.exp(jnp.concatenate(b_last_rows + [pad], axis=0)).T

    ti = lax.broadcasted_iota(jnp.int32, (GLA_TG, GLA_TG), 0)
    tj = lax.broadcasted_iota(jnp.int32, (GLA_TG, GLA_TG), 1)
    keep = (ti // CHUNK == tj // CHUNK) & (ti >= tj)
    for h in range(B_HEADS):
        ks = slice(h * B_KEY_DIM, (h + 1) * B_KEY_DIM)
        vs = slice(h * B_VAL_DIM, (h + 1) * B_VAL_DIM)
        att = lax.dot_general(qe_ref[:, ks], ke_ref[:, ks], (((1,), (1,)), ((), ())),
                              preferred_element_type=jnp.float32)
        att = jnp.where(keep, att, 0.0).astype(bf16)
        oacc_ref[:, vs] = jnp.dot(att, v_ref[:, vs], preferred_element_type=jnp.float32)
        for c in range(n_chunks):
            rows = slice(c * CHUNK, (c + 1) * CHUNK)
            inc_ref[c, h] = lax.dot_general(ks_ref[rows, ks], v_ref[rows, vs], (((0,), (0,)), ((), ())),
                                            preferred_element_type=jnp.float32)

    for h in range(B_HEADS):
        ks = slice(h * B_KEY_DIM, (h + 1) * B_KEY_DIM)
        vs = slice(h * B_VAL_DIM, (h + 1) * B_VAL_DIM)
        st = state_ref[h]
        for c in range(n_chunks):
            rows = slice(c * CHUNK, (c + 1) * CHUNK)
            oacc_ref[rows, vs] += jnp.dot(qe_ref[rows, ks], st.astype(bf16),
                                          preferred_element_type=jnp.float32)
            st = dcol_ref[ks, c:c + 1] * st + inc_ref[c, h]
        state_ref[h] = st

    for h in range(B_HEADS):
        vs = slice(h * B_VAL_DIM, (h + 1) * B_VAL_DIM)
        o_h = oacc_ref[:, vs]
        var = jnp.mean(o_h * o_h, axis=-1, keepdims=True)
        o_n = o_h * lax.rsqrt(var + EPS) * nw_ref[...]
        gate = g_ref[:, vs].astype(jnp.float32)
        silu = gate * (0.5 * jnp.tanh(0.5 * gate) + 0.5)
        o_ref[:, vs] = (o_n * silu).astype(o_ref.dtype)


def _gla(proj, up_pad, bias, out_norm_w, bsz, t):
    nt = t // GLA_TG

    def col(c):
        return lambda b, n: (b * nt + n, c)

    return pl.pallas_call(
        _gla_kernel,
        grid=(bsz, nt),
        in_specs=[
            pl.BlockSpec((GLA_TG, B_QK_WIDTH), col(COL_B_Q // B_QK_WIDTH)),
            pl.BlockSpec((GLA_TG, B_QK_WIDTH), col(COL_B_K // B_QK_WIDTH)),
            pl.BlockSpec((GLA_TG, B_V_WIDTH), col(COL_B_V // B_V_WIDTH)),
            pl.BlockSpec((GLA_TG, B_V_WIDTH), col(COL_B_GATE // B_V_WIDTH)),
            pl.BlockSpec((GLA_TG, LANES), col(COL_B_LOW // LANES)),
            pl.BlockSpec((LANES, B_QK_WIDTH), lambda b, n: (0, 0)),
            pl.BlockSpec((1, B_QK_WIDTH), lambda b, n: (0, 0)),
            pl.BlockSpec((1, B_VAL_DIM), lambda b, n: (0, 0)),
        ],
        out_specs=pl.BlockSpec((GLA_TG, B_V_WIDTH), lambda b, n: (b * nt + n, 0)),
        out_shape=jax.ShapeDtypeStruct((bsz * t, B_V_WIDTH), jnp.bfloat16),
        scratch_shapes=[pltpu.VMEM((B_HEADS, B_KEY_DIM, B_VAL_DIM), jnp.float32),
                        pltpu.VMEM((GLA_TG, B_QK_WIDTH), jnp.bfloat16),
                        pltpu.VMEM((GLA_TG, B_QK_WIDTH), jnp.bfloat16),
                        pltpu.VMEM((GLA_TG, B_QK_WIDTH), jnp.bfloat16),
                        pltpu.VMEM((B_QK_WIDTH, LANES), jnp.float32),
                        pltpu.VMEM((GLA_TG // CHUNK, B_HEADS, B_KEY_DIM, B_VAL_DIM), jnp.float32),
                        pltpu.VMEM((GLA_TG, B_V_WIDTH), jnp.float32)],
        compiler_params=pltpu.CompilerParams(dimension_semantics=("arbitrary", "arbitrary"),
                                             vmem_limit_bytes=VMEM_LIMIT_BYTES),
        name="gla",
    )(proj, proj, proj, proj, proj, up_pad, bias, out_norm_w)


OUT_TM = 512


def _out_proj_kernel(x_ref, oa_ref, ob_ref, ma_ref, mb_ref, wa_ref, wb_ref, wo_ref, fw_ref, o_ref):
    y_a = jnp.dot(oa_ref[...], wa_ref[...], preferred_element_type=jnp.float32)
    y_b = jnp.dot(ob_ref[...], wb_ref[...], preferred_element_type=jnp.float32)
    merged = (jax.nn.sigmoid(ma_ref[...].astype(jnp.float32)) * y_a
              + jax.nn.sigmoid(mb_ref[...].astype(jnp.float32)) * y_b)
    z = x_ref[...] + jnp.dot(merged.astype(jnp.bfloat16), wo_ref[...],
                             preferred_element_type=jnp.float32)
    var = jnp.mean(z * z, axis=-1, keepdims=True)
    o_ref[...] = z * lax.rsqrt(var + EPS) * fw_ref[...]


def _out_proj(x2, o_a, o_b, proj, wa, wb, wo, final_w):
    m = x2.shape[0]
    row = lambda i: (i, 0)
    const = lambda i: (0, 0)
    return pl.pallas_call(
        _out_proj_kernel,
        grid=(m // OUT_TM,),
        in_specs=[
            pl.BlockSpec((OUT_TM, D_MODEL), row),
            pl.BlockSpec((OUT_TM, A_WIDTH), row),
            pl.BlockSpec((OUT_TM, B_V_WIDTH), row),
            pl.BlockSpec((OUT_TM, D_MODEL), lambda i: (i, COL_M_A // D_MODEL)),
            pl.BlockSpec((OUT_TM, D_MODEL), lambda i: (i, COL_M_B // D_MODEL)),
            pl.BlockSpec((A_WIDTH, D_MODEL), const),
            pl.BlockSpec((B_V_WIDTH, D_MODEL), const),
            pl.BlockSpec((D_MODEL, D_MODEL), const),
            pl.BlockSpec((1, D_MODEL), const),
        ],
        out_specs=pl.BlockSpec((OUT_TM, D_MODEL), row),
        out_shape=jax.ShapeDtypeStruct((m, D_MODEL), jnp.float32),
        compiler_params=pltpu.CompilerParams(dimension_semantics=("arbitrary",),
                                             vmem_limit_bytes=VMEM_LIMIT_BYTES),
        name="out_proj",
    )(x2, o_a, o_b, proj, proj, wa, wb, wo, final_w)


def kernel(x, positions, norm_w, w_in, a_sinks, b_gate_up, b_gate_bias, b_out_norm_w,
           w_a_proj, w_b_proj, w_out, final_norm_w):
    bsz, t, d = x.shape
    assert d == D_MODEL and t % GLA_TG == 0 and t % BLOCK == 0 and (bsz * t) % IN_TM == 0
    depth = norm_w.shape[0]
    bf16 = jnp.bfloat16

    half = A_HEAD_DIM // 2
    inv_freq = ROPE_THETA ** (-jnp.arange(half, dtype=jnp.float32) / half)
    inv_freq_row = jnp.tile(inv_freq, LANES // half)[None, :]
    pos2 = positions.reshape(bsz * t, 1)

    x2 = x.reshape(bsz * t, d)
    for layer in range(depth):
        w_perm = _w_in_prep(w_in[layer])
        wa, wb, wo = _w_out_prep(w_a_proj[layer], w_b_proj[layer], w_out[layer])
        up_pad = jnp.zeros((LANES, B_QK_WIDTH), bf16).at[:GATE_RANK].set(b_gate_up[layer].astype(bf16))
        proj = _in_proj(x2, pos2, norm_w[layer][None, :], inv_freq_row, w_perm)
        o_a = _swa(proj, a_sinks[layer], bsz, t)
        o_b = _gla(proj, up_pad, b_gate_bias[layer][None, :], b_out_norm_w[layer][None, :], bsz, t)
        z = _out_proj(x2, o_a, o_b, proj, wa, wb, wo, final_norm_w[None, :])
        assert depth == 1
        x2 = z
    return x2.reshape(bsz, t, d)
```

```python
import functools

import jax
import jax.numpy as jnp
from jax import lax
from jax.experimental import pallas as pl
from jax.experimental.pallas import tpu as pltpu

D_MODEL = 1024
A_HEADS = 16
A_KV_HEADS = 2
A_HEAD_DIM = 64
A_GROUP = A_HEADS // A_KV_HEADS
A_WIDTH = A_HEADS * A_HEAD_DIM
A_KV_WIDTH = A_KV_HEADS * A_HEAD_DIM
WINDOW = 128
BLOCK = 128
ROPE_THETA = 10000.0

B_HEADS = 4
B_QK_WIDTH = D_MODEL // 2
B_V_WIDTH = D_MODEL
B_KEY_DIM = B_QK_WIDTH // B_HEADS
B_VAL_DIM = B_V_WIDTH // B_HEADS
GATE_RANK = 16
GATE_TAU = 16.0
CHUNK = 64

EPS = 1e-5
NEG_INF = -1e30

LANES = 128

COL_A_Q = 0
COL_A_GATE = 1024
COL_B_V = 2048
COL_B_GATE = 3072
COL_M_A = 4096
COL_M_B = 5120
COL_B_Q = 6144
COL_B_K = 6656
COL_A_K = 7168
COL_A_V = 7296
COL_B_LOW = 7424
PROJ_WIDTH = 7552

VMEM_LIMIT_BYTES = 60 * 1024 * 1024


IN_WIDTH = 2 * A_WIDTH + 2 * A_KV_WIDTH + 2 * B_QK_WIDTH + 2 * B_V_WIDTH + GATE_RANK + 2 * D_MODEL
_W_IN_PIECES = ((A_WIDTH, COL_A_Q), (A_KV_WIDTH, COL_A_K), (A_KV_WIDTH, COL_A_V), (A_WIDTH, COL_A_GATE),
                (B_QK_WIDTH, COL_B_Q), (B_QK_WIDTH, COL_B_K), (B_V_WIDTH, COL_B_V), (B_V_WIDTH, COL_B_GATE),
                (GATE_RANK, COL_B_LOW), (D_MODEL, COL_M_A), (D_MODEL, COL_M_B))
W_PREP_ROWS = 128
W_PREP_K = 256


def _w_in_prep_kernel(w_ref, o_ref):
    src = 0
    for width, dst in _W_IN_PIECES:
        o_ref[dst:dst + width, :] = w_ref[src:src + width, :].astype(o_ref.dtype)
        src += width
    pad0 = COL_B_LOW + GATE_RANK
    o_ref[pad0:, :] = jnp.zeros((PROJ_WIDTH - pad0, o_ref.shape[1]), o_ref.dtype)


def _w_in_prep(w_in_t):
    return pl.pallas_call(
        _w_in_prep_kernel,
        grid=(D_MODEL // W_PREP_K,),
        in_specs=[pl.BlockSpec((IN_WIDTH, W_PREP_K), lambda i: (0, i))],
        out_specs=pl.BlockSpec((PROJ_WIDTH, W_PREP_K), lambda i: (0, i)),
        out_shape=jax.ShapeDtypeStruct((PROJ_WIDTH, D_MODEL), jnp.bfloat16),
        compiler_params=pltpu.CompilerParams(dimension_semantics=("arbitrary",),
                                             vmem_limit_bytes=VMEM_LIMIT_BYTES),
        name="w_in_prep",
    )(w_in_t)


def _w_out_prep_kernel(a_ref, b_ref, c_ref, oa_ref, ob_ref, oc_ref):
    oa_ref[...] = a_ref[...].astype(oa_ref.dtype)
    ob_ref[...] = b_ref[...].astype(ob_ref.dtype)
    oc_ref[...] = c_ref[...].astype(oc_ref.dtype)


def _w_out_prep(wa, wb, wo):
    spec = pl.BlockSpec((2 * W_PREP_ROWS, D_MODEL), lambda i: (i, 0))
    shape = jax.ShapeDtypeStruct((D_MODEL, D_MODEL), jnp.bfloat16)
    return pl.pallas_call(
        _w_out_prep_kernel,
        grid=(D_MODEL // (2 * W_PREP_ROWS),),
        in_specs=[spec, spec, spec],
        out_specs=[spec, spec, spec],
        out_shape=[shape, shape, shape],
        compiler_params=pltpu.CompilerParams(dimension_semantics=("arbitrary",)),
        name="w_out_prep",
    )(wa, wb, wo)


IN_TM = 512
IN_CHUNK = 512


def _rope_slab(xs, cos, sin_signed, first_half):
    partner = jnp.where(first_half, pltpu.roll(xs, LANES - A_HEAD_DIM // 2, 1),
                        pltpu.roll(xs, A_HEAD_DIM // 2, 1))
    return xs * cos + partner * sin_signed


def _in_proj_kernel(x_ref, pos_ref, nw_ref, invf_ref, w_ref, o_ref):
    x = x_ref[...]
    var = jnp.mean(x * x, axis=-1, keepdims=True)
    h = (x * lax.rsqrt(var + EPS) * nw_ref[...]).astype(jnp.bfloat16)

    half = A_HEAD_DIM // 2
    n_grp = LANES // half
    qr = x.shape[0] // n_grp
    lane = lax.broadcasted_iota(jnp.int32, (1, LANES), 1)
    grp = lane // half
    pos = pos_ref[...].astype(jnp.float32)
    pos_dense = pos[0:qr]
    for j in range(1, n_grp):
        pos_dense = jnp.where(grp == j, pos[j * qr:(j + 1) * qr], pos_dense)
    ang = pos_dense * invf_ref[...]
    cos_dense = jnp.cos(ang)
    sin_dense = jnp.sin(ang)

    def replicate(dense, j):
        only_j = jnp.where(grp == j, dense, 0.0)
        out = only_j
        for r in range(1, n_grp):
            out = out + pltpu.roll(only_j, r * half, 1)
        return out

    cos = jnp.concatenate([replicate(cos_dense, j) for j in range(n_grp)], axis=0)
    sin = jnp.concatenate([replicate(sin_dense, j) for j in range(n_grp)], axis=0)
    first_half = (lane % A_HEAD_DIM) < half
    sin_signed = jnp.where(first_half, -sin, sin)
    q_scale = A_HEAD_DIM ** -0.5

    for c0 in range(0, PROJ_WIDTH, IN_CHUNK):
        cw = min(IN_CHUNK, PROJ_WIDTH - c0)
        acc = lax.dot_general(h, w_ref[c0:c0 + cw, :], (((1,), (1,)), ((), ())),
                              preferred_element_type=jnp.float32)
        for s0 in range(0, cw, LANES):
            col = c0 + s0
            slab = acc[:, s0:s0 + LANES]
            if COL_A_Q <= col < COL_A_Q + A_WIDTH:
                slab = _rope_slab(slab, cos, sin_signed, first_half) * q_scale
            elif COL_A_K <= col < COL_A_K + A_KV_WIDTH:
                slab = _rope_slab(slab, cos, sin_signed, first_half)
            o_ref[:, col:col + LANES] = slab.astype(o_ref.dtype)


def _in_proj(x2, pos2, norm_w, inv_freq_row, w_perm):
    m = x2.shape[0]
    return pl.pallas_call(
        _in_proj_kernel,
        grid=(m // IN_TM,),
        in_specs=[
            pl.BlockSpec((IN_TM, D_MODEL), lambda i: (i, 0)),
            pl.BlockSpec((IN_TM, 1), lambda i: (i, 0)),
            pl.BlockSpec((1, D_MODEL), lambda i: (0, 0)),
            pl.BlockSpec((1, LANES), lambda i: (0, 0)),
            pl.BlockSpec((PROJ_WIDTH, D_MODEL), lambda i: (0, 0), pipeline_mode=pl.Buffered(1)),
        ],
        out_specs=pl.BlockSpec((IN_TM, PROJ_WIDTH), lambda i: (i, 0)),
        out_shape=jax.ShapeDtypeStruct((m, PROJ_WIDTH), jnp.bfloat16),
        compiler_params=pltpu.CompilerParams(dimension_semantics=("arbitrary",),
                                             vmem_limit_bytes=VMEM_LIMIT_BYTES),
        name="in_proj",
    )(x2, pos2, norm_w, inv_freq_row, w_perm)


SWA_ROWS = 2 * BLOCK
SWA_COLS = 4 * 2 * BLOCK


def _swa_bias(sink_ref, g, first_block):
    row = lax.broadcasted_iota(jnp.int32, (SWA_ROWS, SWA_COLS), 0)
    col = lax.broadcasted_iota(jnp.int32, (SWA_ROWS, SWA_COLS), 1)
    qi, sub = row % BLOCK, row // BLOCK
    ki, hi = col % (2 * BLOCK), col // (2 * BLOCK)
    rel = qi + BLOCK - ki
    valid = (rel >= 0) & (rel < WINDOW)
    if first_block:
        valid = valid & (ki >= BLOCK)
    sink = jnp.zeros((SWA_ROWS, SWA_COLS), jnp.float32)
    for s in range(2):
        for i in range(4):
            sink = jnp.where((sub == s) & (hi == i), sink_ref[g * A_GROUP + s * 4 + i], sink)
    return jnp.where(ki == 0, sink, jnp.where(valid, 0.0, NEG_INF))


def _swa_kernel(sink_ref, q_ref, kp_ref, kc_ref, vp_ref, vc_ref, g_ref, o_ref, bias_ref):
    n = pl.program_id(1)

    @pl.when((pl.program_id(0) == 0) & (n == 0))
    def _():
        for g in range(A_KV_HEADS):
            bias_ref[0, g] = _swa_bias(sink_ref, g, False)
            bias_ref[1, g] = _swa_bias(sink_ref, g, True)

    first = (n == 0).astype(jnp.int32)
    bf16 = jnp.bfloat16
    krow = lax.broadcasted_iota(jnp.int32, (2 * BLOCK, A_KV_WIDTH), 0)
    lane = lax.broadcasted_iota(jnp.int32, (2 * BLOCK, A_KV_WIDTH), 1)
    k = jnp.concatenate([kp_ref[...], kc_ref[...]], axis=0)
    v = jnp.concatenate([vp_ref[...], vc_ref[...]], axis=0)
    k = jnp.where(krow == 0, jnp.zeros_like(k), k)
    v = jnp.where(krow == 0, jnp.zeros_like(v), v)
    k_t = k.T
    v_swapped = pltpu.roll(v, A_HEAD_DIM, 1)
    lo = lane < A_HEAD_DIM
    ones_lo = jnp.where(lo, 1.0, 0.0).astype(bf16)
    ones_hi = jnp.where(lo, 0.0, 1.0).astype(bf16)
    zero_kt = jnp.zeros((A_HEAD_DIM, 2 * BLOCK), bf16)
    zero_v = jnp.zeros_like(v)

    for g in range(A_KV_HEADS):
        k_tg = k_t[g * A_HEAD_DIM:(g + 1) * A_HEAD_DIM, :]
        w_k = jnp.concatenate(
            [jnp.concatenate([k_tg if c == i else zero_kt for c in range(4)], axis=1) for i in range(4)],
            axis=0)
        c0 = g * A_GROUP * A_HEAD_DIM
        q_g = jnp.concatenate([q_ref[:, c0:c0 + 256], q_ref[:, c0 + 256:c0 + 512]], axis=0)
        s = jnp.dot(q_g, w_k, preferred_element_type=jnp.float32) + bias_ref[first, g]
        ps = []
        for i in range(4):
            s_i = s[:, i * 2 * BLOCK:(i + 1) * 2 * BLOCK]
            ps.append(jnp.exp(s_i - jnp.max(s_i, axis=-1, keepdims=True)).astype(bf16))
        v_lo = jnp.where(lo, v if g == 0 else v_swapped, zero_v)
        v_hi = jnp.where(lo, zero_v, v_swapped if g == 0 else v)
        w_v = jnp.concatenate([jnp.concatenate([v_lo, ones_lo], axis=1),
                               jnp.concatenate([v_hi, ones_hi], axis=1)], axis=0)
        p = jnp.concatenate([jnp.concatenate([ps[0], ps[1]], axis=1),
                             jnp.concatenate([ps[2], ps[3]], axis=1)], axis=0)
        r = jnp.dot(p, w_v, preferred_element_type=jnp.float32)
        o_n = r[:, :LANES] / r[:, LANES:]
        for pair in range(2):
            for sub in range(2):
                col = c0 + sub * 256 + pair * LANES
                piece = o_n[pair * SWA_ROWS + sub * BLOCK:pair * SWA_ROWS + (sub + 1) * BLOCK, :]
                gate = g_ref[:, col:col + LANES].astype(jnp.float32)
                silu = gate * (0.5 * jnp.tanh(0.5 * gate) + 0.5)
                o_ref[:, col:col + LANES] = (piece * silu).astype(o_ref.dtype)


def _swa(proj, sinks, bsz, t):
    nb = t // BLOCK
    kcol, vcol = COL_A_K // A_KV_WIDTH, COL_A_V // A_KV_WIDTH

    def cur(col):
        return lambda b, n: (b * nb + n, col)

    def prev(col):
        return lambda b, n: (b * nb + jnp.maximum(n - 1, 0), col)

    return pl.pallas_call(
        _swa_kernel,
        grid=(bsz, nb),
        in_specs=[
            pl.BlockSpec(memory_space=pltpu.SMEM),
            pl.BlockSpec((BLOCK, A_WIDTH), cur(COL_A_Q // A_WIDTH)),
            pl.BlockSpec((BLOCK, A_KV_WIDTH), prev(kcol)),
            pl.BlockSpec((BLOCK, A_KV_WIDTH), cur(kcol)),
            pl.BlockSpec((BLOCK, A_KV_WIDTH), prev(vcol)),
            pl.BlockSpec((BLOCK, A_KV_WIDTH), cur(vcol)),
            pl.BlockSpec((BLOCK, A_WIDTH), cur(COL_A_GATE // A_WIDTH)),
        ],
        out_specs=pl.BlockSpec((BLOCK, A_WIDTH), lambda b, n: (b * nb + n, 0)),
        out_shape=jax.ShapeDtypeStruct((bsz * t, A_WIDTH), jnp.bfloat16),
        scratch_shapes=[pltpu.VMEM((2, A_KV_HEADS, SWA_ROWS, SWA_COLS), jnp.float32)],
        compiler_params=pltpu.CompilerParams(dimension_semantics=("arbitrary", "arbitrary"),
                                             vmem_limit_bytes=VMEM_LIMIT_BYTES),
        name="swa",
    )(sinks, proj, proj, proj, proj, proj, proj)


GLA_TG = 256


def _chunk_cumsum_rows(x):
    pos = lax.broadcasted_iota(jnp.int32, x.shape, 0) % CHUNK
    s = 1
    while s < CHUNK:
        x = x + jnp.where(pos >= s, pltpu.roll(x, s, 0), 0.0)
        s *= 2
    return x


def _gla_kernel(q_ref, k_ref, v_ref, g_ref, low_ref, up_ref, bias_ref, nw_ref, o_ref,
                state_ref, qe_ref, ke_ref, ks_ref, dcol_ref, inc_ref, oacc_ref):
    @pl.when(pl.program_id(1) == 0)
    def _():
        state_ref[...] = jnp.zeros_like(state_ref)

    bf16 = jnp.bfloat16
    n_chunks = GLA_TG // CHUNK

    gk = jnp.dot(low_ref[...], up_ref[...], preferred_element_type=jnp.float32) + bias_ref[...]
    log_a = (jnp.minimum(gk, 0.0) - jnp.log1p(jnp.exp(-jnp.abs(gk)))) / GATE_TAU
    b = _chunk_cumsum_rows(log_a)
    b_last_rows = [b[(c + 1) * CHUNK - 1:(c + 1) * CHUNK, :] for c in range(n_chunks)]
    b_last = jnp.concatenate([jnp.broadcast_to(r, (CHUNK, B_QK_WIDTH)) for r in b_last_rows], axis=0)
    qf = q_ref[...].astype(jnp.float32) * (B_KEY_DIM ** -0.5)
    kf = k_ref[...].astype(jnp.float32)
    qe_ref[...] = (qf * jnp.exp(b)).astype(bf16)
    ke_ref[...] = (kf * jnp.exp(-b)).astype(bf16)
    ks_ref[...] = (kf * jnp.exp(b_last - b)).astype(bf16)
    pad = jnp.zeros((LANES - n_chunks, B_QK_WIDTH), jnp.float32)
    dcol_ref[...] = jnp.exp(jnp.concatenate(b_last_rows + [pad], axis=0)).T

    ti = lax.broadcasted_iota(jnp.int32, (GLA_TG, GLA_TG), 0)
    tj = lax.broadcasted_iota(jnp.int32, (GLA_TG, GLA_TG), 1)
    keep = (ti // CHUNK == tj // CHUNK) & (ti >= tj)
    for h in range(B_HEADS):
        ks = slice(h * B_KEY_DIM, (h + 1) * B_KEY_DIM)
        vs = slice(h * B_VAL_DIM, (h + 1) * B_VAL_DIM)
        att = lax.dot_general(qe_ref[:, ks], ke_ref[:, ks], (((1,), (1,)), ((), ())),
                              preferred_element_type=jnp.float32)
        att = jnp.where(keep, att, 0.0).astype(bf16)
        oacc_ref[:, vs] = jnp.dot(att, v_ref[:, vs], preferred_element_type=jnp.float32)
        for c in range(n_chunks):
            rows = slice(c * CHUNK, (c + 1) * CHUNK)
            inc_ref[c, h] = lax.dot_general(ks_ref[rows, ks], v_ref[rows, vs], (((0,), (0,)), ((), ())),
                                            preferred_element_type=jnp.float32)

    for h in range(B_HEADS):
        ks = slice(h * B_KEY_DIM, (h + 1) * B_KEY_DIM)
        vs = slice(h * B_VAL_DIM, (h + 1) * B_VAL_DIM)
        st = state_ref[h]
        for c in range(n_chunks):
            rows = slice(c * CHUNK, (c + 1) * CHUNK)
            oacc_ref[rows, vs] += jnp.dot(qe_ref[rows, ks], st.astype(bf16),
                                          preferred_element_type=jnp.float32)
            st = dcol_ref[ks, c:c + 1] * st + inc_ref[c, h]
        state_ref[h] = st

    for h in range(B_HEADS):
        vs = slice(h * B_VAL_DIM, (h + 1) * B_VAL_DIM)
        o_h = oacc_ref[:, vs]
        var = jnp.mean(o_h * o_h, axis=-1, keepdims=True)
        o_n = o_h * lax.rsqrt(var + EPS) * nw_ref[...]
        gate = g_ref[:, vs].astype(jnp.float32)
        silu = gate * (0.5 * jnp.tanh(0.5 * gate) + 0.5)
        o_ref[:, vs] = (o_n * silu).astype(o_ref.dtype)


def _gla(proj, up_pad, bias, out_norm_w, bsz, t):
    nt = t // GLA_TG

    def col(c):
        return lambda b, n: (b * nt + n, c)

    return pl.pallas_call(
        _gla_kernel,
        grid=(bsz, nt),
        in_specs=[
            pl.BlockSpec((GLA_TG, B_QK_WIDTH), col(COL_B_Q // B_QK_WIDTH)),
            pl.BlockSpec((GLA_TG, B_QK_WIDTH), col(COL_B_K // B_QK_WIDTH)),
            pl.BlockSpec((GLA_TG, B_V_WIDTH), col(COL_B_V // B_V_WIDTH)),
            pl.BlockSpec((GLA_TG, B_V_WIDTH), col(COL_B_GATE // B_V_WIDTH)),
            pl.BlockSpec((GLA_TG, LANES), col(COL_B_LOW // LANES)),
            pl.BlockSpec((LANES, B_QK_WIDTH), lambda b, n: (0, 0)),
            pl.BlockSpec((1, B_QK_WIDTH), lambda b, n: (0, 0)),
            pl.BlockSpec((1, B_VAL_DIM), lambda b, n: (0, 0)),
        ],
        out_specs=pl.BlockSpec((GLA_TG, B_V_WIDTH), lambda b, n: (b * nt + n, 0)),
        out_shape=jax.ShapeDtypeStruct((bsz * t, B_V_WIDTH), jnp.bfloat16),
        scratch_shapes=[pltpu.VMEM((B_HEADS, B_KEY_DIM, B_VAL_DIM), jnp.float32),
                        pltpu.VMEM((GLA_TG, B_QK_WIDTH), jnp.bfloat16),
                        pltpu.VMEM((GLA_TG, B_QK_WIDTH), jnp.bfloat16),
                        pltpu.VMEM((GLA_TG, B_QK_WIDTH), jnp.bfloat16),
                        pltpu.VMEM((B_QK_WIDTH, LANES), jnp.float32),
                        pltpu.VMEM((GLA_TG // CHUNK, B_HEADS, B_KEY_DIM, B_VAL_DIM), jnp.float32),
                        pltpu.VMEM((GLA_TG, B_V_WIDTH), jnp.float32)],
        compiler_params=pltpu.CompilerParams(dimension_semantics=("arbitrary", "arbitrary"),
                                             vmem_limit_bytes=VMEM_LIMIT_BYTES),
        name="gla",
    )(proj, proj, proj, proj, proj, up_pad, bias, out_norm_w)


OUT_TM = 512


def _out_proj_kernel(x_ref, oa_ref, ob_ref, ma_ref, mb_ref, wa_ref, wb_ref, wo_ref, fw_ref, o_ref):
    y_a = jnp.dot(oa_ref[...], wa_ref[...], preferred_element_type=jnp.float32)
    y_b = jnp.dot(ob_ref[...], wb_ref[...], preferred_element_type=jnp.float32)
    merged = (jax.nn.sigmoid(ma_ref[...].astype(jnp.float32)) * y_a
              + jax.nn.sigmoid(mb_ref[...].astype(jnp.float32)) * y_b)
    z = x_ref[...] + jnp.dot(merged.astype(jnp.bfloat16), wo_ref[...],
                             preferred_element_type=jnp.float32)
    var = jnp.mean(z * z, axis=-1, keepdims=True)
    o_ref[...] = z * lax.rsqrt(var + EPS) * fw_ref[...]


def _out_proj(x2, o_a, o_b, proj, wa, wb, wo, final_w):
    m = x2.shape[0]
    row = lambda i: (i, 0)
    const = lambda i: (0, 0)
    return pl.pallas_call(
        _out_proj_kernel,
        grid=(m // OUT_TM,),
        in_specs=[
            pl.BlockSpec((OUT_TM, D_MODEL), row),
            pl.BlockSpec((OUT_TM, A_WIDTH), row),
            pl.BlockSpec((OUT_TM, B_V_WIDTH), row),
            pl.BlockSpec((OUT_TM, D_MODEL), lambda i: (i, COL_M_A // D_MODEL)),
            pl.BlockSpec((OUT_TM, D_MODEL), lambda i: (i, COL_M_B // D_MODEL)),
            pl.BlockSpec((A_WIDTH, D_MODEL), const),
            pl.BlockSpec((B_V_WIDTH, D_MODEL), const),
            pl.BlockSpec((D_MODEL, D_MODEL), const),
            pl.BlockSpec((1, D_MODEL), const),
        ],
        out_specs=pl.BlockSpec((OUT_TM, D_MODEL), row),
        out_shape=jax.ShapeDtypeStruct((m, D_MODEL), jnp.float32),
        compiler_params=pltpu.CompilerParams(dimension_semantics=("arbitrary",),
                                             vmem_limit_bytes=VMEM_LIMIT_BYTES),
        name="out_proj",
    )(x2, o_a, o_b, proj, proj, wa, wb, wo, final_w)


def kernel(x, positions, norm_w, w_in, a_sinks, b_gate_up, b_gate_bias, b_out_norm_w,
           w_a_proj, w_b_proj, w_out, final_norm_w):
    bsz, t, d = x.shape
    assert d == D_MODEL and t % GLA_TG == 0 and t % BLOCK == 0 and (bsz * t) % IN_TM == 0
    depth = norm_w.shape[0]
    bf16 = jnp.bfloat16

    half = A_HEAD_DIM // 2
    inv_freq = ROPE_THETA ** (-jnp.arange(half, dtype=jnp.float32) / half)
    inv_freq_row = jnp.tile(inv_freq, LANES // half)[None, :]
    pos2 = positions.reshape(bsz * t, 1)

    x2 = x.reshape(bsz * t, d)
    for layer in range(depth):
        w_perm = _w_in_prep(w_in[layer].T)
        wa, wb, wo = _w_out_prep(w_a_proj[layer], w_b_proj[layer], w_out[layer])
        up_pad = jnp.zeros((LANES, B_QK_WIDTH), bf16).at[:GATE_RANK].set(b_gate_up[layer].astype(bf16))
        proj = _in_proj(x2, pos2, norm_w[layer][None, :], inv_freq_row, w_perm)
        o_a = _swa(proj, a_sinks[layer], bsz, t)
        o_b = _gla(proj, up_pad, b_gate_bias[layer][None, :], b_out_norm_w[layer][None, :], bsz, t)
        z = _out_proj(x2, o_a, o_b, proj, wa, wb, wo, final_norm_w[None, :])
        assert depth == 1
        x2 = z
    return x2.reshape(bsz, t, d)
```

```python
import functools

import jax
import jax.numpy as jnp
from jax import lax
from jax.experimental import pallas as pl
from jax.experimental.pallas import tpu as pltpu

D_MODEL = 1024
A_HEADS = 16
A_KV_HEADS = 2
A_HEAD_DIM = 64
A_GROUP = A_HEADS // A_KV_HEADS
A_WIDTH = A_HEADS * A_HEAD_DIM
A_KV_WIDTH = A_KV_HEADS * A_HEAD_DIM
WINDOW = 128
BLOCK = 128
ROPE_THETA = 10000.0

B_HEADS = 4
B_QK_WIDTH = D_MODEL // 2
B_V_WIDTH = D_MODEL
B_KEY_DIM = B_QK_WIDTH // B_HEADS
B_VAL_DIM = B_V_WIDTH // B_HEADS
GATE_RANK = 16
GATE_TAU = 16.0
CHUNK = 64

EPS = 1e-5
NEG_INF = -1e30

LANES = 128

COL_A_Q = 0
COL_A_GATE = 1024
COL_B_V = 2048
COL_B_GATE = 3072
COL_B_Q = 4096
COL_B_K = 4608
COL_A_K = 5120
COL_A_V = 5248
COL_B_LOW = 5376
MIX_COLS = 5504
COL_M_A = MIX_COLS
COL_M_B = MIX_COLS + D_MODEL
PROJ_WIDTH = MIX_COLS + 2 * D_MODEL

VMEM_LIMIT_BYTES = 60 * 1024 * 1024


IN_WIDTH = 2 * A_WIDTH + 2 * A_KV_WIDTH + 2 * B_QK_WIDTH + 2 * B_V_WIDTH + GATE_RANK + 2 * D_MODEL
_W_IN_PIECES = ((A_WIDTH, COL_A_Q), (A_KV_WIDTH, COL_A_K), (A_KV_WIDTH, COL_A_V), (A_WIDTH, COL_A_GATE),
                (B_QK_WIDTH, COL_B_Q), (B_QK_WIDTH, COL_B_K), (B_V_WIDTH, COL_B_V), (B_V_WIDTH, COL_B_GATE),
                (GATE_RANK, COL_B_LOW), (D_MODEL, COL_M_A), (D_MODEL, COL_M_B))
W_PREP_ROWS = 128
W_PREP_K = 256


def _w_in_prep_kernel(w_ref, o_ref):
    src = 0
    for width, dst in _W_IN_PIECES:
        o_ref[dst:dst + width, :] = w_ref[src:src + width, :].astype(o_ref.dtype)
        src += width
    pad0 = COL_B_LOW + GATE_RANK
    o_ref[pad0:MIX_COLS, :] = jnp.zeros((MIX_COLS - pad0, o_ref.shape[1]), o_ref.dtype)


def _w_in_prep(w_in_t):
    return pl.pallas_call(
        _w_in_prep_kernel,
        grid=(D_MODEL // W_PREP_K,),
        in_specs=[pl.BlockSpec((IN_WIDTH, W_PREP_K), lambda i: (0, i))],
        out_specs=pl.BlockSpec((PROJ_WIDTH, W_PREP_K), lambda i: (0, i)),
        out_shape=jax.ShapeDtypeStruct((PROJ_WIDTH, D_MODEL), jnp.bfloat16),
        compiler_params=pltpu.CompilerParams(dimension_semantics=("arbitrary",),
                                             vmem_limit_bytes=VMEM_LIMIT_BYTES),
        name="w_in_prep",
    )(w_in_t)


def _w_out_prep_kernel(a_ref, b_ref, c_ref, oa_ref, ob_ref, oc_ref):
    oa_ref[...] = a_ref[...].astype(oa_ref.dtype)
    ob_ref[...] = b_ref[...].astype(ob_ref.dtype)
    oc_ref[...] = c_ref[...].astype(oc_ref.dtype)


def _w_out_prep(wa, wb, wo):
    spec = pl.BlockSpec((2 * W_PREP_ROWS, D_MODEL), lambda i: (i, 0))
    shape = jax.ShapeDtypeStruct((D_MODEL, D_MODEL), jnp.bfloat16)
    return pl.pallas_call(
        _w_out_prep_kernel,
        grid=(D_MODEL // (2 * W_PREP_ROWS),),
        in_specs=[spec, spec, spec],
        out_specs=[spec, spec, spec],
        out_shape=[shape, shape, shape],
        compiler_params=pltpu.CompilerParams(dimension_semantics=("arbitrary",)),
        name="w_out_prep",
    )(wa, wb, wo)


TILE = 256
IN_CHUNK = 512


def _silu(x):
    return x * (0.5 * jnp.tanh(0.5 * x) + 0.5)


def _rope_slab(xs, cos, sin_signed, first_half):
    partner = jnp.where(first_half, pltpu.roll(xs, LANES - A_HEAD_DIM // 2, 1),
                        pltpu.roll(xs, A_HEAD_DIM // 2, 1))
    return xs * cos + partner * sin_signed


def _project_tile(x_ref, pos_ref, nw_ref, invf_ref, w_ref, mix_ref, mg_ref):
    x = x_ref[...]
    var = jnp.mean(x * x, axis=-1, keepdims=True)
    h = (x * lax.rsqrt(var + EPS) * nw_ref[...]).astype(jnp.bfloat16)
    yield 300

    half = A_HEAD_DIM // 2
    n_grp = LANES // half
    qr = x.shape[0] // n_grp
    lane = lax.broadcasted_iota(jnp.int32, (1, LANES), 1)
    grp = lane // half
    pos = pos_ref[...].astype(jnp.float32)
    pos_dense = pos[0:qr]
    for j in range(1, n_grp):
        pos_dense = jnp.where(grp == j, pos[j * qr:(j + 1) * qr], pos_dense)
    ang = pos_dense * invf_ref[...]
    cos_dense = jnp.cos(ang)
    sin_dense = jnp.sin(ang)

    def replicate(dense, j):
        only_j = jnp.where(grp == j, dense, 0.0)
        out = only_j
        for r in range(1, n_grp):
            out = out + pltpu.roll(only_j, r * half, 1)
        return out

    cos = jnp.concatenate([replicate(cos_dense, j) for j in range(n_grp)], axis=0)
    sin = jnp.concatenate([replicate(sin_dense, j) for j in range(n_grp)], axis=0)
    first_half = (lane % A_HEAD_DIM) < half
    sin_signed = jnp.where(first_half, -sin, sin)
    q_scale = A_HEAD_DIM ** -0.5
    yield 300

    for c0 in range(0, PROJ_WIDTH, IN_CHUNK):
        cw = min(IN_CHUNK, PROJ_WIDTH - c0)
        acc = lax.dot_general(h, w_ref[c0:c0 + cw, :], (((1,), (1,)), ((), ())),
                              preferred_element_type=jnp.float32)
        for s0 in range(0, cw, LANES):
            col = c0 + s0
            slab = acc[:, s0:s0 + LANES]
            if COL_A_Q <= col < COL_A_Q + A_WIDTH:
                slab = _rope_slab(slab, cos, sin_signed, first_half) * q_scale
            elif COL_A_K <= col < COL_A_K + A_KV_WIDTH:
                slab = _rope_slab(slab, cos, sin_signed, first_half)
            if col < MIX_COLS:
                mix_ref[:, col:col + LANES] = slab.astype(mix_ref.dtype)
            else:
                mg_ref[:, col - MIX_COLS:col - MIX_COLS + LANES] = slab.astype(mg_ref.dtype)
        yield cw


SWA_ROWS = 2 * BLOCK
SWA_COLS = 4 * 2 * BLOCK


def _swa_bias(sink_ref, g, first_block):
    row = lax.broadcasted_iota(jnp.int32, (SWA_ROWS, SWA_COLS), 0)
    col = lax.broadcasted_iota(jnp.int32, (SWA_ROWS, SWA_COLS), 1)
    qi, sub = row % BLOCK, row // BLOCK
    ki, hi = col % (2 * BLOCK), col // (2 * BLOCK)
    rel = qi + BLOCK - ki
    valid = (rel >= 0) & (rel < WINDOW)
    if first_block:
        valid = valid & (ki >= BLOCK)
    sink = jnp.zeros((SWA_ROWS, SWA_COLS), jnp.float32)
    for s in range(2):
        for i in range(4):
            sink = jnp.where((sub == s) & (hi == i), sink_ref[g * A_GROUP + s * 4 + i], sink)
    return jnp.where(ki == 0, sink, jnp.where(valid, 0.0, NEG_INF))


def _swa_block(mix_ref, r0, k_prev, v_prev, first, bias_ref, oa_ref):
    bf16 = jnp.bfloat16
    rows = slice(r0, r0 + BLOCK)
    krow = lax.broadcasted_iota(jnp.int32, (2 * BLOCK, A_KV_WIDTH), 0)
    lane = lax.broadcasted_iota(jnp.int32, (2 * BLOCK, A_KV_WIDTH), 1)
    k = jnp.concatenate([k_prev, mix_ref[rows, COL_A_K:COL_A_K + A_KV_WIDTH]], axis=0)
    v = jnp.concatenate([v_prev, mix_ref[rows, COL_A_V:COL_A_V + A_KV_WIDTH]], axis=0)
    k = jnp.where(krow == 0, jnp.zeros_like(k), k)
    v = jnp.where(krow == 0, jnp.zeros_like(v), v)
    k_t = k.T
    v_swapped = pltpu.roll(v, A_HEAD_DIM, 1)
    lo = lane < A_HEAD_DIM
    ones_lo = jnp.where(lo, 1.0, 0.0).astype(bf16)
    ones_hi = jnp.where(lo, 0.0, 1.0).astype(bf16)
    zero_kt = jnp.zeros((A_HEAD_DIM, 2 * BLOCK), bf16)
    zero_v = jnp.zeros_like(v)

    for g in range(A_KV_HEADS):
        k_tg = k_t[g * A_HEAD_DIM:(g + 1) * A_HEAD_DIM, :]
        w_k = jnp.concatenate(
            [jnp.concatenate([k_tg if c == i else zero_kt for c in range(4)], axis=1) for i in range(4)],
            axis=0)
        c0 = COL_A_Q + g * A_GROUP * A_HEAD_DIM
        q_g = jnp.concatenate([mix_ref[rows, c0:c0 + 256], mix_ref[rows, c0 + 256:c0 + 512]], axis=0)
        s = jnp.dot(q_g, w_k, preferred_element_type=jnp.float32) + bias_ref[first, g]
        ps = []
        for i in range(4):
            s_i = s[:, i * 2 * BLOCK:(i + 1) * 2 * BLOCK]
            ps.append(jnp.exp(s_i - jnp.max(s_i, axis=-1, keepdims=True)).astype(bf16))
        yield 550
        v_lo = jnp.where(lo, v if g == 0 else v_swapped, zero_v)
        v_hi = jnp.where(lo, zero_v, v_swapped if g == 0 else v)
        w_v = jnp.concatenate([jnp.concatenate([v_lo, ones_lo], axis=1),
                               jnp.concatenate([v_hi, ones_hi], axis=1)], axis=0)
        p = jnp.concatenate([jnp.concatenate([ps[0], ps[1]], axis=1),
                             jnp.concatenate([ps[2], ps[3]], axis=1)], axis=0)
        r = jnp.dot(p, w_v, preferred_element_type=jnp.float32)
        o_n = r[:, :LANES] / r[:, LANES:]
        for pair in range(2):
            for sub in range(2):
                off = g * A_GROUP * A_HEAD_DIM + sub * 256 + pair * LANES
                piece = o_n[pair * SWA_ROWS + sub * BLOCK:pair * SWA_ROWS + (sub + 1) * BLOCK, :]
                gate = mix_ref[rows, COL_A_GATE + off:COL_A_GATE + off + LANES].astype(jnp.float32)
                oa_ref[rows, off:off + LANES] = (piece * _silu(gate)).astype(oa_ref.dtype)
        yield 400


def _swa_tile(mix_ref, first_tile, kv_prev_ref, bias_ref, oa_ref):
    first = first_tile.astype(jnp.int32)
    k_prev = kv_prev_ref[:, :A_KV_WIDTH]
    v_prev = kv_prev_ref[:, A_KV_WIDTH:]
    for blk in range(TILE // BLOCK):
        r0 = blk * BLOCK
        yield from _swa_block(mix_ref, r0, k_prev, v_prev, first if blk == 0 else 0, bias_ref, oa_ref)
        k_prev = mix_ref[r0:r0 + BLOCK, COL_A_K:COL_A_K + A_KV_WIDTH]
        v_prev = mix_ref[r0:r0 + BLOCK, COL_A_V:COL_A_V + A_KV_WIDTH]
    kv_prev_ref[:, :A_KV_WIDTH] = k_prev
    kv_prev_ref[:, A_KV_WIDTH:] = v_prev


def _chunk_cumsum_rows(x):
    pos = lax.broadcasted_iota(jnp.int32, x.shape, 0) % CHUNK
    s = 1
    while s < CHUNK:
        x = x + jnp.where(pos >= s, pltpu.roll(x, s, 0), 0.0)
        s *= 2
    return x


def _gla_tile(mix_ref, first_tile, up_ref, bias_ref, nw_ref, ob_ref,
              state_ref, qe_ref, ke_ref, ks_ref, dcol_ref, inc_ref, oacc_ref):
    bf16 = jnp.bfloat16
    n_chunks = TILE // CHUNK

    gk = jnp.dot(mix_ref[:, COL_B_LOW:COL_B_LOW + LANES], up_ref[...],
                 preferred_element_type=jnp.float32) + bias_ref[...]
    log_a = (jnp.minimum(gk, 0.0) - jnp.log1p(jnp.exp(-jnp.abs(gk)))) / GATE_TAU
    b = _chunk_cumsum_rows(log_a)
    yield 700
    b_last_rows = [b[(c + 1) * CHUNK - 1:(c + 1) * CHUNK, :] for c in range(n_chunks)]
    b_last = jnp.concatenate([jnp.broadcast_to(r, (CHUNK, B_QK_WIDTH)) for r in b_last_rows], axis=0)
    qf = mix_ref[:, COL_B_Q:COL_B_Q + B_QK_WIDTH].astype(jnp.float32) * (B_KEY_DIM ** -0.5)
    kf = mix_ref[:, COL_B_K:COL_B_K + B_QK_WIDTH].astype(jnp.float32)
    qe_ref[...] = (qf * jnp.exp(b)).astype(bf16)
    ke_ref[...] = (kf * jnp.exp(-b)).astype(bf16)
    ks_ref[...] = (kf * jnp.exp(b_last - b)).astype(bf16)
    pad = jnp.zeros((LANES - n_chunks, B_QK_WIDTH), jnp.float32)
    dcol_ref[...] = jnp.exp(jnp.concatenate(b_last_rows + [pad], axis=0)).T
    yield 700

    ti = lax.broadcasted_iota(jnp.int32, (TILE, TILE), 0)
    tj = lax.broadcasted_iota(jnp.int32, (TILE, TILE), 1)
    keep = (ti // CHUNK == tj // CHUNK) & (ti >= tj)
    for h in range(B_HEADS):
        ks = slice(h * B_KEY_DIM, (h + 1) * B_KEY_DIM)
        vs = slice(COL_B_V + h * B_VAL_DIM, COL_B_V + (h + 1) * B_VAL_DIM)
        os_ = slice(h * B_VAL_DIM, (h + 1) * B_VAL_DIM)
        att = lax.dot_general(qe_ref[:, ks], ke_ref[:, ks], (((1,), (1,)), ((), ())),
                              preferred_element_type=jnp.float32)
        att = jnp.where(keep, att, 0.0).astype(bf16)
        oacc_ref[:, os_] = jnp.dot(att, mix_ref[:, vs], preferred_element_type=jnp.float32)
        for c in range(n_chunks):
            rows = slice(c * CHUNK, (c + 1) * CHUNK)
            inc_ref[c, h] = lax.dot_general(ks_ref[rows, ks], mix_ref[rows, vs], (((0,), (0,)), ((), ())),
                                            preferred_element_type=jnp.float32)
        yield 350

    for h in range(B_HEADS):
        ks = slice(h * B_KEY_DIM, (h + 1) * B_KEY_DIM)
        os_ = slice(h * B_VAL_DIM, (h + 1) * B_VAL_DIM)
        st = jnp.where(first_tile, 0.0, state_ref[h])
        for c in range(n_chunks):
            rows = slice(c * CHUNK, (c + 1) * CHUNK)
            oacc_ref[rows, os_] += jnp.dot(qe_ref[rows, ks], st.astype(bf16),
                                           preferred_element_type=jnp.float32)
            st = dcol_ref[ks, c:c + 1] * st + inc_ref[c, h]
        state_ref[h] = st
        yield 150

    for h in range(B_HEADS):
        os_ = slice(h * B_VAL_DIM, (h + 1) * B_VAL_DIM)
        o_h = oacc_ref[:, os_]
        var = jnp.mean(o_h * o_h, axis=-1, keepdims=True)
        o_n = o_h * lax.rsqrt(var + EPS) * nw_ref[...]
        gate = mix_ref[:, COL_B_GATE + h * B_VAL_DIM:COL_B_GATE + (h + 1) * B_VAL_DIM].astype(jnp.float32)
        ob_ref[:, os_] = (o_n * _silu(gate)).astype(ob_ref.dtype)
        yield 225


def _emit_interleaved(*streams):
    spent = [0.0] * len(streams)
    live = list(range(len(streams)))
    while live:
        k = min(live, key=lambda j: spent[j])
        try:
            spent[k] += next(streams[k])
        except StopIteration:
            live.remove(k)


def _mixer_kernel(tiles_per_seq, sink_ref, x_ref, pos_ref, nw_ref, invf_ref, w_ref, up_ref, gbias_ref,
                  bnw_ref, mg_ref, oa_ref, ob_ref, mix_a, mix_b, kv_prev_ref, sbias_ref,
                  state_ref, qe_ref, ke_ref, ks_ref, dcol_ref, inc_ref, oacc_ref):
    i = pl.program_id(0)
    n = pl.num_programs(0) - 1

    def project(dst):
        yield from _project_tile(x_ref, pos_ref, nw_ref, invf_ref, w_ref, dst, mg_ref)

    def mix(src):
        first_tile = ((i - 1) % tiles_per_seq) == 0
        yield from _gla_tile(src, first_tile, up_ref, gbias_ref, bnw_ref, ob_ref,
                             state_ref, qe_ref, ke_ref, ks_ref, dcol_ref, inc_ref, oacc_ref)
        yield from _swa_tile(src, first_tile, kv_prev_ref, sbias_ref, oa_ref)

    @pl.when(i == 0)
    def _():
        for g in range(A_KV_HEADS):
            sbias_ref[0, g] = _swa_bias(sink_ref, g, False)
            sbias_ref[1, g] = _swa_bias(sink_ref, g, True)
        kv_prev_ref[...] = jnp.zeros_like(kv_prev_ref)
        state_ref[...] = jnp.zeros_like(state_ref)
        _emit_interleaved(project(mix_a))

    @pl.when((i > 0) & (i < n) & (i % 2 == 1))
    def _():
        _emit_interleaved(mix(mix_a), project(mix_b))

    @pl.when((i > 0) & (i < n) & (i % 2 == 0))
    def _():
        _emit_interleaved(mix(mix_b), project(mix_a))

    @pl.when(i == n)
    def _():
        _emit_interleaved(mix(mix_b))


def _mixer(x2, pos2, norm_w, inv_freq_row, w_perm, sinks, up_pad, gate_bias, b_norm_w, tiles_per_seq):
    m = x2.shape[0]
    n = m // TILE
    bf16 = jnp.bfloat16
    cur = lambda i: (jnp.minimum(i, n - 1), 0)
    lag = lambda i: (jnp.maximum(i - 1, 0), 0)
    const = lambda i: (0, 0)
    n_chunks = TILE // CHUNK
    return pl.pallas_call(
        functools.partial(_mixer_kernel, tiles_per_seq),
        grid=(n + 1,),
        in_specs=[
            pl.BlockSpec(memory_space=pltpu.SMEM),
            pl.BlockSpec((TILE, D_MODEL), cur),
            pl.BlockSpec((TILE, 1), cur),
            pl.BlockSpec((1, D_MODEL), const),
            pl.BlockSpec((1, LANES), const),
            pl.BlockSpec((PROJ_WIDTH, D_MODEL), const, pipeline_mode=pl.Buffered(1)),
            pl.BlockSpec((LANES, B_QK_WIDTH), const),
            pl.BlockSpec((1, B_QK_WIDTH), const),
            pl.BlockSpec((1, B_VAL_DIM), const),
        ],
        out_specs=[
            pl.BlockSpec((TILE, 2 * D_MODEL), cur),
            pl.BlockSpec((TILE, A_WIDTH), lag),
            pl.BlockSpec((TILE, B_V_WIDTH), lag),
        ],
        out_shape=[
            jax.ShapeDtypeStruct((m, 2 * D_MODEL), bf16),
            jax.ShapeDtypeStruct((m, A_WIDTH), bf16),
            jax.ShapeDtypeStruct((m, B_V_WIDTH), bf16),
        ],
        scratch_shapes=[
            pltpu.VMEM((TILE, MIX_COLS), bf16),
            pltpu.VMEM((TILE, MIX_COLS), bf16),
            pltpu.VMEM((BLOCK, 2 * A_KV_WIDTH), bf16),
            pltpu.VMEM((2, A_KV_HEADS, SWA_ROWS, SWA_COLS), jnp.float32),
            pltpu.VMEM((B_HEADS, B_KEY_DIM, B_VAL_DIM), jnp.float32),
            pltpu.VMEM((TILE, B_QK_WIDTH), bf16),
            pltpu.VMEM((TILE, B_QK_WIDTH), bf16),
            pltpu.VMEM((TILE, B_QK_WIDTH), bf16),
            pltpu.VMEM((B_QK_WIDTH, LANES), jnp.float32),
            pltpu.VMEM((n_chunks, B_HEADS, B_KEY_DIM, B_VAL_DIM), jnp.float32),
            pltpu.VMEM((TILE, B_V_WIDTH), jnp.float32),
        ],
        compiler_params=pltpu.CompilerParams(dimension_semantics=("arbitrary",),
                                             vmem_limit_bytes=VMEM_LIMIT_BYTES),
        name="mixer",
    )(sinks, x2, pos2, norm_w, inv_freq_row, w_perm, up_pad, gate_bias, b_norm_w)


OUT_TM = 512


def _out_proj_kernel(x_ref, oa_ref, ob_ref, ma_ref, mb_ref, wa_ref, wb_ref, wo_ref, fw_ref, o_ref):
    y_a = jnp.dot(oa_ref[...], wa_ref[...], preferred_element_type=jnp.float32)
    y_b = jnp.dot(ob_ref[...], wb_ref[...], preferred_element_type=jnp.float32)
    merged = (jax.nn.sigmoid(ma_ref[...].astype(jnp.float32)) * y_a
              + jax.nn.sigmoid(mb_ref[...].astype(jnp.float32)) * y_b)
    z = x_ref[...] + jnp.dot(merged.astype(jnp.bfloat16), wo_ref[...],
                             preferred_element_type=jnp.float32)
    var = jnp.mean(z * z, axis=-1, keepdims=True)
    o_ref[...] = z * lax.rsqrt(var + EPS) * fw_ref[...]


def _out_proj(x2, o_a, o_b, mg, wa, wb, wo, final_w):
    m = x2.shape[0]
    row = lambda i: (i, 0)
    const = lambda i: (0, 0)
    return pl.pallas_call(
        _out_proj_kernel,
        grid=(m // OUT_TM,),
        in_specs=[
            pl.BlockSpec((OUT_TM, D_MODEL), row),
            pl.BlockSpec((OUT_TM, A_WIDTH), row),
            pl.BlockSpec((OUT_TM, B_V_WIDTH), row),
            pl.BlockSpec((OUT_TM, D_MODEL), lambda i: (i, 0)),
            pl.BlockSpec((OUT_TM, D_MODEL), lambda i: (i, 1)),
            pl.BlockSpec((A_WIDTH, D_MODEL), const),
            pl.BlockSpec((B_V_WIDTH, D_MODEL), const),
            pl.BlockSpec((D_MODEL, D_MODEL), const),
            pl.BlockSpec((1, D_MODEL), const),
        ],
        out_specs=pl.BlockSpec((OUT_TM, D_MODEL), row),
        out_shape=jax.ShapeDtypeStruct((m, D_MODEL), jnp.float32),
        compiler_params=pltpu.CompilerParams(dimension_semantics=("arbitrary",),
                                             vmem_limit_bytes=VMEM_LIMIT_BYTES),
        name="out_proj",
    )(x2, o_a, o_b, mg, mg, wa, wb, wo, final_w)


def kernel(x, positions, norm_w, w_in, a_sinks, b_gate_up, b_gate_bias, b_out_norm_w,
           w_a_proj, w_b_proj, w_out, final_norm_w):
    bsz, t, d = x.shape
    assert d == D_MODEL and t % (2 * TILE) == 0 and (bsz * t) % OUT_TM == 0
    depth = norm_w.shape[0]
    assert depth == 1
    bf16 = jnp.bfloat16

    half = A_HEAD_DIM // 2
    inv_freq = ROPE_THETA ** (-jnp.arange(half, dtype=jnp.float32) / half)
    inv_freq_row = jnp.tile(inv_freq, LANES // half)[None, :]
    pos2 = positions.reshape(bsz * t, 1)

    x2 = x.reshape(bsz * t, d)
    for layer in range(depth):
        w_perm = _w_in_prep(w_in[layer].T)
        wa, wb, wo = _w_out_prep(w_a_proj[layer], w_b_proj[layer], w_out[layer])
        up_pad = jnp.zeros((LANES, B_QK_WIDTH), bf16).at[:GATE_RANK].set(b_gate_up[layer].astype(bf16))
        mg, o_a, o_b = _mixer(x2, pos2, norm_w[layer][None, :], inv_freq_row, w_perm, a_sinks[layer],
                              up_pad, b_gate_bias[layer][None, :], b_out_norm_w[layer][None, :],
                              t // TILE)
        x2 = _out_proj(x2, o_a, o_b, mg, wa, wb, wo, final_norm_w[None, :])
    return x2.reshape(bsz, t, d)
```

```python
import functools

import jax
import jax.numpy as jnp
from jax import lax
from jax.experimental import pallas as pl
from jax.experimental.pallas import tpu as pltpu

D_MODEL = 1024
A_HEADS = 16
A_KV_HEADS = 2
A_HEAD_DIM = 64
A_GROUP = A_HEADS // A_KV_HEADS
A_WIDTH = A_HEADS * A_HEAD_DIM
A_KV_WIDTH = A_KV_HEADS * A_HEAD_DIM
WINDOW = 128
BLOCK = 128
ROPE_THETA = 10000.0

B_HEADS = 4
B_QK_WIDTH = D_MODEL // 2
B_V_WIDTH = D_MODEL
B_KEY_DIM = B_QK_WIDTH // B_HEADS
B_VAL_DIM = B_V_WIDTH // B_HEADS
GATE_RANK = 16
GATE_TAU = 16.0
CHUNK = 64

EPS = 1e-5
NEG_INF = -1e30
LOG2E = 1.4426950408889634

LANES = 128

COL_A_Q = 0
COL_A_GATE = 1024
COL_B_V = 2048
COL_B_GATE = 3072
COL_B_Q = 4096
COL_B_K = 4608
COL_A_K = 5120
COL_A_V = 5248
COL_B_LOW = 5376
MIX_COLS = 5504
COL_M_A = MIX_COLS
COL_M_B = MIX_COLS + D_MODEL
PROJ_WIDTH = MIX_COLS + 2 * D_MODEL

VMEM_LIMIT_BYTES = 60 * 1024 * 1024


IN_WIDTH = 2 * A_WIDTH + 2 * A_KV_WIDTH + 2 * B_QK_WIDTH + 2 * B_V_WIDTH + GATE_RANK + 2 * D_MODEL
_W_IN_PIECES = ((A_WIDTH, COL_A_Q), (A_KV_WIDTH, COL_A_K), (A_KV_WIDTH, COL_A_V), (A_WIDTH, COL_A_GATE),
                (B_QK_WIDTH, COL_B_Q), (B_QK_WIDTH, COL_B_K), (B_V_WIDTH, COL_B_V), (B_V_WIDTH, COL_B_GATE),
                (GATE_RANK, COL_B_LOW), (D_MODEL, COL_M_A), (D_MODEL, COL_M_B))
W_PREP_ROWS = 128
W_PREP_K = 256


def _w_in_prep_kernel(w_ref, o_ref):
    src = 0
    for width, dst in _W_IN_PIECES:
        o_ref[dst:dst + width, :] = w_ref[src:src + width, :].astype(o_ref.dtype)
        src += width
    pad0 = COL_B_LOW + GATE_RANK
    o_ref[pad0:MIX_COLS, :] = jnp.zeros((MIX_COLS - pad0, o_ref.shape[1]), o_ref.dtype)


def _w_in_prep(w_in_t):
    return pl.pallas_call(
        _w_in_prep_kernel,
        grid=(D_MODEL // W_PREP_K,),
        in_specs=[pl.BlockSpec((IN_WIDTH, W_PREP_K), lambda i: (0, i))],
        out_specs=pl.BlockSpec((PROJ_WIDTH, W_PREP_K), lambda i: (0, i)),
        out_shape=jax.ShapeDtypeStruct((PROJ_WIDTH, D_MODEL), jnp.bfloat16),
        compiler_params=pltpu.CompilerParams(dimension_semantics=("arbitrary",),
                                             vmem_limit_bytes=VMEM_LIMIT_BYTES),
        name="w_in_prep",
    )(w_in_t)


def _w_out_prep_kernel(a_ref, b_ref, c_ref, oa_ref, ob_ref, oc_ref):
    oa_ref[...] = a_ref[...].astype(oa_ref.dtype)
    ob_ref[...] = b_ref[...].astype(ob_ref.dtype)
    oc_ref[...] = c_ref[...].astype(oc_ref.dtype)


def _w_out_prep(wa, wb, wo):
    spec = pl.BlockSpec((2 * W_PREP_ROWS, D_MODEL), lambda i: (i, 0))
    shape = jax.ShapeDtypeStruct((D_MODEL, D_MODEL), jnp.bfloat16)
    return pl.pallas_call(
        _w_out_prep_kernel,
        grid=(D_MODEL // (2 * W_PREP_ROWS),),
        in_specs=[spec, spec, spec],
        out_specs=[spec, spec, spec],
        out_shape=[shape, shape, shape],
        compiler_params=pltpu.CompilerParams(dimension_semantics=("arbitrary",)),
        name="w_out_prep",
    )(wa, wb, wo)


TILE = 256
IN_CHUNK = 512


def _silu(x):
    return x * (0.5 * jnp.tanh(0.5 * x) + 0.5)


def _rope_slab(xs, cos, sin_signed, first_half):
    partner = jnp.where(first_half, pltpu.roll(xs, LANES - A_HEAD_DIM // 2, 1),
                        pltpu.roll(xs, A_HEAD_DIM // 2, 1))
    return xs * cos + partner * sin_signed


def _project_tile(x_ref, pos_ref, nw_ref, invf_ref, w_ref, mix_ref, mg_ref):
    x = x_ref[...]
    var = jnp.mean(x * x, axis=-1, keepdims=True)
    h = (x * lax.rsqrt(var + EPS) * nw_ref[...]).astype(jnp.bfloat16)
    yield 300

    half = A_HEAD_DIM // 2
    n_grp = LANES // half
    qr = x.shape[0] // n_grp
    lane = lax.broadcasted_iota(jnp.int32, (1, LANES), 1)
    grp = lane // half
    pos = pos_ref[...].astype(jnp.float32)
    pos_dense = pos[0:qr]
    for j in range(1, n_grp):
        pos_dense = jnp.where(grp == j, pos[j * qr:(j + 1) * qr], pos_dense)
    ang = pos_dense * invf_ref[...]
    cos_dense = jnp.cos(ang)
    sin_dense = jnp.sin(ang)

    def replicate(dense, j):
        only_j = jnp.where(grp == j, dense, 0.0)
        out = only_j
        for r in range(1, n_grp):
            out = out + pltpu.roll(only_j, r * half, 1)
        return out

    cos = jnp.concatenate([replicate(cos_dense, j) for j in range(n_grp)], axis=0)
    sin = jnp.concatenate([replicate(sin_dense, j) for j in range(n_grp)], axis=0)
    first_half = (lane % A_HEAD_DIM) < half
    sin_signed = jnp.where(first_half, -sin, sin)
    a_q_scale = (A_HEAD_DIM ** -0.5) * LOG2E
    b_q_scale = B_KEY_DIM ** -0.5
    yield 300

    for c0 in range(0, PROJ_WIDTH, IN_CHUNK):
        cw = min(IN_CHUNK, PROJ_WIDTH - c0)
        acc = lax.dot_general(h, w_ref[c0:c0 + cw, :], (((1,), (1,)), ((), ())),
                              preferred_element_type=jnp.float32)
        for s0 in range(0, cw, LANES):
            col = c0 + s0
            slab = acc[:, s0:s0 + LANES]
            if COL_A_Q <= col < COL_A_Q + A_WIDTH:
                slab = _rope_slab(slab, cos, sin_signed, first_half) * a_q_scale
            elif COL_A_K <= col < COL_A_K + A_KV_WIDTH:
                slab = _rope_slab(slab, cos, sin_signed, first_half)
            elif COL_B_Q <= col < COL_B_Q + B_QK_WIDTH:
                slab = slab * b_q_scale
            if col < MIX_COLS:
                mix_ref[:, col:col + LANES] = slab.astype(mix_ref.dtype)
            else:
                mg_ref[:, col - MIX_COLS:col - MIX_COLS + LANES] = slab.astype(mg_ref.dtype)
        yield cw


SWA_ROWS = 2 * BLOCK
SWA_COLS = 4 * 2 * BLOCK


def _swa_bias(sink_ref, g, first_block):
    row = lax.broadcasted_iota(jnp.int32, (SWA_ROWS, SWA_COLS), 0)
    col = lax.broadcasted_iota(jnp.int32, (SWA_ROWS, SWA_COLS), 1)
    qi, sub = row % BLOCK, row // BLOCK
    ki, hi = col % (2 * BLOCK), col // (2 * BLOCK)
    rel = qi + BLOCK - ki
    valid = (rel >= 0) & (rel < WINDOW)
    if first_block:
        valid = valid & (ki >= BLOCK)
    sink = jnp.zeros((SWA_ROWS, SWA_COLS), jnp.float32)
    for s in range(2):
        for i in range(4):
            sink = jnp.where((sub == s) & (hi == i), sink_ref[g * A_GROUP + s * 4 + i] * LOG2E, sink)
    return jnp.where(ki == 0, sink, jnp.where(valid, 0.0, NEG_INF))


def _swa_block(mix_ref, r0, k_prev, v_prev, first, bias_ref, oa_ref):
    bf16 = jnp.bfloat16
    rows = slice(r0, r0 + BLOCK)
    krow = lax.broadcasted_iota(jnp.int32, (2 * BLOCK, A_KV_WIDTH), 0)
    lane = lax.broadcasted_iota(jnp.int32, (2 * BLOCK, A_KV_WIDTH), 1)
    k = jnp.concatenate([k_prev, mix_ref[rows, COL_A_K:COL_A_K + A_KV_WIDTH]], axis=0)
    v = jnp.concatenate([v_prev, mix_ref[rows, COL_A_V:COL_A_V + A_KV_WIDTH]], axis=0)
    k = jnp.where(krow == 0, jnp.zeros_like(k), k)
    v = jnp.where(krow == 0, jnp.zeros_like(v), v)
    k_t = k.T
    v_swapped = pltpu.roll(v, A_HEAD_DIM, 1)
    lo = lane < A_HEAD_DIM
    ones_lo = jnp.where(lo, 1.0, 0.0).astype(bf16)
    ones_hi = jnp.where(lo, 0.0, 1.0).astype(bf16)
    zero_kt = jnp.zeros((A_HEAD_DIM, 2 * BLOCK), bf16)
    zero_v = jnp.zeros_like(v)

    for g in range(A_KV_HEADS):
        k_tg = k_t[g * A_HEAD_DIM:(g + 1) * A_HEAD_DIM, :]
        w_k = jnp.concatenate(
            [jnp.concatenate([k_tg if c == i else zero_kt for c in range(4)], axis=1) for i in range(4)],
            axis=0)
        c0 = COL_A_Q + g * A_GROUP * A_HEAD_DIM
        q_g = jnp.concatenate([mix_ref[rows, c0:c0 + 256], mix_ref[rows, c0 + 256:c0 + 512]], axis=0)
        s = jnp.dot(q_g, w_k, preferred_element_type=jnp.float32) + bias_ref[first, g]
        ps = []
        for i in range(4):
            s_i = s[:, i * 2 * BLOCK:(i + 1) * 2 * BLOCK]
            ps.append(jnp.exp2(s_i - jnp.max(s_i, axis=-1, keepdims=True)).astype(bf16))
        yield 550
        v_lo = jnp.where(lo, v if g == 0 else v_swapped, zero_v)
        v_hi = jnp.where(lo, zero_v, v_swapped if g == 0 else v)
        w_v = jnp.concatenate([jnp.concatenate([v_lo, ones_lo], axis=1),
                               jnp.concatenate([v_hi, ones_hi], axis=1)], axis=0)
        p = jnp.concatenate([jnp.concatenate([ps[0], ps[1]], axis=1),
                             jnp.concatenate([ps[2], ps[3]], axis=1)], axis=0)
        r = jnp.dot(p, w_v, preferred_element_type=jnp.float32)
        o_n = r[:, :LANES] / r[:, LANES:]
        for pair in range(2):
            for sub in range(2):
                off = g * A_GROUP * A_HEAD_DIM + sub * 256 + pair * LANES
                piece = o_n[pair * SWA_ROWS + sub * BLOCK:pair * SWA_ROWS + (sub + 1) * BLOCK, :]
                gate = mix_ref[rows, COL_A_GATE + off:COL_A_GATE + off + LANES].astype(jnp.float32)
                oa_ref[rows, off:off + LANES] = (piece * _silu(gate)).astype(oa_ref.dtype)
        yield 400


def _swa_tile(mix_ref, first_tile, kv_prev_ref, bias_ref, oa_ref):
    first = first_tile.astype(jnp.int32)
    k_prev = kv_prev_ref[:, :A_KV_WIDTH]
    v_prev = kv_prev_ref[:, A_KV_WIDTH:]
    for blk in range(TILE // BLOCK):
        r0 = blk * BLOCK
        yield from _swa_block(mix_ref, r0, k_prev, v_prev, first if blk == 0 else 0, bias_ref, oa_ref)
        k_prev = mix_ref[r0:r0 + BLOCK, COL_A_K:COL_A_K + A_KV_WIDTH]
        v_prev = mix_ref[r0:r0 + BLOCK, COL_A_V:COL_A_V + A_KV_WIDTH]
    kv_prev_ref[:, :A_KV_WIDTH] = k_prev
    kv_prev_ref[:, A_KV_WIDTH:] = v_prev


def _chunk_tril():
    ti = lax.broadcasted_iota(jnp.int32, (TILE, TILE), 0)
    tj = lax.broadcasted_iota(jnp.int32, (TILE, TILE), 1)
    return jnp.where((ti // CHUNK == tj // CHUNK) & (ti >= tj), 1.0, 0.0).astype(jnp.bfloat16)


def _chunk_cumsum_rows(x, tril_ref):
    hi = x.astype(jnp.bfloat16)
    lo = (x - hi.astype(jnp.float32)).astype(jnp.bfloat16)
    both = jnp.dot(tril_ref[...], jnp.concatenate([hi, lo], axis=1), preferred_element_type=jnp.float32)
    return both[:, :x.shape[1]] + both[:, x.shape[1]:]


def _gla_tile(mix_ref, first_tile, up_ref, bias_ref, nw_ref, ob_ref,
              state_ref, qe_ref, ke_ref, ks_ref, dcol_ref, inc_ref, oacc_ref, tril_ref):
    bf16 = jnp.bfloat16
    n_chunks = TILE // CHUNK

    gk = jnp.dot(mix_ref[:, COL_B_LOW:COL_B_LOW + LANES], up_ref[...],
                 preferred_element_type=jnp.float32) + bias_ref[...]
    log_a = (jnp.minimum(gk, 0.0) - jnp.log(1.0 + jnp.exp(-jnp.abs(gk)))) * (1.0 / GATE_TAU)
    b = _chunk_cumsum_rows(log_a, tril_ref)
    yield 700
    b_last_rows = [b[(c + 1) * CHUNK - 1:(c + 1) * CHUNK, :] for c in range(n_chunks)]
    b_last = jnp.concatenate([jnp.broadcast_to(r, (CHUNK, B_QK_WIDTH)) for r in b_last_rows], axis=0)
    qf = mix_ref[:, COL_B_Q:COL_B_Q + B_QK_WIDTH].astype(jnp.float32)
    kf = mix_ref[:, COL_B_K:COL_B_K + B_QK_WIDTH].astype(jnp.float32)
    qe_ref[...] = (qf * jnp.exp(b)).astype(bf16)
    ke_ref[...] = (kf * jnp.exp(-b)).astype(bf16)
    ks_ref[...] = (kf * jnp.exp(b_last - b)).astype(bf16)
    pad = jnp.zeros((LANES - n_chunks, B_QK_WIDTH), jnp.float32)
    dcol_ref[...] = jnp.exp(jnp.concatenate(b_last_rows + [pad], axis=0)).T
    yield 700

    ti = lax.broadcasted_iota(jnp.int32, (TILE, TILE), 0)
    tj = lax.broadcasted_iota(jnp.int32, (TILE, TILE), 1)
    keep = (ti // CHUNK == tj // CHUNK) & (ti >= tj)
    for h in range(B_HEADS):
        ks = slice(h * B_KEY_DIM, (h + 1) * B_KEY_DIM)
        vs = slice(COL_B_V + h * B_VAL_DIM, COL_B_V + (h + 1) * B_VAL_DIM)
        os_ = slice(h * B_VAL_DIM, (h + 1) * B_VAL_DIM)
        att = lax.dot_general(qe_ref[:, ks], ke_ref[:, ks], (((1,), (1,)), ((), ())),
                              preferred_element_type=jnp.float32)
        att = jnp.where(keep, att, 0.0).astype(bf16)
        oacc_ref[:, os_] = jnp.dot(att, mix_ref[:, vs], preferred_element_type=jnp.float32)
        for c in range(n_chunks):
            rows = slice(c * CHUNK, (c + 1) * CHUNK)
            inc_ref[c, h] = lax.dot_general(ks_ref[rows, ks], mix_ref[rows, vs], (((0,), (0,)), ((), ())),
                                            preferred_element_type=jnp.float32)
        yield 350

    for h in range(B_HEADS):
        ks = slice(h * B_KEY_DIM, (h + 1) * B_KEY_DIM)
        os_ = slice(h * B_VAL_DIM, (h + 1) * B_VAL_DIM)
        st = jnp.where(first_tile, 0.0, state_ref[h])
        for c in range(n_chunks):
            rows = slice(c * CHUNK, (c + 1) * CHUNK)
            oacc_ref[rows, os_] += jnp.dot(qe_ref[rows, ks], st.astype(bf16),
                                           preferred_element_type=jnp.float32)
            st = dcol_ref[ks, c:c + 1] * st + inc_ref[c, h]
        state_ref[h] = st
        yield 150

    for h in range(B_HEADS):
        os_ = slice(h * B_VAL_DIM, (h + 1) * B_VAL_DIM)
        o_h = oacc_ref[:, os_]
        var = jnp.mean(o_h * o_h, axis=-1, keepdims=True)
        o_n = o_h * lax.rsqrt(var + EPS) * nw_ref[...]
        gate = mix_ref[:, COL_B_GATE + h * B_VAL_DIM:COL_B_GATE + (h + 1) * B_VAL_DIM].astype(jnp.float32)
        ob_ref[:, os_] = (o_n * _silu(gate)).astype(ob_ref.dtype)
        yield 225


def _emit_interleaved(*streams):
    spent = [0.0] * len(streams)
    live = list(range(len(streams)))
    while live:
        k = min(live, key=lambda j: spent[j])
        try:
            spent[k] += next(streams[k])
        except StopIteration:
            live.remove(k)


def _mixer_kernel(tiles_per_seq, sink_ref, x_ref, pos_ref, nw_ref, invf_ref, w_ref, up_ref, gbias_ref,
                  bnw_ref, mg_ref, oa_ref, ob_ref, mix_a, mix_b, kv_prev_ref, sbias_ref,
                  state_ref, qe_ref, ke_ref, ks_ref, dcol_ref, inc_ref, oacc_ref, tril_ref):
    i = pl.program_id(0)
    n = pl.num_programs(0) - 1

    def project(dst):
        yield from _project_tile(x_ref, pos_ref, nw_ref, invf_ref, w_ref, dst, mg_ref)

    def mix(src):
        first_tile = ((i - 1) % tiles_per_seq) == 0
        yield from _swa_tile(src, first_tile, kv_prev_ref, sbias_ref, oa_ref)
        yield from _gla_tile(src, first_tile, up_ref, gbias_ref, bnw_ref, ob_ref,
                             state_ref, qe_ref, ke_ref, ks_ref, dcol_ref, inc_ref, oacc_ref, tril_ref)

    @pl.when(i == 0)
    def _():
        for g in range(A_KV_HEADS):
            sbias_ref[0, g] = _swa_bias(sink_ref, g, False)
            sbias_ref[1, g] = _swa_bias(sink_ref, g, True)
        kv_prev_ref[...] = jnp.zeros_like(kv_prev_ref)
        state_ref[...] = jnp.zeros_like(state_ref)
        tril_ref[...] = _chunk_tril()
        _emit_interleaved(project(mix_a))

    @pl.when((i > 0) & (i < n) & (i % 2 == 1))
    def _():
        _emit_interleaved(mix(mix_a), project(mix_b))

    @pl.when((i > 0) & (i < n) & (i % 2 == 0))
    def _():
        _emit_interleaved(mix(mix_b), project(mix_a))

    @pl.when(i == n)
    def _():
        _emit_interleaved(mix(mix_b))


def _mixer(x2, pos2, norm_w, inv_freq_row, w_perm, sinks, up_pad, gate_bias, b_norm_w, tiles_per_seq):
    m = x2.shape[0]
    n = m // TILE
    bf16 = jnp.bfloat16
    cur = lambda i: (jnp.minimum(i, n - 1), 0)
    lag = lambda i: (jnp.maximum(i - 1, 0), 0)
    const = lambda i: (0, 0)
    n_chunks = TILE // CHUNK
    return pl.pallas_call(
        functools.partial(_mixer_kernel, tiles_per_seq),
        grid=(n + 1,),
        in_specs=[
            pl.BlockSpec(memory_space=pltpu.SMEM),
            pl.BlockSpec((TILE, D_MODEL), cur),
            pl.BlockSpec((TILE, 1), cur),
            pl.BlockSpec((1, D_MODEL), const),
            pl.BlockSpec((1, LANES), const),
            pl.BlockSpec((PROJ_WIDTH, D_MODEL), const, pipeline_mode=pl.Buffered(1)),
            pl.BlockSpec((LANES, B_QK_WIDTH), const),
            pl.BlockSpec((1, B_QK_WIDTH), const),
            pl.BlockSpec((1, B_VAL_DIM), const),
        ],
        out_specs=[
            pl.BlockSpec((TILE, 2 * D_MODEL), cur),
            pl.BlockSpec((TILE, A_WIDTH), lag),
            pl.BlockSpec((TILE, B_V_WIDTH), lag),
        ],
        out_shape=[
            jax.ShapeDtypeStruct((m, 2 * D_MODEL), bf16),
            jax.ShapeDtypeStruct((m, A_WIDTH), bf16),
            jax.ShapeDtypeStruct((m, B_V_WIDTH), bf16),
        ],
        scratch_shapes=[
            pltpu.VMEM((TILE, MIX_COLS), bf16),
            pltpu.VMEM((TILE, MIX_COLS), bf16),
            pltpu.VMEM((BLOCK, 2 * A_KV_WIDTH), bf16),
            pltpu.VMEM((2, A_KV_HEADS, SWA_ROWS, SWA_COLS), jnp.float32),
            pltpu.VMEM((B_HEADS, B_KEY_DIM, B_VAL_DIM), jnp.float32),
            pltpu.VMEM((TILE, B_QK_WIDTH), bf16),
            pltpu.VMEM((TILE, B_QK_WIDTH), bf16),
            pltpu.VMEM((TILE, B_QK_WIDTH), bf16),
            pltpu.VMEM((B_QK_WIDTH, LANES), jnp.float32),
            pltpu.VMEM((n_chunks, B_HEADS, B_KEY_DIM, B_VAL_DIM), jnp.float32),
            pltpu.VMEM((TILE, B_V_WIDTH), jnp.float32),
            pltpu.VMEM((TILE, TILE), bf16),
        ],
        compiler_params=pltpu.CompilerParams(dimension_semantics=("arbitrary",),
                                             vmem_limit_bytes=VMEM_LIMIT_BYTES),
        name="mixer",
    )(sinks, x2, pos2, norm_w, inv_freq_row, w_perm, up_pad, gate_bias, b_norm_w)


OUT_TM = 512


def _out_proj_kernel(x_ref, oa_ref, ob_ref, ma_ref, mb_ref, wa_ref, wb_ref, wo_ref, fw_ref, o_ref):
    y_a = jnp.dot(oa_ref[...], wa_ref[...], preferred_element_type=jnp.float32)
    y_b = jnp.dot(ob_ref[...], wb_ref[...], preferred_element_type=jnp.float32)
    merged = (jax.nn.sigmoid(ma_ref[...].astype(jnp.float32)) * y_a
              + jax.nn.sigmoid(mb_ref[...].astype(jnp.float32)) * y_b)
    z = x_ref[...] + jnp.dot(merged.astype(jnp.bfloat16), wo_ref[...],
                             preferred_element_type=jnp.float32)
    var = jnp.mean(z * z, axis=-1, keepdims=True)
    o_ref[...] = z * lax.rsqrt(var + EPS) * fw_ref[...]


def _out_proj(x2, o_a, o_b, mg, wa, wb, wo, final_w):
    m = x2.shape[0]
    row = lambda i: (i, 0)
    const = lambda i: (0, 0)
    return pl.pallas_call(
        _out_proj_kernel,
        grid=(m // OUT_TM,),
        in_specs=[
            pl.BlockSpec((OUT_TM, D_MODEL), row),
            pl.BlockSpec((OUT_TM, A_WIDTH), row),
            pl.BlockSpec((OUT_TM, B_V_WIDTH), row),
            pl.BlockSpec((OUT_TM, D_MODEL), lambda i: (i, 0)),
            pl.BlockSpec((OUT_TM, D_MODEL), lambda i: (i, 1)),
            pl.BlockSpec((A_WIDTH, D_MODEL), const),
            pl.BlockSpec((B_V_WIDTH, D_MODEL), const),
            pl.BlockSpec((D_MODEL, D_MODEL), const),
            pl.BlockSpec((1, D_MODEL), const),
        ],
        out_specs=pl.BlockSpec((OUT_TM, D_MODEL), row),
        out_shape=jax.ShapeDtypeStruct((m, D_MODEL), jnp.float32),
        compiler_params=pltpu.CompilerParams(dimension_semantics=("arbitrary",),
                                             vmem_limit_bytes=VMEM_LIMIT_BYTES),
        name="out_proj",
    )(x2, o_a, o_b, mg, mg, wa, wb, wo, final_w)


def kernel(x, positions, norm_w, w_in, a_sinks, b_gate_up, b_gate_bias, b_out_norm_w,
           w_a_proj, w_b_proj, w_out, final_norm_w):
    bsz, t, d = x.shape
    assert d == D_MODEL and t % (2 * TILE) == 0 and (bsz * t) % OUT_TM == 0
    depth = norm_w.shape[0]
    assert depth == 1
    bf16 = jnp.bfloat16

    half = A_HEAD_DIM // 2
    inv_freq = ROPE_THETA ** (-jnp.arange(half, dtype=jnp.float32) / half)
    inv_freq_row = jnp.tile(inv_freq, LANES // half)[None, :]
    pos2 = positions.reshape(bsz * t, 1)

    x2 = x.reshape(bsz * t, d)
    for layer in range(depth):
        w_perm = _w_in_prep(w_in[layer].T)
        wa, wb, wo = _w_out_prep(w_a_proj[layer], w_b_proj[layer], w_out[layer])
        up_pad = jnp.zeros((LANES, B_QK_WIDTH), bf16).at[:GATE_RANK].set(b_gate_up[layer].astype(bf16))
        mg, o_a, o_b = _mixer(x2, pos2, norm_w[layer][None, :], inv_freq_row, w_perm, a_sinks[layer],
                              up_pad, b_gate_bias[layer][None, :], b_out_norm_w[layer][None, :],
                              t // TILE)
        x2 = _out_proj(x2, o_a, o_b, mg, wa, wb, wo, final_norm_w[None, :])
    return x2.reshape(bsz, t, d)
```

```python
import functools

import jax
import jax.numpy as jnp
from jax import lax
from jax.experimental import pallas as pl
from jax.experimental.pallas import tpu as pltpu

D_MODEL = 1024
A_HEADS = 16
A_KV_HEADS = 2
A_HEAD_DIM = 64
A_GROUP = A_HEADS // A_KV_HEADS
A_WIDTH = A_HEADS * A_HEAD_DIM
A_KV_WIDTH = A_KV_HEADS * A_HEAD_DIM
WINDOW = 128
BLOCK = 128
ROPE_THETA = 10000.0

B_HEADS = 4
B_QK_WIDTH = D_MODEL // 2
B_V_WIDTH = D_MODEL
B_KEY_DIM = B_QK_WIDTH // B_HEADS
B_VAL_DIM = B_V_WIDTH // B_HEADS
GATE_RANK = 16
GATE_TAU = 16.0
CHUNK = 64

EPS = 1e-5
NEG_INF = -1e30
LOG2E = 1.4426950408889634

LANES = 128

COL_A_Q = 0
COL_A_GATE = 1024
COL_B_V = 2048
COL_B_GATE = 3072
COL_B_Q = 4096
COL_B_K = 4608
COL_A_K = 5120
COL_A_V = 5248
COL_B_LOW = 5376
MIX_COLS = 5504
COL_M_A = MIX_COLS
COL_M_B = MIX_COLS + D_MODEL
PROJ_WIDTH = MIX_COLS + 2 * D_MODEL

VMEM_LIMIT_BYTES = 60 * 1024 * 1024


IN_WIDTH = 2 * A_WIDTH + 2 * A_KV_WIDTH + 2 * B_QK_WIDTH + 2 * B_V_WIDTH + GATE_RANK + 2 * D_MODEL
_W_IN_PIECES = ((A_WIDTH, COL_A_Q), (A_KV_WIDTH, COL_A_K), (A_KV_WIDTH, COL_A_V), (A_WIDTH, COL_A_GATE),
                (B_QK_WIDTH, COL_B_Q), (B_QK_WIDTH, COL_B_K), (B_V_WIDTH, COL_B_V), (B_V_WIDTH, COL_B_GATE),
                (GATE_RANK, COL_B_LOW), (D_MODEL, COL_M_A), (D_MODEL, COL_M_B))
W_PREP_ROWS = 128
W_PREP_K = 256


def _w_in_prep_kernel(w_ref, o_ref):
    src = 0
    for width, dst in _W_IN_PIECES:
        if width % LANES == 0:
            for r in range(0, width, LANES):
                o_ref[:, dst + r:dst + r + LANES] = w_ref[src + r:src + r + LANES, :].T.astype(o_ref.dtype)
        else:
            slab = w_ref[src:src + LANES, :].T
            lane = lax.broadcasted_iota(jnp.int32, slab.shape, 1)
            o_ref[:, dst:dst + LANES] = jnp.where(lane < width, slab, 0.0).astype(o_ref.dtype)
        src += width


def _w_in_prep(w_in_t):
    return pl.pallas_call(
        _w_in_prep_kernel,
        grid=(D_MODEL // W_PREP_K,),
        in_specs=[pl.BlockSpec((IN_WIDTH, W_PREP_K), lambda i: (0, i))],
        out_specs=pl.BlockSpec((W_PREP_K, PROJ_WIDTH), lambda i: (i, 0)),
        out_shape=jax.ShapeDtypeStruct((D_MODEL, PROJ_WIDTH), jnp.bfloat16),
        compiler_params=pltpu.CompilerParams(dimension_semantics=("arbitrary",),
                                             vmem_limit_bytes=VMEM_LIMIT_BYTES),
        name="w_in_prep",
    )(w_in_t)


def _w_out_prep_kernel(a_ref, b_ref, c_ref, oa_ref, ob_ref, oc_ref):
    oa_ref[...] = a_ref[...].astype(oa_ref.dtype)
    ob_ref[...] = b_ref[...].astype(ob_ref.dtype)
    oc_ref[...] = c_ref[...].astype(oc_ref.dtype)


def _w_out_prep(wa, wb, wo):
    spec = pl.BlockSpec((2 * W_PREP_ROWS, D_MODEL), lambda i: (i, 0))
    shape = jax.ShapeDtypeStruct((D_MODEL, D_MODEL), jnp.bfloat16)
    return pl.pallas_call(
        _w_out_prep_kernel,
        grid=(D_MODEL // (2 * W_PREP_ROWS),),
        in_specs=[spec, spec, spec],
        out_specs=[spec, spec, spec],
        out_shape=[shape, shape, shape],
        compiler_params=pltpu.CompilerParams(dimension_semantics=("arbitrary",)),
        name="w_out_prep",
    )(wa, wb, wo)


TILE = 256
IN_CHUNK = 512


def _silu(x):
    return x * (0.5 * jnp.tanh(0.5 * x) + 0.5)


def _rope_slab(xs, cos, sin_signed, first_half):
    partner = jnp.where(first_half, pltpu.roll(xs, LANES - A_HEAD_DIM // 2, 1),
                        pltpu.roll(xs, A_HEAD_DIM // 2, 1))
    return xs * cos + partner * sin_signed


def _project_tile(x_ref, pos_ref, nw_ref, invf_ref, w_ref, mix_ref, mg_ref):
    x = x_ref[...]
    var = jnp.mean(x * x, axis=-1, keepdims=True)
    h = (x * lax.rsqrt(var + EPS) * nw_ref[...]).astype(jnp.bfloat16)
    yield 300

    half = A_HEAD_DIM // 2
    n_grp = LANES // half
    qr = x.shape[0] // n_grp
    lane = lax.broadcasted_iota(jnp.int32, (1, LANES), 1)
    grp = lane // half
    pos = pos_ref[...].astype(jnp.float32)
    pos_dense = pos[0:qr]
    for j in range(1, n_grp):
        pos_dense = jnp.where(grp == j, pos[j * qr:(j + 1) * qr], pos_dense)
    ang = pos_dense * invf_ref[...]
    cos_dense = jnp.cos(ang)
    sin_dense = jnp.sin(ang)

    def replicate(dense, j):
        only_j = jnp.where(grp == j, dense, 0.0)
        out = only_j
        for r in range(1, n_grp):
            out = out + pltpu.roll(only_j, r * half, 1)
        return out

    cos = jnp.concatenate([replicate(cos_dense, j) for j in range(n_grp)], axis=0)
    sin = jnp.concatenate([replicate(sin_dense, j) for j in range(n_grp)], axis=0)
    first_half = (lane % A_HEAD_DIM) < half
    sin_signed = jnp.where(first_half, -sin, sin)
    a_q_scale = (A_HEAD_DIM ** -0.5) * LOG2E
    b_q_scale = B_KEY_DIM ** -0.5
    yield 300

    for c0 in range(0, PROJ_WIDTH, IN_CHUNK):
        cw = min(IN_CHUNK, PROJ_WIDTH - c0)
        acc = jnp.dot(h, w_ref[:, c0:c0 + cw], preferred_element_type=jnp.float32)
        for s0 in range(0, cw, LANES):
            col = c0 + s0
            slab = acc[:, s0:s0 + LANES]
            if COL_A_Q <= col < COL_A_Q + A_WIDTH:
                slab = _rope_slab(slab, cos, sin_signed, first_half) * a_q_scale
            elif COL_A_K <= col < COL_A_K + A_KV_WIDTH:
                slab = _rope_slab(slab, cos, sin_signed, first_half)
            elif COL_B_Q <= col < COL_B_Q + B_QK_WIDTH:
                slab = slab * b_q_scale
            if col < MIX_COLS:
                mix_ref[:, col:col + LANES] = slab.astype(mix_ref.dtype)
            else:
                mg_ref[:, col - MIX_COLS:col - MIX_COLS + LANES] = slab.astype(mg_ref.dtype)
        yield cw


SWA_ROWS = 2 * BLOCK
SWA_COLS = 4 * 2 * BLOCK


def _swa_bias(sink_ref, g, first_block):
    row = lax.broadcasted_iota(jnp.int32, (SWA_ROWS, SWA_COLS), 0)
    col = lax.broadcasted_iota(jnp.int32, (SWA_ROWS, SWA_COLS), 1)
    qi, sub = row % BLOCK, row // BLOCK
    ki, hi = col % (2 * BLOCK), col // (2 * BLOCK)
    rel = qi + BLOCK - ki
    valid = (rel >= 0) & (rel < WINDOW)
    if first_block:
        valid = valid & (ki >= BLOCK)
    sink = jnp.zeros((SWA_ROWS, SWA_COLS), jnp.float32)
    for s in range(2):
        for i in range(4):
            sink = jnp.where((sub == s) & (hi == i), sink_ref[g * A_GROUP + s * 4 + i] * LOG2E, sink)
    return jnp.where(ki == 0, sink, jnp.where(valid, 0.0, NEG_INF))


def _swa_block(mix_ref, r0, k_prev, v_prev, first, bias_ref, oa_ref):
    bf16 = jnp.bfloat16
    rows = slice(r0, r0 + BLOCK)
    krow = lax.broadcasted_iota(jnp.int32, (2 * BLOCK, A_KV_WIDTH), 0)
    lane = lax.broadcasted_iota(jnp.int32, (2 * BLOCK, A_KV_WIDTH), 1)
    k = jnp.concatenate([k_prev, mix_ref[rows, COL_A_K:COL_A_K + A_KV_WIDTH]], axis=0)
    v = jnp.concatenate([v_prev, mix_ref[rows, COL_A_V:COL_A_V + A_KV_WIDTH]], axis=0)
    k = jnp.where(krow == 0, jnp.zeros_like(k), k)
    v = jnp.where(krow == 0, jnp.zeros_like(v), v)
    k_t = k.T
    v_swapped = pltpu.roll(v, A_HEAD_DIM, 1)
    lo = lane < A_HEAD_DIM
    ones_lo = jnp.where(lo, 1.0, 0.0).astype(bf16)
    ones_hi = jnp.where(lo, 0.0, 1.0).astype(bf16)
    zero_kt = jnp.zeros((A_HEAD_DIM, 2 * BLOCK), bf16)
    zero_v = jnp.zeros_like(v)

    for g in range(A_KV_HEADS):
        k_tg = k_t[g * A_HEAD_DIM:(g + 1) * A_HEAD_DIM, :]
        w_k = jnp.concatenate(
            [jnp.concatenate([k_tg if c == i else zero_kt for c in range(4)], axis=1) for i in range(4)],
            axis=0)
        c0 = COL_A_Q + g * A_GROUP * A_HEAD_DIM
        q_g = jnp.concatenate([mix_ref[rows, c0:c0 + 256], mix_ref[rows, c0 + 256:c0 + 512]], axis=0)
        s = jnp.dot(q_g, w_k, preferred_element_type=jnp.float32) + bias_ref[first, g]
        ps = []
        for i in range(4):
            s_i = s[:, i * 2 * BLOCK:(i + 1) * 2 * BLOCK]
            ps.append(jnp.exp2(s_i - jnp.max(s_i, axis=-1, keepdims=True)).astype(bf16))
        yield 550
        v_lo = jnp.where(lo, v if g == 0 else v_swapped, zero_v)
        v_hi = jnp.where(lo, zero_v, v_swapped if g == 0 else v)
        w_v = jnp.concatenate([jnp.concatenate([v_lo, ones_lo], axis=1),
                               jnp.concatenate([v_hi, ones_hi], axis=1)], axis=0)
        p = jnp.concatenate([jnp.concatenate([ps[0], ps[1]], axis=1),
                             jnp.concatenate([ps[2], ps[3]], axis=1)], axis=0)
        r = jnp.dot(p, w_v, preferred_element_type=jnp.float32)
        o_n = r[:, :LANES] / r[:, LANES:]
        for pair in range(2):
            for sub in range(2):
                off = g * A_GROUP * A_HEAD_DIM + sub * 256 + pair * LANES
                piece = o_n[pair * SWA_ROWS + sub * BLOCK:pair * SWA_ROWS + (sub + 1) * BLOCK, :]
                gate = mix_ref[rows, COL_A_GATE + off:COL_A_GATE + off + LANES].astype(jnp.float32)
                oa_ref[rows, off:off + LANES] = (piece * _silu(gate)).astype(oa_ref.dtype)
        yield 400


def _swa_tile(mix_ref, first_tile, kv_prev_ref, bias_ref, oa_ref):
    first = first_tile.astype(jnp.int32)
    k_prev = kv_prev_ref[:, :A_KV_WIDTH]
    v_prev = kv_prev_ref[:, A_KV_WIDTH:]
    for blk in range(TILE // BLOCK):
        r0 = blk * BLOCK
        yield from _swa_block(mix_ref, r0, k_prev, v_prev, first if blk == 0 else 0, bias_ref, oa_ref)
        k_prev = mix_ref[r0:r0 + BLOCK, COL_A_K:COL_A_K + A_KV_WIDTH]
        v_prev = mix_ref[r0:r0 + BLOCK, COL_A_V:COL_A_V + A_KV_WIDTH]
    kv_prev_ref[:, :A_KV_WIDTH] = k_prev
    kv_prev_ref[:, A_KV_WIDTH:] = v_prev


def _chunk_tril():
    ti = lax.broadcasted_iota(jnp.int32, (TILE, TILE), 0)
    tj = lax.broadcasted_iota(jnp.int32, (TILE, TILE), 1)
    return jnp.where((ti // CHUNK == tj // CHUNK) & (ti >= tj), 1.0, 0.0).astype(jnp.bfloat16)


def _chunk_cumsum_rows(x, tril_ref):
    hi = x.astype(jnp.bfloat16)
    lo = (x - hi.astype(jnp.float32)).astype(jnp.bfloat16)
    both = jnp.dot(tril_ref[...], jnp.concatenate([hi, lo], axis=1), preferred_element_type=jnp.float32)
    return both[:, :x.shape[1]] + both[:, x.shape[1]:]


def _gla_tile(mix_ref, first_tile, up_ref, bias_ref, nw_ref, ob_ref,
              state_ref, qe_ref, ke_ref, ks_ref, dcol_ref, inc_ref, oacc_ref, tril_ref):
    bf16 = jnp.bfloat16
    n_chunks = TILE // CHUNK

    gk = jnp.dot(mix_ref[:, COL_B_LOW:COL_B_LOW + LANES], up_ref[...],
                 preferred_element_type=jnp.float32) + bias_ref[...]
    log_a = (jnp.minimum(gk, 0.0) - jnp.log(1.0 + jnp.exp(-jnp.abs(gk)))) * (1.0 / GATE_TAU)
    b = _chunk_cumsum_rows(log_a, tril_ref)
    yield 700
    b_last_rows = [b[(c + 1) * CHUNK - 1:(c + 1) * CHUNK, :] for c in range(n_chunks)]
    b_last = jnp.concatenate([jnp.broadcast_to(r, (CHUNK, B_QK_WIDTH)) for r in b_last_rows], axis=0)
    qf = mix_ref[:, COL_B_Q:COL_B_Q + B_QK_WIDTH].astype(jnp.float32)
    kf = mix_ref[:, COL_B_K:COL_B_K + B_QK_WIDTH].astype(jnp.float32)
    qe_ref[...] = (qf * jnp.exp(b)).astype(bf16)
    ke_ref[...] = (kf * jnp.exp(-b)).astype(bf16)
    ks_ref[...] = (kf * jnp.exp(b_last - b)).astype(bf16)
    pad = jnp.zeros((LANES - n_chunks, B_QK_WIDTH), jnp.float32)
    dcol_ref[...] = jnp.exp(jnp.concatenate(b_last_rows + [pad], axis=0)).T
    yield 700

    ti = lax.broadcasted_iota(jnp.int32, (TILE, TILE), 0)
    tj = lax.broadcasted_iota(jnp.int32, (TILE, TILE), 1)
    keep = (ti // CHUNK == tj // CHUNK) & (ti >= tj)
    for h in range(B_HEADS):
        ks = slice(h * B_KEY_DIM, (h + 1) * B_KEY_DIM)
        vs = slice(COL_B_V + h * B_VAL_DIM, COL_B_V + (h + 1) * B_VAL_DIM)
        os_ = slice(h * B_VAL_DIM, (h + 1) * B_VAL_DIM)
        att = lax.dot_general(qe_ref[:, ks], ke_ref[:, ks], (((1,), (1,)), ((), ())),
                              preferred_element_type=jnp.float32)
        att = jnp.where(keep, att, 0.0).astype(bf16)
        oacc_ref[:, os_] = jnp.dot(att, mix_ref[:, vs], preferred_element_type=jnp.float32)
        for c in range(n_chunks):
            rows = slice(c * CHUNK, (c + 1) * CHUNK)
            inc_ref[c, h] = lax.dot_general(ks_ref[rows, ks], mix_ref[rows, vs], (((0,), (0,)), ((), ())),
                                            preferred_element_type=jnp.float32)
        yield 350

    for h in range(B_HEADS):
        ks = slice(h * B_KEY_DIM, (h + 1) * B_KEY_DIM)
        os_ = slice(h * B_VAL_DIM, (h + 1) * B_VAL_DIM)
        st = jnp.where(first_tile, 0.0, state_ref[h])
        for c in range(n_chunks):
            rows = slice(c * CHUNK, (c + 1) * CHUNK)
            oacc_ref[rows, os_] += jnp.dot(qe_ref[rows, ks], st.astype(bf16),
                                           preferred_element_type=jnp.float32)
            st = dcol_ref[ks, c:c + 1] * st + inc_ref[c, h]
        state_ref[h] = st
        yield 150

    for h in range(B_HEADS):
        os_ = slice(h * B_VAL_DIM, (h + 1) * B_VAL_DIM)
        o_h = oacc_ref[:, os_]
        var = jnp.mean(o_h * o_h, axis=-1, keepdims=True)
        o_n = o_h * lax.rsqrt(var + EPS) * nw_ref[...]
        gate = mix_ref[:, COL_B_GATE + h * B_VAL_DIM:COL_B_GATE + (h + 1) * B_VAL_DIM].astype(jnp.float32)
        ob_ref[:, os_] = (o_n * _silu(gate)).astype(ob_ref.dtype)
        yield 225


def _emit_interleaved(*streams):
    spent = [0.0] * len(streams)
    live = list(range(len(streams)))
    while live:
        k = min(live, key=lambda j: spent[j])
        try:
            spent[k] += next(streams[k])
        except StopIteration:
            live.remove(k)


def _mixer_kernel(tiles_per_seq, sink_ref, x_ref, pos_ref, nw_ref, invf_ref, w_ref, up_ref, gbias_ref,
                  bnw_ref, mg_ref, oa_ref, ob_ref, mix_a, mix_b, kv_prev_ref, sbias_ref,
                  state_ref, qe_ref, ke_ref, ks_ref, dcol_ref, inc_ref, oacc_ref, tril_ref):
    i = pl.program_id(0)
    n = pl.num_programs(0) - 1

    def project(dst):
        yield from _project_tile(x_ref, pos_ref, nw_ref, invf_ref, w_ref, dst, mg_ref)

    def mix(src):
        first_tile = ((i - 1) % tiles_per_seq) == 0
        yield from _swa_tile(src, first_tile, kv_prev_ref, sbias_ref, oa_ref)
        yield from _gla_tile(src, first_tile, up_ref, gbias_ref, bnw_ref, ob_ref,
                             state_ref, qe_ref, ke_ref, ks_ref, dcol_ref, inc_ref, oacc_ref, tril_ref)

    @pl.when(i == 0)
    def _():
        for g in range(A_KV_HEADS):
            sbias_ref[0, g] = _swa_bias(sink_ref, g, False)
            sbias_ref[1, g] = _swa_bias(sink_ref, g, True)
        kv_prev_ref[...] = jnp.zeros_like(kv_prev_ref)
        state_ref[...] = jnp.zeros_like(state_ref)
        tril_ref[...] = _chunk_tril()
        _emit_interleaved(project(mix_a))

    @pl.when((i > 0) & (i < n) & (i % 2 == 1))
    def _():
        _emit_interleaved(mix(mix_a), project(mix_b))

    @pl.when((i > 0) & (i < n) & (i % 2 == 0))
    def _():
        _emit_interleaved(mix(mix_b), project(mix_a))

    @pl.when(i == n)
    def _():
        _emit_interleaved(mix(mix_b))


def _mixer(x2, pos2, norm_w, inv_freq_row, w_perm, sinks, up_pad, gate_bias, b_norm_w, tiles_per_seq):
    m = x2.shape[0]
    n = m // TILE
    bf16 = jnp.bfloat16
    cur = lambda i: (jnp.minimum(i, n - 1), 0)
    lag = lambda i: (jnp.maximum(i - 1, 0), 0)
    const = lambda i: (0, 0)
    n_chunks = TILE // CHUNK
    return pl.pallas_call(
        functools.partial(_mixer_kernel, tiles_per_seq),
        grid=(n + 1,),
        in_specs=[
            pl.BlockSpec(memory_space=pltpu.SMEM),
            pl.BlockSpec((TILE, D_MODEL), cur),
            pl.BlockSpec((TILE, 1), cur),
            pl.BlockSpec((1, D_MODEL), const),
            pl.BlockSpec((1, LANES), const),
            pl.BlockSpec((D_MODEL, PROJ_WIDTH), const, pipeline_mode=pl.Buffered(1)),
            pl.BlockSpec((LANES, B_QK_WIDTH), const),
            pl.BlockSpec((1, B_QK_WIDTH), const),
            pl.BlockSpec((1, B_VAL_DIM), const),
        ],
        out_specs=[
            pl.BlockSpec((TILE, 2 * D_MODEL), cur),
            pl.BlockSpec((TILE, A_WIDTH), lag),
            pl.BlockSpec((TILE, B_V_WIDTH), lag),
        ],
        out_shape=[
            jax.ShapeDtypeStruct((m, 2 * D_MODEL), bf16),
            jax.ShapeDtypeStruct((m, A_WIDTH), bf16),
            jax.ShapeDtypeStruct((m, B_V_WIDTH), bf16),
        ],
        scratch_shapes=[
            pltpu.VMEM((TILE, MIX_COLS), bf16),
            pltpu.VMEM((TILE, MIX_COLS), bf16),
            pltpu.VMEM((BLOCK, 2 * A_KV_WIDTH), bf16),
            pltpu.VMEM((2, A_KV_HEADS, SWA_ROWS, SWA_COLS), jnp.float32),
            pltpu.VMEM((B_HEADS, B_KEY_DIM, B_VAL_DIM), jnp.float32),
            pltpu.VMEM((TILE, B_QK_WIDTH), bf16),
            pltpu.VMEM((TILE, B_QK_WIDTH), bf16),
            pltpu.VMEM((TILE, B_QK_WIDTH), bf16),
            pltpu.VMEM((B_QK_WIDTH, LANES), jnp.float32),
            pltpu.VMEM((n_chunks, B_HEADS, B_KEY_DIM, B_VAL_DIM), jnp.float32),
            pltpu.VMEM((TILE, B_V_WIDTH), jnp.float32),
            pltpu.VMEM((TILE, TILE), bf16),
        ],
        compiler_params=pltpu.CompilerParams(dimension_semantics=("arbitrary",),
                                             vmem_limit_bytes=VMEM_LIMIT_BYTES),
        name="mixer",
    )(sinks, x2, pos2, norm_w, inv_freq_row, w_perm, up_pad, gate_bias, b_norm_w)


OUT_TM = 512


def _out_proj_kernel(x_ref, oa_ref, ob_ref, ma_ref, mb_ref, wa_ref, wb_ref, wo_ref, fw_ref, o_ref):
    y_a = jnp.dot(oa_ref[...], wa_ref[...], preferred_element_type=jnp.float32)
    y_b = jnp.dot(ob_ref[...], wb_ref[...], preferred_element_type=jnp.float32)
    merged = (jax.nn.sigmoid(ma_ref[...].astype(jnp.float32)) * y_a
              + jax.nn.sigmoid(mb_ref[...].astype(jnp.float32)) * y_b)
    z = x_ref[...] + jnp.dot(merged.astype(jnp.bfloat16), wo_ref[...],
                             preferred_element_type=jnp.float32)
    var = jnp.mean(z * z, axis=-1, keepdims=True)
    o_ref[...] = z * lax.rsqrt(var + EPS) * fw_ref[...]


def _out_proj(x2, o_a, o_b, mg, wa, wb, wo, final_w):
    m = x2.shape[0]
    row = lambda i: (i, 0)
    const = lambda i: (0, 0)
    return pl.pallas_call(
        _out_proj_kernel,
        grid=(m // OUT_TM,),
        in_specs=[
            pl.BlockSpec((OUT_TM, D_MODEL), row),
            pl.BlockSpec((OUT_TM, A_WIDTH), row),
            pl.BlockSpec((OUT_TM, B_V_WIDTH), row),
            pl.BlockSpec((OUT_TM, D_MODEL), lambda i: (i, 0)),
            pl.BlockSpec((OUT_TM, D_MODEL), lambda i: (i, 1)),
            pl.BlockSpec((A_WIDTH, D_MODEL), const),
            pl.BlockSpec((B_V_WIDTH, D_MODEL), const),
            pl.BlockSpec((D_MODEL, D_MODEL), const),
            pl.BlockSpec((1, D_MODEL), const),
        ],
        out_specs=pl.BlockSpec((OUT_TM, D_MODEL), row),
        out_shape=jax.ShapeDtypeStruct((m, D_MODEL), jnp.float32),
        compiler_params=pltpu.CompilerParams(dimension_semantics=("arbitrary",),
                                             vmem_limit_bytes=VMEM_LIMIT_BYTES),
        name="out_proj",
    )(x2, o_a, o_b, mg, mg, wa, wb, wo, final_w)


def kernel(x, positions, norm_w, w_in, a_sinks, b_gate_up, b_gate_bias, b_out_norm_w,
           w_a_proj, w_b_proj, w_out, final_norm_w):
    bsz, t, d = x.shape
    assert d == D_MODEL and t % (2 * TILE) == 0 and (bsz * t) % OUT_TM == 0
    depth = norm_w.shape[0]
    assert depth == 1
    bf16 = jnp.bfloat16

    half = A_HEAD_DIM // 2
    inv_freq = ROPE_THETA ** (-jnp.arange(half, dtype=jnp.float32) / half)
    inv_freq_row = jnp.tile(inv_freq, LANES // half)[None, :]
    pos2 = positions.reshape(bsz * t, 1)

    x2 = x.reshape(bsz * t, d)
    for layer in range(depth):
        w_perm = _w_in_prep(w_in[layer].T)
        wa, wb, wo = _w_out_prep(w_a_proj[layer], w_b_proj[layer], w_out[layer])
        up_pad = jnp.zeros((LANES, B_QK_WIDTH), bf16).at[:GATE_RANK].set(b_gate_up[layer].astype(bf16))
        mg, o_a, o_b = _mixer(x2, pos2, norm_w[layer][None, :], inv_freq_row, w_perm, a_sinks[layer],
                              up_pad, b_gate_bias[layer][None, :], b_out_norm_w[layer][None, :],
                              t // TILE)
        x2 = _out_proj(x2, o_a, o_b, mg, wa, wb, wo, final_norm_w[None, :])
    return x2.reshape(bsz, t, d)
```

```python
import functools

import jax
import jax.numpy as jnp
from jax import lax
from jax.experimental import pallas as pl
from jax.experimental.pallas import tpu as pltpu

D_MODEL = 1024
A_HEADS = 16
A_KV_HEADS = 2
A_HEAD_DIM = 64
A_GROUP = A_HEADS // A_KV_HEADS
A_WIDTH = A_HEADS * A_HEAD_DIM
A_KV_WIDTH = A_KV_HEADS * A_HEAD_DIM
WINDOW = 128
BLOCK = 128
ROPE_THETA = 10000.0

B_HEADS = 4
B_QK_WIDTH = D_MODEL // 2
B_V_WIDTH = D_MODEL
B_KEY_DIM = B_QK_WIDTH // B_HEADS
B_VAL_DIM = B_V_WIDTH // B_HEADS
GATE_RANK = 16
GATE_TAU = 16.0
CHUNK = 64

EPS = 1e-5
NEG_INF = -1e30
LOG2E = 1.4426950408889634

LANES = 128

COL_A_Q = 0
COL_A_GATE = 1024
COL_B_V = 2048
COL_B_GATE = 3072
COL_B_Q = 4096
COL_B_K = 4608
COL_A_K = 5120
COL_A_V = 5248
COL_B_LOW = 5376
MIX_COLS = 5504
COL_M_A = MIX_COLS
COL_M_B = MIX_COLS + D_MODEL
PROJ_WIDTH = MIX_COLS + 2 * D_MODEL

VMEM_LIMIT_BYTES = 60 * 1024 * 1024


IN_WIDTH = 2 * A_WIDTH + 2 * A_KV_WIDTH + 2 * B_QK_WIDTH + 2 * B_V_WIDTH + GATE_RANK + 2 * D_MODEL
_W_IN_PIECES = ((A_WIDTH, COL_A_Q), (A_KV_WIDTH, COL_A_K), (A_KV_WIDTH, COL_A_V), (A_WIDTH, COL_A_GATE),
                (B_QK_WIDTH, COL_B_Q), (B_QK_WIDTH, COL_B_K), (B_V_WIDTH, COL_B_V), (B_V_WIDTH, COL_B_GATE),
                (GATE_RANK, COL_B_LOW), (D_MODEL, COL_M_A), (D_MODEL, COL_M_B))
W_PREP_ROWS = 128
W_PREP_K = 256


def _w_in_prep_kernel(w_ref, o_ref):
    src = 0
    for width, dst in _W_IN_PIECES:
        if width % LANES == 0:
            for r in range(0, width, LANES):
                o_ref[:, dst + r:dst + r + LANES] = w_ref[src + r:src + r + LANES, :].T.astype(o_ref.dtype)
        else:
            slab = w_ref[src:src + LANES, :].T
            lane = lax.broadcasted_iota(jnp.int32, slab.shape, 1)
            o_ref[:, dst:dst + LANES] = jnp.where(lane < width, slab, 0.0).astype(o_ref.dtype)
        src += width


def _w_in_prep(w_in_t):
    return pl.pallas_call(
        _w_in_prep_kernel,
        grid=(D_MODEL // W_PREP_K,),
        in_specs=[pl.BlockSpec((IN_WIDTH, W_PREP_K), lambda i: (0, i))],
        out_specs=pl.BlockSpec((W_PREP_K, PROJ_WIDTH), lambda i: (i, 0)),
        out_shape=jax.ShapeDtypeStruct((D_MODEL, PROJ_WIDTH), jnp.bfloat16),
        compiler_params=pltpu.CompilerParams(dimension_semantics=("arbitrary",),
                                             vmem_limit_bytes=VMEM_LIMIT_BYTES),
        name="w_in_prep",
    )(w_in_t)


def _w_out_prep_kernel(a_ref, b_ref, c_ref, oa_ref, ob_ref, oc_ref):
    oa_ref[...] = a_ref[...].astype(oa_ref.dtype)
    ob_ref[...] = b_ref[...].astype(ob_ref.dtype)
    oc_ref[...] = c_ref[...].astype(oc_ref.dtype)


def _w_out_prep(wa, wb, wo):
    spec = pl.BlockSpec((2 * W_PREP_ROWS, D_MODEL), lambda i: (i, 0))
    shape = jax.ShapeDtypeStruct((D_MODEL, D_MODEL), jnp.bfloat16)
    return pl.pallas_call(
        _w_out_prep_kernel,
        grid=(D_MODEL // (2 * W_PREP_ROWS),),
        in_specs=[spec, spec, spec],
        out_specs=[spec, spec, spec],
        out_shape=[shape, shape, shape],
        compiler_params=pltpu.CompilerParams(dimension_semantics=("arbitrary",)),
        name="w_out_prep",
    )(wa, wb, wo)


TILE = 256
IN_CHUNK = 512


def _silu(x):
    return x * (0.5 * jnp.tanh(0.5 * x) + 0.5)


def _rope_slab(xs, cos, sin_signed, first_half):
    partner = jnp.where(first_half, pltpu.roll(xs, LANES - A_HEAD_DIM // 2, 1),
                        pltpu.roll(xs, A_HEAD_DIM // 2, 1))
    return xs * cos + partner * sin_signed


def _project_tile(x_ref, pos_ref, nw_ref, invf_ref, w_ref, mix_ref, mg_ref):
    x = x_ref[...]
    var = jnp.mean(x * x, axis=-1, keepdims=True)
    h = (x * lax.rsqrt(var + EPS) * nw_ref[...]).astype(jnp.bfloat16)
    yield 300

    half = A_HEAD_DIM // 2
    n_grp = LANES // half
    qr = x.shape[0] // n_grp
    lane = lax.broadcasted_iota(jnp.int32, (1, LANES), 1)
    grp = lane // half
    pos = pos_ref[...].astype(jnp.float32)
    pos_dense = pos[0:qr]
    for j in range(1, n_grp):
        pos_dense = jnp.where(grp == j, pos[j * qr:(j + 1) * qr], pos_dense)
    ang = pos_dense * invf_ref[...]
    cos_dense = jnp.cos(ang)
    sin_dense = jnp.sin(ang)

    def replicate(dense, j):
        only_j = jnp.where(grp == j, dense, 0.0)
        out = only_j
        for r in range(1, n_grp):
            out = out + pltpu.roll(only_j, r * half, 1)
        return out

    cos = jnp.concatenate([replicate(cos_dense, j) for j in range(n_grp)], axis=0)
    sin = jnp.concatenate([replicate(sin_dense, j) for j in range(n_grp)], axis=0)
    first_half = (lane % A_HEAD_DIM) < half
    sin_signed = jnp.where(first_half, -sin, sin)
    a_q_scale = (A_HEAD_DIM ** -0.5) * LOG2E
    b_q_scale = B_KEY_DIM ** -0.5
    yield 300

    for c0 in range(0, PROJ_WIDTH, IN_CHUNK):
        cw = min(IN_CHUNK, PROJ_WIDTH - c0)
        acc = jnp.dot(h, w_ref[:, c0:c0 + cw], preferred_element_type=jnp.float32)
        for s0 in range(0, cw, LANES):
            col = c0 + s0
            slab = acc[:, s0:s0 + LANES]
            if COL_A_Q <= col < COL_A_Q + A_WIDTH:
                slab = _rope_slab(slab, cos, sin_signed, first_half) * a_q_scale
            elif COL_A_K <= col < COL_A_K + A_KV_WIDTH:
                slab = _rope_slab(slab, cos, sin_signed, first_half)
            elif COL_B_Q <= col < COL_B_Q + B_QK_WIDTH:
                slab = slab * b_q_scale
            if col < MIX_COLS:
                mix_ref[:, col:col + LANES] = slab.astype(mix_ref.dtype)
            else:
                mg_ref[:, col - MIX_COLS:col - MIX_COLS + LANES] = slab.astype(mg_ref.dtype)
        yield cw


SWA_ROWS = 2 * BLOCK
SWA_COLS = 4 * 2 * BLOCK


def _swa_bias(sink_ref, g, first_block):
    row = lax.broadcasted_iota(jnp.int32, (SWA_ROWS, SWA_COLS), 0)
    col = lax.broadcasted_iota(jnp.int32, (SWA_ROWS, SWA_COLS), 1)
    qi, sub = row % BLOCK, row // BLOCK
    ki, hi = col % (2 * BLOCK), col // (2 * BLOCK)
    rel = qi + BLOCK - ki
    valid = (rel >= 0) & (rel < WINDOW)
    if first_block:
        valid = valid & (ki >= BLOCK)
    sink = jnp.zeros((SWA_ROWS, SWA_COLS), jnp.float32)
    for s in range(2):
        for i in range(4):
            sink = jnp.where((sub == s) & (hi == i), sink_ref[g * A_GROUP + s * 4 + i] * LOG2E, sink)
    return jnp.where(ki == 0, sink, jnp.where(valid, 0.0, NEG_INF))


def _swa_block(mix_ref, r0, k_prev, v_prev, first, bias_ref, oa_ref):
    bf16 = jnp.bfloat16
    rows = slice(r0, r0 + BLOCK)
    krow = lax.broadcasted_iota(jnp.int32, (2 * BLOCK, A_KV_WIDTH), 0)
    lane = lax.broadcasted_iota(jnp.int32, (2 * BLOCK, A_KV_WIDTH), 1)
    k = jnp.concatenate([k_prev, mix_ref[rows, COL_A_K:COL_A_K + A_KV_WIDTH]], axis=0)
    v = jnp.concatenate([v_prev, mix_ref[rows, COL_A_V:COL_A_V + A_KV_WIDTH]], axis=0)
    k = jnp.where(krow == 0, jnp.zeros_like(k), k)
    v = jnp.where(krow == 0, jnp.zeros_like(v), v)
    k_t = k.T
    v_swapped = pltpu.roll(v, A_HEAD_DIM, 1)
    lo = lane < A_HEAD_DIM
    ones_lo = jnp.where(lo, 1.0, 0.0).astype(bf16)
    ones_hi = jnp.where(lo, 0.0, 1.0).astype(bf16)
    zero_kt = jnp.zeros((A_HEAD_DIM, 2 * BLOCK), bf16)
    zero_v = jnp.zeros_like(v)

    for g in range(A_KV_HEADS):
        k_tg = k_t[g * A_HEAD_DIM:(g + 1) * A_HEAD_DIM, :]
        w_k = jnp.concatenate(
            [jnp.concatenate([k_tg if c == i else zero_kt for c in range(4)], axis=1) for i in range(4)],
            axis=0)
        c0 = COL_A_Q + g * A_GROUP * A_HEAD_DIM
        q_g = jnp.concatenate([mix_ref[rows, c0:c0 + 256], mix_ref[rows, c0 + 256:c0 + 512]], axis=0)
        s = jnp.dot(q_g, w_k, preferred_element_type=jnp.float32) + bias_ref[first, g]
        ps = []
        for i in range(4):
            s_i = s[:, i * 2 * BLOCK:(i + 1) * 2 * BLOCK]
            ps.append(jnp.exp2(s_i - jnp.max(s_i, axis=-1, keepdims=True)).astype(bf16))
        yield 550
        v_lo = jnp.where(lo, v if g == 0 else v_swapped, zero_v)
        v_hi = jnp.where(lo, zero_v, v_swapped if g == 0 else v)
        w_v = jnp.concatenate([jnp.concatenate([v_lo, ones_lo], axis=1),
                               jnp.concatenate([v_hi, ones_hi], axis=1)], axis=0)
        p = jnp.concatenate([jnp.concatenate([ps[0], ps[1]], axis=1),
                             jnp.concatenate([ps[2], ps[3]], axis=1)], axis=0)
        r = jnp.dot(p, w_v, preferred_element_type=jnp.float32)
        o_n = r[:, :LANES] / r[:, LANES:]
        for pair in range(2):
            for sub in range(2):
                off = g * A_GROUP * A_HEAD_DIM + sub * 256 + pair * LANES
                piece = o_n[pair * SWA_ROWS + sub * BLOCK:pair * SWA_ROWS + (sub + 1) * BLOCK, :]
                gate = mix_ref[rows, COL_A_GATE + off:COL_A_GATE + off + LANES].astype(jnp.float32)
                oa_ref[rows, off:off + LANES] = (piece * _silu(gate)).astype(oa_ref.dtype)
        yield 400


def _swa_tile(mix_ref, first_tile, kv_prev_ref, bias_ref, oa_ref):
    first = first_tile.astype(jnp.int32)
    k_prev = kv_prev_ref[:, :A_KV_WIDTH]
    v_prev = kv_prev_ref[:, A_KV_WIDTH:]
    for blk in range(TILE // BLOCK):
        r0 = blk * BLOCK
        yield from _swa_block(mix_ref, r0, k_prev, v_prev, first if blk == 0 else 0, bias_ref, oa_ref)
        k_prev = mix_ref[r0:r0 + BLOCK, COL_A_K:COL_A_K + A_KV_WIDTH]
        v_prev = mix_ref[r0:r0 + BLOCK, COL_A_V:COL_A_V + A_KV_WIDTH]
    kv_prev_ref[:, :A_KV_WIDTH] = k_prev
    kv_prev_ref[:, A_KV_WIDTH:] = v_prev


def _chunk_tril():
    ti = lax.broadcasted_iota(jnp.int32, (TILE, TILE), 0)
    tj = lax.broadcasted_iota(jnp.int32, (TILE, TILE), 1)
    return jnp.where((ti // CHUNK == tj // CHUNK) & (ti >= tj), 1.0, 0.0).astype(jnp.bfloat16)


def _chunk_cumsum_rows(x, tril_ref):
    hi = x.astype(jnp.bfloat16)
    lo = (x - hi.astype(jnp.float32)).astype(jnp.bfloat16)
    both = jnp.dot(tril_ref[...], jnp.concatenate([hi, lo], axis=1), preferred_element_type=jnp.float32)
    return both[:, :x.shape[1]] + both[:, x.shape[1]:]


def _gla_tile(mix_ref, first_tile, up_ref, bias_ref, nw_ref, ob_ref,
              state_ref, qe_ref, ke_ref, ks_ref, dcol_ref, inc_ref, oacc_ref, tril_ref):
    bf16 = jnp.bfloat16
    n_chunks = TILE // CHUNK

    gk = jnp.dot(mix_ref[:, COL_B_LOW:COL_B_LOW + LANES], up_ref[...],
                 preferred_element_type=jnp.float32) + bias_ref[...]
    log_a = (jnp.minimum(gk, 0.0) - jnp.log(1.0 + jnp.exp(-jnp.abs(gk)))) * (1.0 / GATE_TAU)
    b = _chunk_cumsum_rows(log_a, tril_ref)
    yield 700
    b_last_rows = [b[(c + 1) * CHUNK - 1:(c + 1) * CHUNK, :] for c in range(n_chunks)]
    b_last = jnp.concatenate([jnp.broadcast_to(r, (CHUNK, B_QK_WIDTH)) for r in b_last_rows], axis=0)
    qf = mix_ref[:, COL_B_Q:COL_B_Q + B_QK_WIDTH].astype(jnp.float32)
    kf = mix_ref[:, COL_B_K:COL_B_K + B_QK_WIDTH].astype(jnp.float32)
    qe_ref[...] = (qf * jnp.exp(b)).astype(bf16)
    ke_ref[...] = (kf * jnp.exp(-b)).astype(bf16)
    ks_ref[...] = (kf * jnp.exp(b_last - b)).astype(bf16)
    pad = jnp.zeros((LANES - n_chunks, B_QK_WIDTH), jnp.float32)
    dcol_ref[...] = jnp.exp(jnp.concatenate(b_last_rows + [pad], axis=0)).T
    yield 700

    ti = lax.broadcasted_iota(jnp.int32, (TILE, TILE), 0)
    tj = lax.broadcasted_iota(jnp.int32, (TILE, TILE), 1)
    keep = (ti // CHUNK == tj // CHUNK) & (ti >= tj)
    for h in range(B_HEADS):
        ks = slice(h * B_KEY_DIM, (h + 1) * B_KEY_DIM)
        vs = slice(COL_B_V + h * B_VAL_DIM, COL_B_V + (h + 1) * B_VAL_DIM)
        os_ = slice(h * B_VAL_DIM, (h + 1) * B_VAL_DIM)
        att = lax.dot_general(qe_ref[:, ks], ke_ref[:, ks], (((1,), (1,)), ((), ())),
                              preferred_element_type=jnp.float32)
        att = jnp.where(keep, att, 0.0).astype(bf16)
        oacc_ref[:, os_] = jnp.dot(att, mix_ref[:, vs], preferred_element_type=jnp.float32)
        for c in range(n_chunks):
            rows = slice(c * CHUNK, (c + 1) * CHUNK)
            inc_ref[c, h] = lax.dot_general(ks_ref[rows, ks], mix_ref[rows, vs], (((0,), (0,)), ((), ())),
                                            preferred_element_type=jnp.float32)
        yield 350

    for h in range(B_HEADS):
        ks = slice(h * B_KEY_DIM, (h + 1) * B_KEY_DIM)
        os_ = slice(h * B_VAL_DIM, (h + 1) * B_VAL_DIM)
        st = jnp.where(first_tile, 0.0, state_ref[h])
        for c in range(n_chunks):
            rows = slice(c * CHUNK, (c + 1) * CHUNK)
            oacc_ref[rows, os_] += jnp.dot(qe_ref[rows, ks], st.astype(bf16),
                                           preferred_element_type=jnp.float32)
            st = dcol_ref[ks, c:c + 1] * st + inc_ref[c, h]
        state_ref[h] = st
        yield 150

    for h in range(B_HEADS):
        os_ = slice(h * B_VAL_DIM, (h + 1) * B_VAL_DIM)
        o_h = oacc_ref[:, os_]
        var = jnp.mean(o_h * o_h, axis=-1, keepdims=True)
        o_n = o_h * lax.rsqrt(var + EPS) * nw_ref[...]
        gate = mix_ref[:, COL_B_GATE + h * B_VAL_DIM:COL_B_GATE + (h + 1) * B_VAL_DIM].astype(jnp.float32)
        ob_ref[:, os_] = (o_n * _silu(gate)).astype(ob_ref.dtype)
        yield 225


def _emit_interleaved(*streams):
    spent = [0.0] * len(streams)
    live = list(range(len(streams)))
    while live:
        k = min(live, key=lambda j: spent[j])
        try:
            spent[k] += next(streams[k])
        except StopIteration:
            live.remove(k)


def _mixer_kernel(tiles_per_seq, sink_ref, x_ref, pos_ref, nw_ref, invf_ref, w_ref, up_ref, gbias_ref,
                  bnw_ref, mg_ref, oa_ref, ob_ref, mix_a, mix_b, kv_prev_ref, sbias_ref,
                  state_ref, qe_ref, ke_ref, ks_ref, dcol_ref, inc_ref, oacc_ref, tril_ref):
    i = pl.program_id(0)
    n = pl.num_programs(0) - 1

    def project(dst):
        yield from _project_tile(x_ref, pos_ref, nw_ref, invf_ref, w_ref, dst, mg_ref)

    def mix(src):
        first_tile = ((i - 1) % tiles_per_seq) == 0
        yield from _swa_tile(src, first_tile, kv_prev_ref, sbias_ref, oa_ref)
        yield from _gla_tile(src, first_tile, up_ref, gbias_ref, bnw_ref, ob_ref,
                             state_ref, qe_ref, ke_ref, ks_ref, dcol_ref, inc_ref, oacc_ref, tril_ref)

    @pl.when(i == 0)
    def _():
        for g in range(A_KV_HEADS):
            sbias_ref[0, g] = _swa_bias(sink_ref, g, False)
            sbias_ref[1, g] = _swa_bias(sink_ref, g, True)
        kv_prev_ref[...] = jnp.zeros_like(kv_prev_ref)
        state_ref[...] = jnp.zeros_like(state_ref)
        tril_ref[...] = _chunk_tril()
        _emit_interleaved(project(mix_a))

    @pl.when((i > 0) & (i < n) & (i % 2 == 1))
    def _():
        _emit_interleaved(mix(mix_a), project(mix_b))

    @pl.when((i > 0) & (i < n) & (i % 2 == 0))
    def _():
        _emit_interleaved(mix(mix_b), project(mix_a))

    @pl.when(i == n)
    def _():
        _emit_interleaved(mix(mix_b))


def _mixer(x2, pos2, norm_w, inv_freq_row, w_perm, sinks, up_pad, gate_bias, b_norm_w, tiles_per_seq):
    m = x2.shape[0]
    n = m // TILE
    bf16 = jnp.bfloat16
    cur = lambda i: (jnp.minimum(i, n - 1), 0)
    lag = lambda i: (jnp.maximum(i - 1, 0), 0)
    const = lambda i: (0, 0)
    n_chunks = TILE // CHUNK
    return pl.pallas_call(
        functools.partial(_mixer_kernel, tiles_per_seq),
        grid=(n + 1,),
        in_specs=[
            pl.BlockSpec(memory_space=pltpu.SMEM),
            pl.BlockSpec((TILE, D_MODEL), cur),
            pl.BlockSpec((TILE, 1), cur),
            pl.BlockSpec((1, D_MODEL), const),
            pl.BlockSpec((1, LANES), const),
            pl.BlockSpec((D_MODEL, PROJ_WIDTH), const, pipeline_mode=pl.Buffered(1)),
            pl.BlockSpec((LANES, B_QK_WIDTH), const),
            pl.BlockSpec((1, B_QK_WIDTH), const),
            pl.BlockSpec((1, B_VAL_DIM), const),
        ],
        out_specs=[
            pl.BlockSpec((TILE, 2 * D_MODEL), cur),
            pl.BlockSpec((TILE, A_WIDTH), lag),
            pl.BlockSpec((TILE, B_V_WIDTH), lag),
        ],
        out_shape=[
            jax.ShapeDtypeStruct((m, 2 * D_MODEL), bf16),
            jax.ShapeDtypeStruct((m, A_WIDTH), bf16),
            jax.ShapeDtypeStruct((m, B_V_WIDTH), bf16),
        ],
        scratch_shapes=[
            pltpu.VMEM((TILE, MIX_COLS), bf16),
            pltpu.VMEM((TILE, MIX_COLS), bf16),
            pltpu.VMEM((BLOCK, 2 * A_KV_WIDTH), bf16),
            pltpu.VMEM((2, A_KV_HEADS, SWA_ROWS, SWA_COLS), jnp.float32),
            pltpu.VMEM((B_HEADS, B_KEY_DIM, B_VAL_DIM), jnp.float32),
            pltpu.VMEM((TILE, B_QK_WIDTH), bf16),
            pltpu.VMEM((TILE, B_QK_WIDTH), bf16),
            pltpu.VMEM((TILE, B_QK_WIDTH), bf16),
            pltpu.VMEM((B_QK_WIDTH, LANES), jnp.float32),
            pltpu.VMEM((n_chunks, B_HEADS, B_KEY_DIM, B_VAL_DIM), jnp.float32),
            pltpu.VMEM((TILE, B_V_WIDTH), jnp.float32),
            pltpu.VMEM((TILE, TILE), bf16),
        ],
        compiler_params=pltpu.CompilerParams(dimension_semantics=("arbitrary",),
                                             vmem_limit_bytes=VMEM_LIMIT_BYTES),
        name="mixer",
    )(sinks, x2, pos2, norm_w, inv_freq_row, w_perm, up_pad, gate_bias, b_norm_w)


OUT_TM = 1024
OUT_SUB = 256


def _sigmoid(x):
    return 0.5 * jnp.tanh(0.5 * x) + 0.5


def _out_proj_kernel(x_ref, oa_ref, ob_ref, ma_ref, mb_ref, wa_ref, wb_ref, wo_ref, fw_ref, o_ref):
    for r0 in range(0, OUT_TM, OUT_SUB):
        rows = slice(r0, r0 + OUT_SUB)
        y_a = jnp.dot(oa_ref[rows, :], wa_ref[...], preferred_element_type=jnp.float32)
        y_b = jnp.dot(ob_ref[rows, :], wb_ref[...], preferred_element_type=jnp.float32)
        merged = (_sigmoid(ma_ref[rows, :].astype(jnp.float32)) * y_a
                  + _sigmoid(mb_ref[rows, :].astype(jnp.float32)) * y_b)
        z = x_ref[rows, :] + jnp.dot(merged.astype(jnp.bfloat16), wo_ref[...],
                                     preferred_element_type=jnp.float32)
        var = jnp.mean(z * z, axis=-1, keepdims=True)
        o_ref[rows, :] = z * lax.rsqrt(var + EPS) * fw_ref[...]


def _out_proj(x2, o_a, o_b, mg, wa, wb, wo, final_w):
    m = x2.shape[0]
    row = lambda i: (i, 0)
    const = lambda i: (0, 0)
    return pl.pallas_call(
        _out_proj_kernel,
        grid=(m // OUT_TM,),
        in_specs=[
            pl.BlockSpec((OUT_TM, D_MODEL), row),
            pl.BlockSpec((OUT_TM, A_WIDTH), row),
            pl.BlockSpec((OUT_TM, B_V_WIDTH), row),
            pl.BlockSpec((OUT_TM, D_MODEL), lambda i: (i, 0)),
            pl.BlockSpec((OUT_TM, D_MODEL), lambda i: (i, 1)),
            pl.BlockSpec((A_WIDTH, D_MODEL), const),
            pl.BlockSpec((B_V_WIDTH, D_MODEL), const),
            pl.BlockSpec((D_MODEL, D_MODEL), const),
            pl.BlockSpec((1, D_MODEL), const),
        ],
        out_specs=pl.BlockSpec((OUT_TM, D_MODEL), row),
        out_shape=jax.ShapeDtypeStruct((m, D_MODEL), jnp.float32),
        compiler_params=pltpu.CompilerParams(dimension_semantics=("arbitrary",),
                                             vmem_limit_bytes=VMEM_LIMIT_BYTES),
        name="out_proj",
    )(x2, o_a, o_b, mg, mg, wa, wb, wo, final_w)


def kernel(x, positions, norm_w, w_in, a_sinks, b_gate_up, b_gate_bias, b_out_norm_w,
           w_a_proj, w_b_proj, w_out, final_norm_w):
    bsz, t, d = x.shape
    assert d == D_MODEL and t % (2 * TILE) == 0 and (bsz * t) % OUT_TM == 0
    depth = norm_w.shape[0]
    assert depth == 1
    bf16 = jnp.bfloat16

    half = A_HEAD_DIM // 2
    inv_freq = ROPE_THETA ** (-jnp.arange(half, dtype=jnp.float32) / half)
    inv_freq_row = jnp.tile(inv_freq, LANES // half)[None, :]
    pos2 = positions.reshape(bsz * t, 1)

    x2 = x.reshape(bsz * t, d)
    for layer in range(depth):
        w_perm = _w_in_prep(w_in[layer].T)
        wa, wb, wo = _w_out_prep(w_a_proj[layer], w_b_proj[layer], w_out[layer])
        up_pad = jnp.zeros((LANES, B_QK_WIDTH), bf16).at[:GATE_RANK].set(b_gate_up[layer].astype(bf16))
        mg, o_a, o_b = _mixer(x2, pos2, norm_w[layer][None, :], inv_freq_row, w_perm, a_sinks[layer],
                              up_pad, b_gate_bias[layer][None, :], b_out_norm_w[layer][None, :],
                              t // TILE)
        x2 = _out_proj(x2, o_a, o_b, mg, wa, wb, wo, final_norm_w[None, :])
    return x2.reshape(bsz, t, d)
```

```python
import functools

import jax
import jax.numpy as jnp
from jax import lax
from jax.experimental import pallas as pl
from jax.experimental.pallas import tpu as pltpu

D_MODEL = 1024
A_HEADS = 16
A_KV_HEADS = 2
A_HEAD_DIM = 64
A_GROUP = A_HEADS // A_KV_HEADS
A_WIDTH = A_HEADS * A_HEAD_DIM
A_KV_WIDTH = A_KV_HEADS * A_HEAD_DIM
WINDOW = 128
BLOCK = 128
ROPE_THETA = 10000.0

B_HEADS = 4
B_QK_WIDTH = D_MODEL // 2
B_V_WIDTH = D_MODEL
B_KEY_DIM = B_QK_WIDTH // B_HEADS
B_VAL_DIM = B_V_WIDTH // B_HEADS
GATE_RANK = 16
GATE_TAU = 16.0
CHUNK = 64

EPS = 1e-5
NEG_INF = -1e30
LOG2E = 1.4426950408889634

LANES = 128

COL_A_Q = 0
COL_A_GATE = 1024
COL_B_V = 2048
COL_B_GATE = 3072
COL_B_Q = 4096
COL_B_K = 4608
COL_A_K = 5120
COL_A_V = 5248
COL_B_LOW = 5376
MIX_COLS = 5504
COL_M_A = MIX_COLS
COL_M_B = MIX_COLS + D_MODEL
PROJ_WIDTH = MIX_COLS + 2 * D_MODEL

VMEM_LIMIT_BYTES = 60 * 1024 * 1024


IN_WIDTH = 2 * A_WIDTH + 2 * A_KV_WIDTH + 2 * B_QK_WIDTH + 2 * B_V_WIDTH + GATE_RANK + 2 * D_MODEL
_W_IN_PIECES = ((A_WIDTH, COL_A_Q), (A_KV_WIDTH, COL_A_K), (A_KV_WIDTH, COL_A_V), (A_WIDTH, COL_A_GATE),
                (B_QK_WIDTH, COL_B_Q), (B_QK_WIDTH, COL_B_K), (B_V_WIDTH, COL_B_V), (B_V_WIDTH, COL_B_GATE),
                (GATE_RANK, COL_B_LOW), (D_MODEL, COL_M_A), (D_MODEL, COL_M_B))
W_PREP_ROWS = 128
W_PREP_K = 256


def _w_in_prep_kernel(w_ref, o_ref):
    src = 0
    for width, dst in _W_IN_PIECES:
        if width % LANES == 0:
            for r in range(0, width, LANES):
                o_ref[:, dst + r:dst + r + LANES] = w_ref[src + r:src + r + LANES, :].T.astype(o_ref.dtype)
        else:
            slab = w_ref[src:src + LANES, :].T
            lane = lax.broadcasted_iota(jnp.int32, slab.shape, 1)
            o_ref[:, dst:dst + LANES] = jnp.where(lane < width, slab, 0.0).astype(o_ref.dtype)
        src += width


def _w_in_prep(w_in_t):
    return pl.pallas_call(
        _w_in_prep_kernel,
        grid=(D_MODEL // W_PREP_K,),
        in_specs=[pl.BlockSpec((IN_WIDTH, W_PREP_K), lambda i: (0, i))],
        out_specs=pl.BlockSpec((W_PREP_K, PROJ_WIDTH), lambda i: (i, 0)),
        out_shape=jax.ShapeDtypeStruct((D_MODEL, PROJ_WIDTH), jnp.bfloat16),
        compiler_params=pltpu.CompilerParams(dimension_semantics=("arbitrary",),
                                             vmem_limit_bytes=VMEM_LIMIT_BYTES),
        name="w_in_prep",
    )(w_in_t)


def _w_out_prep_kernel(a_ref, b_ref, c_ref, oa_ref, ob_ref, oc_ref):
    oa_ref[...] = a_ref[...].astype(oa_ref.dtype)
    ob_ref[...] = b_ref[...].astype(ob_ref.dtype)
    oc_ref[...] = c_ref[...].astype(oc_ref.dtype)


def _w_out_prep(wa, wb, wo):
    spec = pl.BlockSpec((2 * W_PREP_ROWS, D_MODEL), lambda i: (i, 0))
    shape = jax.ShapeDtypeStruct((D_MODEL, D_MODEL), jnp.bfloat16)
    return pl.pallas_call(
        _w_out_prep_kernel,
        grid=(D_MODEL // (2 * W_PREP_ROWS),),
        in_specs=[spec, spec, spec],
        out_specs=[spec, spec, spec],
        out_shape=[shape, shape, shape],
        compiler_params=pltpu.CompilerParams(dimension_semantics=("arbitrary",)),
        name="w_out_prep",
    )(wa, wb, wo)


TILE = 256
IN_CHUNK = 512


def _silu(x):
    return x * (0.5 * jnp.tanh(0.5 * x) + 0.5)


def _rope_slab(xs, cos, sin_signed, first_half):
    partner = jnp.where(first_half, pltpu.roll(xs, LANES - A_HEAD_DIM // 2, 1),
                        pltpu.roll(xs, A_HEAD_DIM // 2, 1))
    return xs * cos + partner * sin_signed


def _project_tile(x_ref, pos_ref, nw_ref, invf_ref, w_ref, mix_ref):
    x = x_ref[...]
    var = jnp.mean(x * x, axis=-1, keepdims=True)
    h = (x * lax.rsqrt(var + EPS) * nw_ref[...]).astype(jnp.bfloat16)
    yield 300

    half = A_HEAD_DIM // 2
    n_grp = LANES // half
    qr = x.shape[0] // n_grp
    lane = lax.broadcasted_iota(jnp.int32, (1, LANES), 1)
    grp = lane // half
    pos = pos_ref[...].astype(jnp.float32)
    pos_dense = pos[0:qr]
    for j in range(1, n_grp):
        pos_dense = jnp.where(grp == j, pos[j * qr:(j + 1) * qr], pos_dense)
    ang = pos_dense * invf_ref[...]
    cos_dense = jnp.cos(ang)
    sin_dense = jnp.sin(ang)

    def replicate(dense, j):
        only_j = jnp.where(grp == j, dense, 0.0)
        out = only_j
        for r in range(1, n_grp):
            out = out + pltpu.roll(only_j, r * half, 1)
        return out

    cos = jnp.concatenate([replicate(cos_dense, j) for j in range(n_grp)], axis=0)
    sin = jnp.concatenate([replicate(sin_dense, j) for j in range(n_grp)], axis=0)
    first_half = (lane % A_HEAD_DIM) < half
    sin_signed = jnp.where(first_half, -sin, sin)
    a_q_scale = (A_HEAD_DIM ** -0.5) * LOG2E
    b_q_scale = B_KEY_DIM ** -0.5
    yield 300

    for c0 in range(0, PROJ_WIDTH, IN_CHUNK):
        cw = min(IN_CHUNK, PROJ_WIDTH - c0)
        acc = jnp.dot(h, w_ref[:, c0:c0 + cw], preferred_element_type=jnp.float32)
        for s0 in range(0, cw, LANES):
            col = c0 + s0
            slab = acc[:, s0:s0 + LANES]
            if COL_A_Q <= col < COL_A_Q + A_WIDTH:
                slab = _rope_slab(slab, cos, sin_signed, first_half) * a_q_scale
            elif COL_A_K <= col < COL_A_K + A_KV_WIDTH:
                slab = _rope_slab(slab, cos, sin_signed, first_half)
            elif COL_B_Q <= col < COL_B_Q + B_QK_WIDTH:
                slab = slab * b_q_scale
            mix_ref[:, col:col + LANES] = slab.astype(mix_ref.dtype)
        yield cw


SWA_ROWS = 2 * BLOCK
SWA_COLS = 4 * 2 * BLOCK


def _swa_bias(sink_ref, g, first_block):
    row = lax.broadcasted_iota(jnp.int32, (SWA_ROWS, SWA_COLS), 0)
    col = lax.broadcasted_iota(jnp.int32, (SWA_ROWS, SWA_COLS), 1)
    qi, sub = row % BLOCK, row // BLOCK
    ki, hi = col % (2 * BLOCK), col // (2 * BLOCK)
    rel = qi + BLOCK - ki
    valid = (rel >= 0) & (rel < WINDOW)
    if first_block:
        valid = valid & (ki >= BLOCK)
    sink = jnp.zeros((SWA_ROWS, SWA_COLS), jnp.float32)
    for s in range(2):
        for i in range(4):
            sink = jnp.where((sub == s) & (hi == i), sink_ref[g * A_GROUP + s * 4 + i] * LOG2E, sink)
    return jnp.where(ki == 0, sink, jnp.where(valid, 0.0, NEG_INF))


def _swa_block(mix_ref, r0, k_prev, v_prev, first, bias_ref, oa_ref):
    bf16 = jnp.bfloat16
    rows = slice(r0, r0 + BLOCK)
    krow = lax.broadcasted_iota(jnp.int32, (2 * BLOCK, A_KV_WIDTH), 0)
    lane = lax.broadcasted_iota(jnp.int32, (2 * BLOCK, A_KV_WIDTH), 1)
    k = jnp.concatenate([k_prev, mix_ref[rows, COL_A_K:COL_A_K + A_KV_WIDTH]], axis=0)
    v = jnp.concatenate([v_prev, mix_ref[rows, COL_A_V:COL_A_V + A_KV_WIDTH]], axis=0)
    k = jnp.where(krow == 0, jnp.zeros_like(k), k)
    v = jnp.where(krow == 0, jnp.zeros_like(v), v)
    k_t = k.T
    v_swapped = pltpu.roll(v, A_HEAD_DIM, 1)
    lo = lane < A_HEAD_DIM
    ones_lo = jnp.where(lo, 1.0, 0.0).astype(bf16)
    ones_hi = jnp.where(lo, 0.0, 1.0).astype(bf16)
    zero_kt = jnp.zeros((A_HEAD_DIM, 2 * BLOCK), bf16)
    zero_v = jnp.zeros_like(v)

    for g in range(A_KV_HEADS):
        k_tg = k_t[g * A_HEAD_DIM:(g + 1) * A_HEAD_DIM, :]
        w_k = jnp.concatenate(
            [jnp.concatenate([k_tg if c == i else zero_kt for c in range(4)], axis=1) for i in range(4)],
            axis=0)
        c0 = COL_A_Q + g * A_GROUP * A_HEAD_DIM
        q_g = jnp.concatenate([mix_ref[rows, c0:c0 + 256], mix_ref[rows, c0 + 256:c0 + 512]], axis=0)
        s = jnp.dot(q_g, w_k, preferred_element_type=jnp.float32) + bias_ref[first, g]
        ps = []
        for i in range(4):
            s_i = s[:, i * 2 * BLOCK:(i + 1) * 2 * BLOCK]
            ps.append(jnp.exp2(s_i - jnp.max(s_i, axis=-1, keepdims=True)).astype(bf16))
        yield 550
        v_lo = jnp.where(lo, v if g == 0 else v_swapped, zero_v)
        v_hi = jnp.where(lo, zero_v, v_swapped if g == 0 else v)
        w_v = jnp.concatenate([jnp.concatenate([v_lo, ones_lo], axis=1),
                               jnp.concatenate([v_hi, ones_hi], axis=1)], axis=0)
        p = jnp.concatenate([jnp.concatenate([ps[0], ps[1]], axis=1),
                             jnp.concatenate([ps[2], ps[3]], axis=1)], axis=0)
        r = jnp.dot(p, w_v, preferred_element_type=jnp.float32)
        o_n = r[:, :LANES] / r[:, LANES:]
        for pair in range(2):
            for sub in range(2):
                off = g * A_GROUP * A_HEAD_DIM + sub * 256 + pair * LANES
                piece = o_n[pair * SWA_ROWS + sub * BLOCK:pair * SWA_ROWS + (sub + 1) * BLOCK, :]
                gate = mix_ref[rows, COL_A_GATE + off:COL_A_GATE + off + LANES].astype(jnp.float32)
                oa_ref[rows, off:off + LANES] = (piece * _silu(gate)).astype(oa_ref.dtype)
        yield 400


def _swa_tile(mix_ref, first_tile, kv_prev_ref, bias_ref, oa_ref):
    first = first_tile.astype(jnp.int32)
    k_prev = kv_prev_ref[:, :A_KV_WIDTH]
    v_prev = kv_prev_ref[:, A_KV_WIDTH:]
    for blk in range(TILE // BLOCK):
        r0 = blk * BLOCK
        yield from _swa_block(mix_ref, r0, k_prev, v_prev, first if blk == 0 else 0, bias_ref, oa_ref)
        k_prev = mix_ref[r0:r0 + BLOCK, COL_A_K:COL_A_K + A_KV_WIDTH]
        v_prev = mix_ref[r0:r0 + BLOCK, COL_A_V:COL_A_V + A_KV_WIDTH]
    kv_prev_ref[:, :A_KV_WIDTH] = k_prev
    kv_prev_ref[:, A_KV_WIDTH:] = v_prev


def _chunk_tril():
    ti = lax.broadcasted_iota(jnp.int32, (TILE, TILE), 0)
    tj = lax.broadcasted_iota(jnp.int32, (TILE, TILE), 1)
    return jnp.where((ti // CHUNK == tj // CHUNK) & (ti >= tj), 1.0, 0.0).astype(jnp.bfloat16)


def _chunk_cumsum_rows(x, tril_ref):
    hi = x.astype(jnp.bfloat16)
    lo = (x - hi.astype(jnp.float32)).astype(jnp.bfloat16)
    both = jnp.dot(tril_ref[...], jnp.concatenate([hi, lo], axis=1), preferred_element_type=jnp.float32)
    return both[:, :x.shape[1]] + both[:, x.shape[1]:]


def _gla_tile(mix_ref, first_tile, up_ref, bias_ref, nw_ref, ob_ref,
              state_ref, qe_ref, ke_ref, ks_ref, dcol_ref, inc_ref, oacc_ref, tril_ref):
    bf16 = jnp.bfloat16
    n_chunks = TILE // CHUNK

    gk = jnp.dot(mix_ref[:, COL_B_LOW:COL_B_LOW + LANES], up_ref[...],
                 preferred_element_type=jnp.float32) + bias_ref[...]
    log_a = (jnp.minimum(gk, 0.0) - jnp.log(1.0 + jnp.exp(-jnp.abs(gk)))) * (1.0 / GATE_TAU)
    b = _chunk_cumsum_rows(log_a, tril_ref)
    yield 700
    b_last_rows = [b[(c + 1) * CHUNK - 1:(c + 1) * CHUNK, :] for c in range(n_chunks)]
    b_last = jnp.concatenate([jnp.broadcast_to(r, (CHUNK, B_QK_WIDTH)) for r in b_last_rows], axis=0)
    qf = mix_ref[:, COL_B_Q:COL_B_Q + B_QK_WIDTH].astype(jnp.float32)
    kf = mix_ref[:, COL_B_K:COL_B_K + B_QK_WIDTH].astype(jnp.float32)
    qe_ref[...] = (qf * jnp.exp(b)).astype(bf16)
    ke_ref[...] = (kf * jnp.exp(-b)).astype(bf16)
    ks_ref[...] = (kf * jnp.exp(b_last - b)).astype(bf16)
    pad = jnp.zeros((LANES - n_chunks, B_QK_WIDTH), jnp.float32)
    dcol_ref[...] = jnp.exp(jnp.concatenate(b_last_rows + [pad], axis=0)).T
    yield 700

    ti = lax.broadcasted_iota(jnp.int32, (TILE, TILE), 0)
    tj = lax.broadcasted_iota(jnp.int32, (TILE, TILE), 1)
    keep = (ti // CHUNK == tj // CHUNK) & (ti >= tj)
    for h in range(B_HEADS):
        ks = slice(h * B_KEY_DIM, (h + 1) * B_KEY_DIM)
        vs = slice(COL_B_V + h * B_VAL_DIM, COL_B_V + (h + 1) * B_VAL_DIM)
        os_ = slice(h * B_VAL_DIM, (h + 1) * B_VAL_DIM)
        att = lax.dot_general(qe_ref[:, ks], ke_ref[:, ks], (((1,), (1,)), ((), ())),
                              preferred_element_type=jnp.float32)
        att = jnp.where(keep, att, 0.0).astype(bf16)
        oacc_ref[:, os_] = jnp.dot(att, mix_ref[:, vs], preferred_element_type=jnp.float32)
        for c in range(n_chunks):
            rows = slice(c * CHUNK, (c + 1) * CHUNK)
            inc_ref[c, h] = lax.dot_general(ks_ref[rows, ks], mix_ref[rows, vs], (((0,), (0,)), ((), ())),
                                            preferred_element_type=jnp.float32)
        yield 350

    for h in range(B_HEADS):
        ks = slice(h * B_KEY_DIM, (h + 1) * B_KEY_DIM)
        os_ = slice(h * B_VAL_DIM, (h + 1) * B_VAL_DIM)
        st = jnp.where(first_tile, 0.0, state_ref[h])
        for c in range(n_chunks):
            rows = slice(c * CHUNK, (c + 1) * CHUNK)
            oacc_ref[rows, os_] += jnp.dot(qe_ref[rows, ks], st.astype(bf16),
                                           preferred_element_type=jnp.float32)
            st = dcol_ref[ks, c:c + 1] * st + inc_ref[c, h]
        state_ref[h] = st
        yield 150

    for h in range(B_HEADS):
        os_ = slice(h * B_VAL_DIM, (h + 1) * B_VAL_DIM)
        o_h = oacc_ref[:, os_]
        var = jnp.mean(o_h * o_h, axis=-1, keepdims=True)
        o_n = o_h * lax.rsqrt(var + EPS) * nw_ref[...]
        gate = mix_ref[:, COL_B_GATE + h * B_VAL_DIM:COL_B_GATE + (h + 1) * B_VAL_DIM].astype(jnp.float32)
        ob_ref[:, os_] = (o_n * _silu(gate)).astype(ob_ref.dtype)
        yield 225


def _emit_interleaved(streams, weights):
    spent = [0.0] * len(streams)
    live = list(range(len(streams)))
    while live:
        k = min(live, key=lambda j: spent[j])
        try:
            spent[k] += next(streams[k]) * weights[k]
        except StopIteration:
            live.remove(k)


def _sigmoid(x):
    return 0.5 * jnp.tanh(0.5 * x) + 0.5


def _branch_proj(o_ref, w_ref):
    return jnp.dot(o_ref[...], w_ref[...], preferred_element_type=jnp.float32)


def _merge_out_tile(mix_ref, y_a, y_b, xlag_ref, wo_ref, fw_ref, out_ref):
    m_a = mix_ref[:, COL_M_A:COL_M_A + D_MODEL].astype(jnp.float32)
    m_b = mix_ref[:, COL_M_B:COL_M_B + D_MODEL].astype(jnp.float32)
    merged = (_sigmoid(m_a) * y_a + _sigmoid(m_b) * y_b).astype(jnp.bfloat16)
    yield 500
    z = xlag_ref[...] + jnp.dot(merged, wo_ref[...], preferred_element_type=jnp.float32)
    var = jnp.mean(z * z, axis=-1, keepdims=True)
    out_ref[...] = z * lax.rsqrt(var + EPS) * fw_ref[...]
    yield 1100


PROJECT_COST = 600 + PROJ_WIDTH
FINISH_COST = 3800 + 4100 + 1200 + 1600


def _block_kernel(tiles_per_seq, sink_ref, x_ref, xlag_ref, pos_ref, nw_ref, invf_ref, w_ref, up_ref,
                  gbias_ref, bnw_ref, wa_ref, wb_ref, wo_ref, fw_ref, out_ref,
                  mix_a, mix_b, oa_ref, ob_ref, kv_prev_ref, sbias_ref,
                  state_ref, qe_ref, ke_ref, ks_ref, dcol_ref, inc_ref, oacc_ref, tril_ref):
    i = pl.program_id(0)
    n = pl.num_programs(0) - 1

    def project(dst):
        yield from _project_tile(x_ref, pos_ref, nw_ref, invf_ref, w_ref, dst)

    def finish(src):
        first_tile = ((i - 1) % tiles_per_seq) == 0
        yield from _swa_tile(src, first_tile, kv_prev_ref, sbias_ref, oa_ref)
        y_a = _branch_proj(oa_ref, wa_ref)
        yield 600
        yield from _gla_tile(src, first_tile, up_ref, gbias_ref, bnw_ref, ob_ref,
                             state_ref, qe_ref, ke_ref, ks_ref, dcol_ref, inc_ref, oacc_ref, tril_ref)
        y_b = _branch_proj(ob_ref, wb_ref)
        yield 600
        yield from _merge_out_tile(src, y_a, y_b, xlag_ref, wo_ref, fw_ref, out_ref)

    both = (1.0, FINISH_COST / PROJECT_COST)

    @pl.when(i == 0)
    def _():
        for g in range(A_KV_HEADS):
            sbias_ref[0, g] = _swa_bias(sink_ref, g, False)
            sbias_ref[1, g] = _swa_bias(sink_ref, g, True)
        kv_prev_ref[...] = jnp.zeros_like(kv_prev_ref)
        state_ref[...] = jnp.zeros_like(state_ref)
        tril_ref[...] = _chunk_tril()
        _emit_interleaved([project(mix_a)], (1.0,))

    @pl.when((i > 0) & (i < n) & (i % 2 == 1))
    def _():
        _emit_interleaved([finish(mix_a), project(mix_b)], both)

    @pl.when((i > 0) & (i < n) & (i % 2 == 0))
    def _():
        _emit_interleaved([finish(mix_b), project(mix_a)], both)

    @pl.when(i == n)
    def _():
        _emit_interleaved([finish(mix_b)], (1.0,))


def _block(x2, pos2, norm_w, inv_freq_row, w_perm, sinks, up_pad, gate_bias, b_norm_w, wa, wb, wo,
           final_w, tiles_per_seq):
    m = x2.shape[0]
    n = m // TILE
    bf16 = jnp.bfloat16
    cur = lambda i: (jnp.minimum(i, n - 1), 0)
    lag = lambda i: (jnp.maximum(i - 1, 0), 0)
    const = lambda i: (0, 0)
    resident = dict(pipeline_mode=pl.Buffered(1))
    n_chunks = TILE // CHUNK
    return pl.pallas_call(
        functools.partial(_block_kernel, tiles_per_seq),
        grid=(n + 1,),
        in_specs=[
            pl.BlockSpec(memory_space=pltpu.SMEM),
            pl.BlockSpec((TILE, D_MODEL), cur),
            pl.BlockSpec((TILE, D_MODEL), lag),
            pl.BlockSpec((TILE, 1), cur),
            pl.BlockSpec((1, D_MODEL), const),
            pl.BlockSpec((1, LANES), const),
            pl.BlockSpec((D_MODEL, PROJ_WIDTH), const, **resident),
            pl.BlockSpec((LANES, B_QK_WIDTH), const),
            pl.BlockSpec((1, B_QK_WIDTH), const),
            pl.BlockSpec((1, B_VAL_DIM), const),
            pl.BlockSpec((A_WIDTH, D_MODEL), const, **resident),
            pl.BlockSpec((B_V_WIDTH, D_MODEL), const, **resident),
            pl.BlockSpec((D_MODEL, D_MODEL), const, **resident),
            pl.BlockSpec((1, D_MODEL), const),
        ],
        out_specs=pl.BlockSpec((TILE, D_MODEL), lag),
        out_shape=jax.ShapeDtypeStruct((m, D_MODEL), jnp.float32),
        scratch_shapes=[
            pltpu.VMEM((TILE, PROJ_WIDTH), bf16),
            pltpu.VMEM((TILE, PROJ_WIDTH), bf16),
            pltpu.VMEM((TILE, A_WIDTH), bf16),
            pltpu.VMEM((TILE, B_V_WIDTH), bf16),
            pltpu.VMEM((BLOCK, 2 * A_KV_WIDTH), bf16),
            pltpu.VMEM((2, A_KV_HEADS, SWA_ROWS, SWA_COLS), jnp.float32),
            pltpu.VMEM((B_HEADS, B_KEY_DIM, B_VAL_DIM), jnp.float32),
            pltpu.VMEM((TILE, B_QK_WIDTH), bf16),
            pltpu.VMEM((TILE, B_QK_WIDTH), bf16),
            pltpu.VMEM((TILE, B_QK_WIDTH), bf16),
            pltpu.VMEM((B_QK_WIDTH, LANES), jnp.float32),
            pltpu.VMEM((n_chunks, B_HEADS, B_KEY_DIM, B_VAL_DIM), jnp.float32),
            pltpu.VMEM((TILE, B_V_WIDTH), jnp.float32),
            pltpu.VMEM((TILE, TILE), bf16),
        ],
        compiler_params=pltpu.CompilerParams(dimension_semantics=("arbitrary",),
                                             vmem_limit_bytes=VMEM_LIMIT_BYTES),
        name="block",
    )(sinks, x2, x2, pos2, norm_w, inv_freq_row, w_perm, up_pad, gate_bias, b_norm_w, wa, wb, wo, final_w)


def kernel(x, positions, norm_w, w_in, a_sinks, b_gate_up, b_gate_bias, b_out_norm_w,
           w_a_proj, w_b_proj, w_out, final_norm_w):
    bsz, t, d = x.shape
    assert d == D_MODEL and t % (2 * TILE) == 0
    depth = norm_w.shape[0]
    assert depth == 1
    bf16 = jnp.bfloat16

    half = A_HEAD_DIM // 2
    inv_freq = ROPE_THETA ** (-jnp.arange(half, dtype=jnp.float32) / half)
    inv_freq_row = jnp.tile(inv_freq, LANES // half)[None, :]
    pos2 = positions.reshape(bsz * t, 1)

    x2 = x.reshape(bsz * t, d)
    for layer in range(depth):
        w_perm = _w_in_prep(w_in[layer].T)
        wa, wb, wo = _w_out_prep(w_a_proj[layer], w_b_proj[layer], w_out[layer])
        up_pad = jnp.zeros((LANES, B_QK_WIDTH), bf16).at[:GATE_RANK].set(b_gate_up[layer].astype(bf16))
        x2 = _block(x2, pos2, norm_w[layer][None, :], inv_freq_row, w_perm, a_sinks[layer], up_pad,
                    b_gate_bias[layer][None, :], b_out_norm_w[layer][None, :], wa, wb, wo,
                    final_norm_w[None, :], t // TILE)
    return x2.reshape(bsz, t, d)
```

```python
import functools

import jax
import jax.numpy as jnp
from jax import lax
from jax.experimental import pallas as pl
from jax.experimental.pallas import tpu as pltpu

D_MODEL = 1024
A_HEADS = 16
A_KV_HEADS = 2
A_HEAD_DIM = 64
A_GROUP = A_HEADS // A_KV_HEADS
A_WIDTH = A_HEADS * A_HEAD_DIM
A_KV_WIDTH = A_KV_HEADS * A_HEAD_DIM
WINDOW = 128
BLOCK = 128
ROPE_THETA = 10000.0

B_HEADS = 4
B_QK_WIDTH = D_MODEL // 2
B_V_WIDTH = D_MODEL
B_KEY_DIM = B_QK_WIDTH // B_HEADS
B_VAL_DIM = B_V_WIDTH // B_HEADS
GATE_RANK = 16
GATE_TAU = 16.0
CHUNK = 64

EPS = 1e-5
NEG_INF = -1e30
LOG2E = 1.4426950408889634

LANES = 128

COL_A_Q = 0
COL_A_GATE = 1024
COL_B_V = 2048
COL_B_GATE = 3072
COL_B_Q = 4096
COL_B_K = 4608
COL_A_K = 5120
COL_A_V = 5248
COL_B_LOW = 5376
MIX_COLS = 5504
COL_M_A = MIX_COLS
COL_M_B = MIX_COLS + D_MODEL
PROJ_WIDTH = MIX_COLS + 2 * D_MODEL

VMEM_LIMIT_BYTES = 60 * 1024 * 1024


IN_WIDTH = 2 * A_WIDTH + 2 * A_KV_WIDTH + 2 * B_QK_WIDTH + 2 * B_V_WIDTH + GATE_RANK + 2 * D_MODEL
_W_IN_PIECES = ((A_WIDTH, COL_A_Q), (A_KV_WIDTH, COL_A_K), (A_KV_WIDTH, COL_A_V), (A_WIDTH, COL_A_GATE),
                (B_QK_WIDTH, COL_B_Q), (B_QK_WIDTH, COL_B_K), (B_V_WIDTH, COL_B_V), (B_V_WIDTH, COL_B_GATE),
                (GATE_RANK, COL_B_LOW), (D_MODEL, COL_M_A), (D_MODEL, COL_M_B))
W_SLAB = 128
W_STAGE_BUFS = 4


def _stream_weights(w_in_hbm, wa_hbm, wb_hbm, wo_hbm, w_ref, wa_ref, wb_ref, wo_ref, stage_ref, sem_ref):
    jobs = []
    src = 0
    for width, dst in _W_IN_PIECES:
        for r in range(0, width, W_SLAB):
            jobs.append((w_in_hbm, src + r, True, w_ref, dst + r, min(W_SLAB, width - r)))
        src += width
    jobs.sort(key=lambda j: j[4])
    for hbm, dst_ref in ((wa_hbm, wa_ref), (wb_hbm, wb_ref), (wo_hbm, wo_ref)):
        for r in range(0, D_MODEL, W_SLAB):
            jobs.append((hbm, r, False, dst_ref, r, W_SLAB))

    def copy(j):
        hbm, row = jobs[j][0], jobs[j][1]
        slot = j % W_STAGE_BUFS
        return pltpu.make_async_copy(hbm.at[pl.ds(row, W_SLAB), :], stage_ref.at[slot], sem_ref.at[slot])

    for j in range(min(W_STAGE_BUFS, len(jobs))):
        copy(j).start()
    for j, (_, _, transpose, dst_ref, off, valid) in enumerate(jobs):
        copy(j).wait()
        slab = stage_ref[j % W_STAGE_BUFS]
        if transpose:
            slab = slab.T
            if valid < W_SLAB:
                lane = lax.broadcasted_iota(jnp.int32, slab.shape, 1)
                slab = jnp.where(lane < valid, slab, 0.0)
            dst_ref[:, off:off + W_SLAB] = slab.astype(dst_ref.dtype)
        else:
            dst_ref[off:off + W_SLAB, :] = slab.astype(dst_ref.dtype)
        if j + W_STAGE_BUFS < len(jobs):
            copy(j + W_STAGE_BUFS).start()


TILE = 256
IN_CHUNK = 512


def _silu(x):
    return x * (0.5 * jnp.tanh(0.5 * x) + 0.5)


def _rope_slab(xs, cos, sin_signed, first_half):
    partner = jnp.where(first_half, pltpu.roll(xs, LANES - A_HEAD_DIM // 2, 1),
                        pltpu.roll(xs, A_HEAD_DIM // 2, 1))
    return xs * cos + partner * sin_signed


def _project_tile(x_ref, pos_ref, nw_ref, invf_ref, w_ref, mix_ref):
    x = x_ref[...]
    var = jnp.mean(x * x, axis=-1, keepdims=True)
    h = (x * lax.rsqrt(var + EPS) * nw_ref[...]).astype(jnp.bfloat16)
    yield 300

    half = A_HEAD_DIM // 2
    n_grp = LANES // half
    qr = x.shape[0] // n_grp
    lane = lax.broadcasted_iota(jnp.int32, (1, LANES), 1)
    grp = lane // half
    pos = pos_ref[...].astype(jnp.float32)
    pos_dense = pos[0:qr]
    for j in range(1, n_grp):
        pos_dense = jnp.where(grp == j, pos[j * qr:(j + 1) * qr], pos_dense)
    ang = pos_dense * invf_ref[...]
    cos_dense = jnp.cos(ang)
    sin_dense = jnp.sin(ang)

    def replicate(dense, j):
        only_j = jnp.where(grp == j, dense, 0.0)
        out = only_j
        for r in range(1, n_grp):
            out = out + pltpu.roll(only_j, r * half, 1)
        return out

    cos = jnp.concatenate([replicate(cos_dense, j) for j in range(n_grp)], axis=0)
    sin = jnp.concatenate([replicate(sin_dense, j) for j in range(n_grp)], axis=0)
    first_half = (lane % A_HEAD_DIM) < half
    sin_signed = jnp.where(first_half, -sin, sin)
    a_q_scale = (A_HEAD_DIM ** -0.5) * LOG2E
    b_q_scale = B_KEY_DIM ** -0.5
    yield 300

    for c0 in range(0, PROJ_WIDTH, IN_CHUNK):
        cw = min(IN_CHUNK, PROJ_WIDTH - c0)
        acc = jnp.dot(h, w_ref[:, c0:c0 + cw], preferred_element_type=jnp.float32)
        for s0 in range(0, cw, LANES):
            col = c0 + s0
            slab = acc[:, s0:s0 + LANES]
            if COL_A_Q <= col < COL_A_Q + A_WIDTH:
                slab = _rope_slab(slab, cos, sin_signed, first_half) * a_q_scale
            elif COL_A_K <= col < COL_A_K + A_KV_WIDTH:
                slab = _rope_slab(slab, cos, sin_signed, first_half)
            elif COL_B_Q <= col < COL_B_Q + B_QK_WIDTH:
                slab = slab * b_q_scale
            mix_ref[:, col:col + LANES] = slab.astype(mix_ref.dtype)
        yield cw


SWA_ROWS = 2 * BLOCK
SWA_COLS = 4 * 2 * BLOCK


def _swa_bias(sink_ref, g, first_block):
    row = lax.broadcasted_iota(jnp.int32, (SWA_ROWS, SWA_COLS), 0)
    col = lax.broadcasted_iota(jnp.int32, (SWA_ROWS, SWA_COLS), 1)
    qi, sub = row % BLOCK, row // BLOCK
    ki, hi = col % (2 * BLOCK), col // (2 * BLOCK)
    rel = qi + BLOCK - ki
    valid = (rel >= 0) & (rel < WINDOW)
    if first_block:
        valid = valid & (ki >= BLOCK)
    sink = jnp.zeros((SWA_ROWS, SWA_COLS), jnp.float32)
    for s in range(2):
        for i in range(4):
            sink = jnp.where((sub == s) & (hi == i), sink_ref[g * A_GROUP + s * 4 + i] * LOG2E, sink)
    return jnp.where(ki == 0, sink, jnp.where(valid, 0.0, NEG_INF))


def _swa_block(mix_ref, r0, k_prev, v_prev, first, bias_ref, oa_ref):
    bf16 = jnp.bfloat16
    rows = slice(r0, r0 + BLOCK)
    krow = lax.broadcasted_iota(jnp.int32, (2 * BLOCK, A_KV_WIDTH), 0)
    lane = lax.broadcasted_iota(jnp.int32, (2 * BLOCK, A_KV_WIDTH), 1)
    k = jnp.concatenate([k_prev, mix_ref[rows, COL_A_K:COL_A_K + A_KV_WIDTH]], axis=0)
    v = jnp.concatenate([v_prev, mix_ref[rows, COL_A_V:COL_A_V + A_KV_WIDTH]], axis=0)
    k = jnp.where(krow == 0, jnp.zeros_like(k), k)
    v = jnp.where(krow == 0, jnp.zeros_like(v), v)
    k_t = k.T
    v_swapped = pltpu.roll(v, A_HEAD_DIM, 1)
    lo = lane < A_HEAD_DIM
    ones_lo = jnp.where(lo, 1.0, 0.0).astype(bf16)
    ones_hi = jnp.where(lo, 0.0, 1.0).astype(bf16)
    zero_kt = jnp.zeros((A_HEAD_DIM, 2 * BLOCK), bf16)
    zero_v = jnp.zeros_like(v)

    for g in range(A_KV_HEADS):
        k_tg = k_t[g * A_HEAD_DIM:(g + 1) * A_HEAD_DIM, :]
        w_k = jnp.concatenate(
            [jnp.concatenate([k_tg if c == i else zero_kt for c in range(4)], axis=1) for i in range(4)],
            axis=0)
        c0 = COL_A_Q + g * A_GROUP * A_HEAD_DIM
        q_g = jnp.concatenate([mix_ref[rows, c0:c0 + 256], mix_ref[rows, c0 + 256:c0 + 512]], axis=0)
        s = jnp.dot(q_g, w_k, preferred_element_type=jnp.float32) + bias_ref[first, g]
        ps = []
        for i in range(4):
            s_i = s[:, i * 2 * BLOCK:(i + 1) * 2 * BLOCK]
            ps.append(jnp.exp2(s_i - jnp.max(s_i, axis=-1, keepdims=True)).astype(bf16))
        yield 550
        v_lo = jnp.where(lo, v if g == 0 else v_swapped, zero_v)
        v_hi = jnp.where(lo, zero_v, v_swapped if g == 0 else v)
        w_v = jnp.concatenate([jnp.concatenate([v_lo, ones_lo], axis=1),
                               jnp.concatenate([v_hi, ones_hi], axis=1)], axis=0)
        p = jnp.concatenate([jnp.concatenate([ps[0], ps[1]], axis=1),
                             jnp.concatenate([ps[2], ps[3]], axis=1)], axis=0)
        r = jnp.dot(p, w_v, preferred_element_type=jnp.float32)
        o_n = r[:, :LANES] / r[:, LANES:]
        for pair in range(2):
            for sub in range(2):
                off = g * A_GROUP * A_HEAD_DIM + sub * 256 + pair * LANES
                piece = o_n[pair * SWA_ROWS + sub * BLOCK:pair * SWA_ROWS + (sub + 1) * BLOCK, :]
                gate = mix_ref[rows, COL_A_GATE + off:COL_A_GATE + off + LANES].astype(jnp.float32)
                oa_ref[rows, off:off + LANES] = (piece * _silu(gate)).astype(oa_ref.dtype)
        yield 400


def _swa_tile(mix_ref, first_tile, kv_prev_ref, bias_ref, oa_ref):
    first = first_tile.astype(jnp.int32)
    k_prev = kv_prev_ref[:, :A_KV_WIDTH]
    v_prev = kv_prev_ref[:, A_KV_WIDTH:]
    for blk in range(TILE // BLOCK):
        r0 = blk * BLOCK
        yield from _swa_block(mix_ref, r0, k_prev, v_prev, first if blk == 0 else 0, bias_ref, oa_ref)
        k_prev = mix_ref[r0:r0 + BLOCK, COL_A_K:COL_A_K + A_KV_WIDTH]
        v_prev = mix_ref[r0:r0 + BLOCK, COL_A_V:COL_A_V + A_KV_WIDTH]
    kv_prev_ref[:, :A_KV_WIDTH] = k_prev
    kv_prev_ref[:, A_KV_WIDTH:] = v_prev


def _chunk_tril():
    ti = lax.broadcasted_iota(jnp.int32, (TILE, TILE), 0)
    tj = lax.broadcasted_iota(jnp.int32, (TILE, TILE), 1)
    return jnp.where((ti // CHUNK == tj // CHUNK) & (ti >= tj), 1.0, 0.0).astype(jnp.bfloat16)


def _chunk_cumsum_rows(x, tril_ref):
    hi = x.astype(jnp.bfloat16)
    lo = (x - hi.astype(jnp.float32)).astype(jnp.bfloat16)
    both = jnp.dot(tril_ref[...], jnp.concatenate([hi, lo], axis=1), preferred_element_type=jnp.float32)
    return both[:, :x.shape[1]] + both[:, x.shape[1]:]


def _gla_tile(mix_ref, first_tile, up_ref, bias_ref, nw_ref, ob_ref,
              state_ref, qe_ref, ke_ref, ks_ref, dcol_ref, inc_ref, oacc_ref, tril_ref):
    bf16 = jnp.bfloat16
    n_chunks = TILE // CHUNK

    gk = jnp.dot(mix_ref[:, COL_B_LOW:COL_B_LOW + LANES], up_ref[...],
                 preferred_element_type=jnp.float32) + bias_ref[...]
    log_a = (jnp.minimum(gk, 0.0) - jnp.log(1.0 + jnp.exp(-jnp.abs(gk)))) * (1.0 / GATE_TAU)
    b = _chunk_cumsum_rows(log_a, tril_ref)
    yield 700
    b_last_rows = [b[(c + 1) * CHUNK - 1:(c + 1) * CHUNK, :] for c in range(n_chunks)]
    b_last = jnp.concatenate([jnp.broadcast_to(r, (CHUNK, B_QK_WIDTH)) for r in b_last_rows], axis=0)
    qf = mix_ref[:, COL_B_Q:COL_B_Q + B_QK_WIDTH].astype(jnp.float32)
    kf = mix_ref[:, COL_B_K:COL_B_K + B_QK_WIDTH].astype(jnp.float32)
    qe_ref[...] = (qf * jnp.exp(b)).astype(bf16)
    ke_ref[...] = (kf * jnp.exp(-b)).astype(bf16)
    ks_ref[...] = (kf * jnp.exp(b_last - b)).astype(bf16)
    pad = jnp.zeros((LANES - n_chunks, B_QK_WIDTH), jnp.float32)
    dcol_ref[...] = jnp.exp(jnp.concatenate(b_last_rows + [pad], axis=0)).T
    yield 700

    ti = lax.broadcasted_iota(jnp.int32, (TILE, TILE), 0)
    tj = lax.broadcasted_iota(jnp.int32, (TILE, TILE), 1)
    keep = (ti // CHUNK == tj // CHUNK) & (ti >= tj)
    for h in range(B_HEADS):
        ks = slice(h * B_KEY_DIM, (h + 1) * B_KEY_DIM)
        vs = slice(COL_B_V + h * B_VAL_DIM, COL_B_V + (h + 1) * B_VAL_DIM)
        os_ = slice(h * B_VAL_DIM, (h + 1) * B_VAL_DIM)
        att = lax.dot_general(qe_ref[:, ks], ke_ref[:, ks], (((1,), (1,)), ((), ())),
                              preferred_element_type=jnp.float32)
        att = jnp.where(keep, att, 0.0).astype(bf16)
        oacc_ref[:, os_] = jnp.dot(att, mix_ref[:, vs], preferred_element_type=jnp.float32)
        for c in range(n_chunks):
            rows = slice(c * CHUNK, (c + 1) * CHUNK)
            inc_ref[c, h] = lax.dot_general(ks_ref[rows, ks], mix_ref[rows, vs], (((0,), (0,)), ((), ())),
                                            preferred_element_type=jnp.float32)
        yield 350

    for h in range(B_HEADS):
        ks = slice(h * B_KEY_DIM, (h + 1) * B_KEY_DIM)
        os_ = slice(h * B_VAL_DIM, (h + 1) * B_VAL_DIM)
        st = jnp.where(first_tile, 0.0, state_ref[h])
        for c in range(n_chunks):
            rows = slice(c * CHUNK, (c + 1) * CHUNK)
            oacc_ref[rows, os_] += jnp.dot(qe_ref[rows, ks], st.astype(bf16),
                                           preferred_element_type=jnp.float32)
            st = dcol_ref[ks, c:c + 1] * st + inc_ref[c, h]
        state_ref[h] = st
        yield 150

    for h in range(B_HEADS):
        os_ = slice(h * B_VAL_DIM, (h + 1) * B_VAL_DIM)
        o_h = oacc_ref[:, os_]
        var = jnp.mean(o_h * o_h, axis=-1, keepdims=True)
        o_n = o_h * lax.rsqrt(var + EPS) * nw_ref[...]
        gate = mix_ref[:, COL_B_GATE + h * B_VAL_DIM:COL_B_GATE + (h + 1) * B_VAL_DIM].astype(jnp.float32)
        ob_ref[:, os_] = (o_n * _silu(gate)).astype(ob_ref.dtype)
        yield 225


def _emit_interleaved(streams, weights):
    spent = [0.0] * len(streams)
    live = list(range(len(streams)))
    while live:
        k = min(live, key=lambda j: spent[j])
        try:
            spent[k] += next(streams[k]) * weights[k]
        except StopIteration:
            live.remove(k)


def _sigmoid(x):
    return 0.5 * jnp.tanh(0.5 * x) + 0.5


def _branch_proj(o_ref, w_ref):
    return jnp.dot(o_ref[...], w_ref[...], preferred_element_type=jnp.float32)


def _merge_out_tile(mix_ref, y_a, y_b, xlag_ref, wo_ref, fw_ref, out_ref):
    m_a = mix_ref[:, COL_M_A:COL_M_A + D_MODEL].astype(jnp.float32)
    m_b = mix_ref[:, COL_M_B:COL_M_B + D_MODEL].astype(jnp.float32)
    merged = (_sigmoid(m_a) * y_a + _sigmoid(m_b) * y_b).astype(jnp.bfloat16)
    yield 500
    z = xlag_ref[...] + jnp.dot(merged, wo_ref[...], preferred_element_type=jnp.float32)
    var = jnp.mean(z * z, axis=-1, keepdims=True)
    out_ref[...] = z * lax.rsqrt(var + EPS) * fw_ref[...]
    yield 1100


PROJECT_COST = 600 + PROJ_WIDTH
FINISH_COST = 3800 + 4100 + 1200 + 1600


def _block_kernel(tiles_per_seq, sink_ref, x_ref, xlag_ref, pos_ref, nw_ref, invf_ref, w_in_hbm, up_ref,
                  gbias_ref, bnw_ref, wa_hbm, wb_hbm, wo_hbm, fw_ref, out_ref,
                  w_ref, wa_ref, wb_ref, wo_ref, stage_ref, wsem_ref,
                  mix_a, mix_b, oa_ref, ob_ref, kv_prev_ref, sbias_ref,
                  state_ref, qe_ref, ke_ref, ks_ref, dcol_ref, inc_ref, oacc_ref, tril_ref):
    i = pl.program_id(0)
    n = pl.num_programs(0) - 1

    def project(dst):
        yield from _project_tile(x_ref, pos_ref, nw_ref, invf_ref, w_ref, dst)

    def finish(src):
        first_tile = ((i - 1) % tiles_per_seq) == 0
        yield from _swa_tile(src, first_tile, kv_prev_ref, sbias_ref, oa_ref)
        y_a = _branch_proj(oa_ref, wa_ref)
        yield 600
        yield from _gla_tile(src, first_tile, up_ref, gbias_ref, bnw_ref, ob_ref,
                             state_ref, qe_ref, ke_ref, ks_ref, dcol_ref, inc_ref, oacc_ref, tril_ref)
        y_b = _branch_proj(ob_ref, wb_ref)
        yield 600
        yield from _merge_out_tile(src, y_a, y_b, xlag_ref, wo_ref, fw_ref, out_ref)

    both = (1.0, FINISH_COST / PROJECT_COST)

    @pl.when(i == 0)
    def _():
        for g in range(A_KV_HEADS):
            sbias_ref[0, g] = _swa_bias(sink_ref, g, False)
            sbias_ref[1, g] = _swa_bias(sink_ref, g, True)
        kv_prev_ref[...] = jnp.zeros_like(kv_prev_ref)
        state_ref[...] = jnp.zeros_like(state_ref)
        tril_ref[...] = _chunk_tril()
        _stream_weights(w_in_hbm, wa_hbm, wb_hbm, wo_hbm, w_ref, wa_ref, wb_ref, wo_ref, stage_ref, wsem_ref)
        _emit_interleaved([project(mix_a)], (1.0,))

    @pl.when((i > 0) & (i < n) & (i % 2 == 1))
    def _():
        _emit_interleaved([finish(mix_a), project(mix_b)], both)

    @pl.when((i > 0) & (i < n) & (i % 2 == 0))
    def _():
        _emit_interleaved([finish(mix_b), project(mix_a)], both)

    @pl.when(i == n)
    def _():
        _emit_interleaved([finish(mix_b)], (1.0,))


def _block(x2, pos2, norm_w, inv_freq_row, w_in_t, sinks, up_pad, gate_bias, b_norm_w, wa, wb, wo,
           final_w, tiles_per_seq):
    m = x2.shape[0]
    n = m // TILE
    assert w_in_t.shape == (IN_WIDTH, D_MODEL) and w_in_t.dtype == jnp.float32
    bf16 = jnp.bfloat16
    cur = lambda i: (jnp.minimum(i, n - 1), 0)
    lag = lambda i: (jnp.maximum(i - 1, 0), 0)
    const = lambda i: (0, 0)
    n_chunks = TILE // CHUNK
    return pl.pallas_call(
        functools.partial(_block_kernel, tiles_per_seq),
        grid=(n + 1,),
        in_specs=[
            pl.BlockSpec(memory_space=pltpu.SMEM),
            pl.BlockSpec((TILE, D_MODEL), cur),
            pl.BlockSpec((TILE, D_MODEL), lag),
            pl.BlockSpec((TILE, 1), cur),
            pl.BlockSpec((1, D_MODEL), const),
            pl.BlockSpec((1, LANES), const),
            pl.BlockSpec(memory_space=pl.ANY),
            pl.BlockSpec((LANES, B_QK_WIDTH), const),
            pl.BlockSpec((1, B_QK_WIDTH), const),
            pl.BlockSpec((1, B_VAL_DIM), const),
            pl.BlockSpec(memory_space=pl.ANY),
            pl.BlockSpec(memory_space=pl.ANY),
            pl.BlockSpec(memory_space=pl.ANY),
            pl.BlockSpec((1, D_MODEL), const),
        ],
        out_specs=pl.BlockSpec((TILE, D_MODEL), lag),
        out_shape=jax.ShapeDtypeStruct((m, D_MODEL), jnp.float32),
        scratch_shapes=[
            pltpu.VMEM((D_MODEL, PROJ_WIDTH), bf16),
            pltpu.VMEM((A_WIDTH, D_MODEL), bf16),
            pltpu.VMEM((B_V_WIDTH, D_MODEL), bf16),
            pltpu.VMEM((D_MODEL, D_MODEL), bf16),
            pltpu.VMEM((W_STAGE_BUFS, W_SLAB, D_MODEL), jnp.float32),
            pltpu.SemaphoreType.DMA((W_STAGE_BUFS,)),
            pltpu.VMEM((TILE, PROJ_WIDTH), bf16),
            pltpu.VMEM((TILE, PROJ_WIDTH), bf16),
            pltpu.VMEM((TILE, A_WIDTH), bf16),
            pltpu.VMEM((TILE, B_V_WIDTH), bf16),
            pltpu.VMEM((BLOCK, 2 * A_KV_WIDTH), bf16),
            pltpu.VMEM((2, A_KV_HEADS, SWA_ROWS, SWA_COLS), jnp.float32),
            pltpu.VMEM((B_HEADS, B_KEY_DIM, B_VAL_DIM), jnp.float32),
            pltpu.VMEM((TILE, B_QK_WIDTH), bf16),
            pltpu.VMEM((TILE, B_QK_WIDTH), bf16),
            pltpu.VMEM((TILE, B_QK_WIDTH), bf16),
            pltpu.VMEM((B_QK_WIDTH, LANES), jnp.float32),
            pltpu.VMEM((n_chunks, B_HEADS, B_KEY_DIM, B_VAL_DIM), jnp.float32),
            pltpu.VMEM((TILE, B_V_WIDTH), jnp.float32),
            pltpu.VMEM((TILE, TILE), bf16),
        ],
        compiler_params=pltpu.CompilerParams(dimension_semantics=("arbitrary",),
                                             vmem_limit_bytes=VMEM_LIMIT_BYTES),
        name="block",
    )(sinks, x2, x2, pos2, norm_w, inv_freq_row, w_in_t, up_pad, gate_bias, b_norm_w, wa, wb, wo, final_w)


def kernel(x, positions, norm_w, w_in, a_sinks, b_gate_up, b_gate_bias, b_out_norm_w,
           w_a_proj, w_b_proj, w_out, final_norm_w):
    bsz, t, d = x.shape
    assert d == D_MODEL and t % (2 * TILE) == 0
    depth = norm_w.shape[0]
    assert depth == 1
    bf16 = jnp.bfloat16

    half = A_HEAD_DIM // 2
    inv_freq = ROPE_THETA ** (-jnp.arange(half, dtype=jnp.float32) / half)
    inv_freq_row = jnp.tile(inv_freq, LANES // half)[None, :]
    pos2 = positions.reshape(bsz * t, 1)

    x2 = x.reshape(bsz * t, d)
    for layer in range(depth):
        up_pad = jnp.zeros((LANES, B_QK_WIDTH), bf16).at[:GATE_RANK].set(b_gate_up[layer].astype(bf16))
        x2 = _block(x2, pos2, norm_w[layer][None, :], inv_freq_row, w_in[layer].T, a_sinks[layer], up_pad,
                    b_gate_bias[layer][None, :], b_out_norm_w[layer][None, :], w_a_proj[layer],
                    w_b_proj[layer], w_out[layer], final_norm_w[None, :], t // TILE)
    return x2.reshape(bsz, t, d)
```

```python
import functools

import jax
import jax.numpy as jnp
from jax import lax
from jax.experimental import pallas as pl
from jax.experimental.pallas import tpu as pltpu

D_MODEL = 1024
A_HEADS = 16
A_KV_HEADS = 2
A_HEAD_DIM = 64
A_GROUP = A_HEADS // A_KV_HEADS
A_WIDTH = A_HEADS * A_HEAD_DIM
A_KV_WIDTH = A_KV_HEADS * A_HEAD_DIM
WINDOW = 128
BLOCK = 128
ROPE_THETA = 10000.0

B_HEADS = 4
B_QK_WIDTH = D_MODEL // 2
B_V_WIDTH = D_MODEL
B_KEY_DIM = B_QK_WIDTH // B_HEADS
B_VAL_DIM = B_V_WIDTH // B_HEADS
GATE_RANK = 16
GATE_TAU = 16.0
CHUNK = 64

EPS = 1e-5
NEG_INF = -1e30
LOG2E = 1.4426950408889634

LANES = 128

COL_A_Q = 0
COL_A_GATE = 1024
COL_B_V = 2048
COL_B_GATE = 3072
COL_B_Q = 4096
COL_B_K = 4608
COL_A_K = 5120
COL_A_V = 5248
COL_B_LOW = 5376
MIX_COLS = 5504
COL_M_A = MIX_COLS
COL_M_B = MIX_COLS + D_MODEL
PROJ_WIDTH = MIX_COLS + 2 * D_MODEL

VMEM_LIMIT_BYTES = 60 * 1024 * 1024


IN_WIDTH = 2 * A_WIDTH + 2 * A_KV_WIDTH + 2 * B_QK_WIDTH + 2 * B_V_WIDTH + GATE_RANK + 2 * D_MODEL
_W_IN_PIECES = ((A_WIDTH, COL_A_Q), (A_KV_WIDTH, COL_A_K), (A_KV_WIDTH, COL_A_V), (A_WIDTH, COL_A_GATE),
                (B_QK_WIDTH, COL_B_Q), (B_QK_WIDTH, COL_B_K), (B_V_WIDTH, COL_B_V), (B_V_WIDTH, COL_B_GATE),
                (GATE_RANK, COL_B_LOW), (D_MODEL, COL_M_A), (D_MODEL, COL_M_B))
W_SLAB = 128
W_STAGE_BUFS = 4


def _stream_weights(w_in_hbm, wa_hbm, wb_hbm, wo_hbm, w_ref, wa_ref, wb_ref, wo_ref, stage_ref, sem_ref):
    jobs = []
    src = 0
    for width, dst in _W_IN_PIECES:
        for r in range(0, width, W_SLAB):
            jobs.append((w_in_hbm, src + r, True, w_ref, dst + r, min(W_SLAB, width - r)))
        src += width
    jobs.sort(key=lambda j: j[4])
    for hbm, dst_ref in ((wa_hbm, wa_ref), (wb_hbm, wb_ref), (wo_hbm, wo_ref)):
        for r in range(0, D_MODEL, W_SLAB):
            jobs.append((hbm, r, False, dst_ref, r, W_SLAB))

    def copy(j):
        hbm, row = jobs[j][0], jobs[j][1]
        slot = j % W_STAGE_BUFS
        return pltpu.make_async_copy(hbm.at[pl.ds(row, W_SLAB), :], stage_ref.at[slot], sem_ref.at[slot])

    for j in range(min(W_STAGE_BUFS, len(jobs))):
        copy(j).start()
    for j, (_, _, transpose, dst_ref, off, valid) in enumerate(jobs):
        copy(j).wait()
        slab = stage_ref[j % W_STAGE_BUFS]
        if transpose:
            slab = slab.T
            if valid < W_SLAB:
                lane = lax.broadcasted_iota(jnp.int32, slab.shape, 1)
                slab = jnp.where(lane < valid, slab, 0.0)
            dst_ref[:, off:off + W_SLAB] = slab.astype(dst_ref.dtype)
        else:
            dst_ref[off:off + W_SLAB, :] = slab.astype(dst_ref.dtype)
        if j + W_STAGE_BUFS < len(jobs):
            copy(j + W_STAGE_BUFS).start()
        yield (dst_ref, off + W_SLAB)


TILE = 256
IN_CHUNK = 512


def _silu(x):
    return x * (0.5 * jnp.tanh(0.5 * x) + 0.5)


def _rope_slab(xs, cos, sin_signed, first_half):
    partner = jnp.where(first_half, pltpu.roll(xs, LANES - A_HEAD_DIM // 2, 1),
                        pltpu.roll(xs, A_HEAD_DIM // 2, 1))
    return xs * cos + partner * sin_signed


def _project_tile(x_ref, pos_ref, batch_row, nw_ref, invf_ref, w_ref, mix_ref):
    x = x_ref[...]
    var = jnp.mean(x * x, axis=-1, keepdims=True)
    h = (x * lax.rsqrt(var + EPS) * nw_ref[...]).astype(jnp.bfloat16)
    yield 300

    half = A_HEAD_DIM // 2
    n_grp = LANES // half
    qr = x.shape[0] // n_grp
    lane = lax.broadcasted_iota(jnp.int32, (1, LANES), 1)
    grp = lane // half
    assert x.shape[0] % LANES == 0 and LANES % qr == 0
    prow = pos_ref[pl.ds(batch_row, 1), :].astype(jnp.float32)
    n_cols = x.shape[0] // LANES
    pos_t = jnp.concatenate([prow[:, k * LANES:(k + 1) * LANES] for k in range(n_cols)]
                            + [jnp.zeros((LANES - n_cols, LANES), jnp.float32)], axis=0).T

    def pos_col(j):
        first = j * qr
        return pos_t[first % LANES:first % LANES + qr, first // LANES:first // LANES + 1]

    pos_dense = pos_col(0)
    for j in range(1, n_grp):
        pos_dense = jnp.where(grp == j, pos_col(j), pos_dense)
    ang = pos_dense * invf_ref[...]
    cos_dense = jnp.cos(ang)
    sin_dense = jnp.sin(ang)

    def replicate(dense, j):
        only_j = jnp.where(grp == j, dense, 0.0)
        out = only_j
        for r in range(1, n_grp):
            out = out + pltpu.roll(only_j, r * half, 1)
        return out

    cos = jnp.concatenate([replicate(cos_dense, j) for j in range(n_grp)], axis=0)
    sin = jnp.concatenate([replicate(sin_dense, j) for j in range(n_grp)], axis=0)
    first_half = (lane % A_HEAD_DIM) < half
    sin_signed = jnp.where(first_half, -sin, sin)
    a_q_scale = (A_HEAD_DIM ** -0.5) * LOG2E
    b_q_scale = B_KEY_DIM ** -0.5
    yield 300

    for c0 in range(0, PROJ_WIDTH, IN_CHUNK):
        cw = min(IN_CHUNK, PROJ_WIDTH - c0)
        acc = jnp.dot(h, w_ref[:, c0:c0 + cw], preferred_element_type=jnp.float32)
        for s0 in range(0, cw, LANES):
            col = c0 + s0
            slab = acc[:, s0:s0 + LANES]
            if COL_A_Q <= col < COL_A_Q + A_WIDTH:
                slab = _rope_slab(slab, cos, sin_signed, first_half) * a_q_scale
            elif COL_A_K <= col < COL_A_K + A_KV_WIDTH:
                slab = _rope_slab(slab, cos, sin_signed, first_half)
            elif COL_B_Q <= col < COL_B_Q + B_QK_WIDTH:
                slab = slab * b_q_scale
            mix_ref[:, col:col + LANES] = slab.astype(mix_ref.dtype)
        yield cw


SWA_ROWS = 2 * BLOCK
SWA_COLS = 4 * 2 * BLOCK


def _swa_bias(sink_ref, g, first_block):
    row = lax.broadcasted_iota(jnp.int32, (SWA_ROWS, SWA_COLS), 0)
    col = lax.broadcasted_iota(jnp.int32, (SWA_ROWS, SWA_COLS), 1)
    qi, sub = row % BLOCK, row // BLOCK
    ki, hi = col % (2 * BLOCK), col // (2 * BLOCK)
    rel = qi + BLOCK - ki
    valid = (rel >= 0) & (rel < WINDOW)
    if first_block:
        valid = valid & (ki >= BLOCK)
    sink = jnp.zeros((SWA_ROWS, SWA_COLS), jnp.float32)
    for s in range(2):
        for i in range(4):
            sink = jnp.where((sub == s) & (hi == i), sink_ref[g * A_GROUP + s * 4 + i] * LOG2E, sink)
    return jnp.where(ki == 0, sink, jnp.where(valid, 0.0, NEG_INF))


def _swa_block(mix_ref, r0, k_prev, v_prev, first, bias_ref, oa_ref):
    bf16 = jnp.bfloat16
    rows = slice(r0, r0 + BLOCK)
    krow = lax.broadcasted_iota(jnp.int32, (2 * BLOCK, A_KV_WIDTH), 0)
    lane = lax.broadcasted_iota(jnp.int32, (2 * BLOCK, A_KV_WIDTH), 1)
    k = jnp.concatenate([k_prev, mix_ref[rows, COL_A_K:COL_A_K + A_KV_WIDTH]], axis=0)
    v = jnp.concatenate([v_prev, mix_ref[rows, COL_A_V:COL_A_V + A_KV_WIDTH]], axis=0)
    k = jnp.where(krow == 0, jnp.zeros_like(k), k)
    v = jnp.where(krow == 0, jnp.zeros_like(v), v)
    k_t = k.T
    v_swapped = pltpu.roll(v, A_HEAD_DIM, 1)
    lo = lane < A_HEAD_DIM
    ones_lo = jnp.where(lo, 1.0, 0.0).astype(bf16)
    ones_hi = jnp.where(lo, 0.0, 1.0).astype(bf16)
    zero_kt = jnp.zeros((A_HEAD_DIM, 2 * BLOCK), bf16)
    zero_v = jnp.zeros_like(v)

    for g in range(A_KV_HEADS):
        k_tg = k_t[g * A_HEAD_DIM:(g + 1) * A_HEAD_DIM, :]
        w_k = jnp.concatenate(
            [jnp.concatenate([k_tg if c == i else zero_kt for c in range(4)], axis=1) for i in range(4)],
            axis=0)
        c0 = COL_A_Q + g * A_GROUP * A_HEAD_DIM
        q_g = jnp.concatenate([mix_ref[rows, c0:c0 + 256], mix_ref[rows, c0 + 256:c0 + 512]], axis=0)
        s = jnp.dot(q_g, w_k, preferred_element_type=jnp.float32) + bias_ref[first, g]
        ps = []
        for i in range(4):
            s_i = s[:, i * 2 * BLOCK:(i + 1) * 2 * BLOCK]
            ps.append(jnp.exp2(s_i - jnp.max(s_i, axis=-1, keepdims=True)).astype(bf16))
        yield 550
        v_lo = jnp.where(lo, v if g == 0 else v_swapped, zero_v)
        v_hi = jnp.where(lo, zero_v, v_swapped if g == 0 else v)
        w_v = jnp.concatenate([jnp.concatenate([v_lo, ones_lo], axis=1),
                               jnp.concatenate([v_hi, ones_hi], axis=1)], axis=0)
        p = jnp.concatenate([jnp.concatenate([ps[0], ps[1]], axis=1),
                             jnp.concatenate([ps[2], ps[3]], axis=1)], axis=0)
        r = jnp.dot(p, w_v, preferred_element_type=jnp.float32)
        o_n = r[:, :LANES] / r[:, LANES:]
        for pair in range(2):
            for sub in range(2):
                off = g * A_GROUP * A_HEAD_DIM + sub * 256 + pair * LANES
                piece = o_n[pair * SWA_ROWS + sub * BLOCK:pair * SWA_ROWS + (sub + 1) * BLOCK, :]
                gate = mix_ref[rows, COL_A_GATE + off:COL_A_GATE + off + LANES].astype(jnp.float32)
                oa_ref[rows, off:off + LANES] = (piece * _silu(gate)).astype(oa_ref.dtype)
        yield 400


def _swa_tile(mix_ref, first_tile, kv_prev_ref, bias_ref, oa_ref):
    first = first_tile.astype(jnp.int32)
    k_prev = kv_prev_ref[:, :A_KV_WIDTH]
    v_prev = kv_prev_ref[:, A_KV_WIDTH:]
    for blk in range(TILE // BLOCK):
        r0 = blk * BLOCK
        yield from _swa_block(mix_ref, r0, k_prev, v_prev, first if blk == 0 else 0, bias_ref, oa_ref)
        k_prev = mix_ref[r0:r0 + BLOCK, COL_A_K:COL_A_K + A_KV_WIDTH]
        v_prev = mix_ref[r0:r0 + BLOCK, COL_A_V:COL_A_V + A_KV_WIDTH]
    kv_prev_ref[:, :A_KV_WIDTH] = k_prev
    kv_prev_ref[:, A_KV_WIDTH:] = v_prev


def _chunk_tril():
    ti = lax.broadcasted_iota(jnp.int32, (TILE, TILE), 0)
    tj = lax.broadcasted_iota(jnp.int32, (TILE, TILE), 1)
    return jnp.where((ti // CHUNK == tj // CHUNK) & (ti >= tj), 1.0, 0.0).astype(jnp.bfloat16)


def _chunk_cumsum_rows(x, tril_ref):
    hi = x.astype(jnp.bfloat16)
    lo = (x - hi.astype(jnp.float32)).astype(jnp.bfloat16)
    both = jnp.dot(tril_ref[...], jnp.concatenate([hi, lo], axis=1), preferred_element_type=jnp.float32)
    return both[:, :x.shape[1]] + both[:, x.shape[1]:]


def _gla_tile(mix_ref, first_tile, up_ref, bias_ref, nw_ref, ob_ref,
              state_ref, qe_ref, ke_ref, ks_ref, dcol_ref, inc_ref, oacc_ref, tril_ref):
    bf16 = jnp.bfloat16
    n_chunks = TILE // CHUNK

    gk = jnp.dot(mix_ref[:, COL_B_LOW:COL_B_LOW + LANES], up_ref[...],
                 preferred_element_type=jnp.float32) + bias_ref[...]
    log_a = (jnp.minimum(gk, 0.0) - jnp.log(1.0 + jnp.exp(-jnp.abs(gk)))) * (1.0 / GATE_TAU)
    b = _chunk_cumsum_rows(log_a, tril_ref)
    yield 700
    b_last_rows = [b[(c + 1) * CHUNK - 1:(c + 1) * CHUNK, :] for c in range(n_chunks)]
    b_last = jnp.concatenate([jnp.broadcast_to(r, (CHUNK, B_QK_WIDTH)) for r in b_last_rows], axis=0)
    qf = mix_ref[:, COL_B_Q:COL_B_Q + B_QK_WIDTH].astype(jnp.float32)
    kf = mix_ref[:, COL_B_K:COL_B_K + B_QK_WIDTH].astype(jnp.float32)
    qe_ref[...] = (qf * jnp.exp(b)).astype(bf16)
    ke_ref[...] = (kf * jnp.exp(-b)).astype(bf16)
    ks_ref[...] = (kf * jnp.exp(b_last - b)).astype(bf16)
    pad = jnp.zeros((LANES - n_chunks, B_QK_WIDTH), jnp.float32)
    dcol_ref[...] = jnp.exp(jnp.concatenate(b_last_rows + [pad], axis=0)).T
    yield 700

    ti = lax.broadcasted_iota(jnp.int32, (TILE, TILE), 0)
    tj = lax.broadcasted_iota(jnp.int32, (TILE, TILE), 1)
    keep = (ti // CHUNK == tj // CHUNK) & (ti >= tj)
    for h in range(B_HEADS):
        ks = slice(h * B_KEY_DIM, (h + 1) * B_KEY_DIM)
        vs = slice(COL_B_V + h * B_VAL_DIM, COL_B_V + (h + 1) * B_VAL_DIM)
        os_ = slice(h * B_VAL_DIM, (h + 1) * B_VAL_DIM)
        att = lax.dot_general(qe_ref[:, ks], ke_ref[:, ks], (((1,), (1,)), ((), ())),
                              preferred_element_type=jnp.float32)
        att = jnp.where(keep, att, 0.0).astype(bf16)
        oacc_ref[:, os_] = jnp.dot(att, mix_ref[:, vs], preferred_element_type=jnp.float32)
        for c in range(n_chunks):
            rows = slice(c * CHUNK, (c + 1) * CHUNK)
            inc_ref[c, h] = lax.dot_general(ks_ref[rows, ks], mix_ref[rows, vs], (((0,), (0,)), ((), ())),
                                            preferred_element_type=jnp.float32)
        yield 350

    for h in range(B_HEADS):
        ks = slice(h * B_KEY_DIM, (h + 1) * B_KEY_DIM)
        os_ = slice(h * B_VAL_DIM, (h + 1) * B_VAL_DIM)
        st = jnp.where(first_tile, 0.0, state_ref[h])
        for c in range(n_chunks):
            rows = slice(c * CHUNK, (c + 1) * CHUNK)
            oacc_ref[rows, os_] += jnp.dot(qe_ref[rows, ks], st.astype(bf16),
                                           preferred_element_type=jnp.float32)
            st = dcol_ref[ks, c:c + 1] * st + inc_ref[c, h]
        state_ref[h] = st
        yield 150

    for h in range(B_HEADS):
        os_ = slice(h * B_VAL_DIM, (h + 1) * B_VAL_DIM)
        o_h = oacc_ref[:, os_]
        var = jnp.mean(o_h * o_h, axis=-1, keepdims=True)
        o_n = o_h * lax.rsqrt(var + EPS) * nw_ref[...]
        gate = mix_ref[:, COL_B_GATE + h * B_VAL_DIM:COL_B_GATE + (h + 1) * B_VAL_DIM].astype(jnp.float32)
        ob_ref[:, os_] = (o_n * _silu(gate)).astype(ob_ref.dtype)
        yield 225


def _emit_interleaved(streams, weights):
    spent = [0.0] * len(streams)
    live = list(range(len(streams)))
    while live:
        k = min(live, key=lambda j: spent[j])
        try:
            spent[k] += next(streams[k]) * weights[k]
        except StopIteration:
            live.remove(k)


def _sigmoid(x):
    return 0.5 * jnp.tanh(0.5 * x) + 0.5


def _branch_proj(o_ref, w_ref):
    return jnp.dot(o_ref[...], w_ref[...], preferred_element_type=jnp.float32)


def _merge_out_tile(mix_ref, y_a, y_b, xlag_ref, wo_ref, fw_ref, out_ref):
    m_a = mix_ref[:, COL_M_A:COL_M_A + D_MODEL].astype(jnp.float32)
    m_b = mix_ref[:, COL_M_B:COL_M_B + D_MODEL].astype(jnp.float32)
    merged = (_sigmoid(m_a) * y_a + _sigmoid(m_b) * y_b).astype(jnp.bfloat16)
    yield 500
    z = xlag_ref[...] + jnp.dot(merged, wo_ref[...], preferred_element_type=jnp.float32)
    var = jnp.mean(z * z, axis=-1, keepdims=True)
    out_ref[...] = z * lax.rsqrt(var + EPS) * fw_ref[...]
    yield 1100


PROJECT_COST = 600 + PROJ_WIDTH
FINISH_COST = 3800 + 4100 + 1200 + 1600


def _block_kernel(tiles_per_seq, sink_ref, x_ref, xlag_ref, pos_ref, nw_ref, invf_ref, w_in_hbm, up_f32_ref,
                  gbias_ref, bnw_ref, wa_hbm, wb_hbm, wo_hbm, fw_ref, out_ref,
                  w_ref, wa_ref, wb_ref, wo_ref, stage_ref, wsem_ref, up_ref,
                  mix_a, mix_b, oa_ref, ob_ref, kv_prev_ref, sbias_ref,
                  state_ref, qe_ref, ke_ref, ks_ref, dcol_ref, inc_ref, oacc_ref, tril_ref):
    i = pl.program_id(0)
    n = pl.num_programs(0) - 1

    def project(dst):
        yield from _project_tile(x_ref, pos_ref, i // tiles_per_seq, nw_ref, invf_ref, w_ref, dst)

    def finish(src):
        first_tile = ((i - 1) % tiles_per_seq) == 0
        yield from _swa_tile(src, first_tile, kv_prev_ref, sbias_ref, oa_ref)
        y_a = _branch_proj(oa_ref, wa_ref)
        yield 600
        yield from _gla_tile(src, first_tile, up_ref, gbias_ref, bnw_ref, ob_ref,
                             state_ref, qe_ref, ke_ref, ks_ref, dcol_ref, inc_ref, oacc_ref, tril_ref)
        y_b = _branch_proj(ob_ref, wb_ref)
        yield 600
        yield from _merge_out_tile(src, y_a, y_b, xlag_ref, wo_ref, fw_ref, out_ref)

    both = (1.0, FINISH_COST / PROJECT_COST)

    @pl.when(i == 0)
    def _():
        for g in range(A_KV_HEADS):
            sbias_ref[0, g] = _swa_bias(sink_ref, g, False)
            sbias_ref[1, g] = _swa_bias(sink_ref, g, True)
        up_ref[...] = jnp.zeros_like(up_ref)
        up_ref[0:GATE_RANK, :] = up_f32_ref[...].astype(up_ref.dtype)
        kv_prev_ref[...] = jnp.zeros_like(kv_prev_ref)
        state_ref[...] = jnp.zeros_like(state_ref)
        tril_ref[...] = _chunk_tril()
        weights = _stream_weights(w_in_hbm, wa_hbm, wb_hbm, wo_hbm, w_ref, wa_ref, wb_ref, wo_ref,
                                  stage_ref, wsem_ref)
        proj = project(mix_a)
        next(proj)
        next(proj)
        cols_ready = 0
        for c0 in range(0, PROJ_WIDTH, IN_CHUNK):
            while cols_ready < min(c0 + IN_CHUNK, PROJ_WIDTH):
                _, cols_ready = next(weights)
            next(proj)
        for _ in weights:
            pass
        for _ in proj:
            pass

    @pl.when((i > 0) & (i < n) & (i % 2 == 1))
    def _():
        _emit_interleaved([finish(mix_a), project(mix_b)], both)

    @pl.when((i > 0) & (i < n) & (i % 2 == 0))
    def _():
        _emit_interleaved([finish(mix_b), project(mix_a)], both)

    @pl.when(i == n)
    def _():
        _emit_interleaved([finish(mix_b)], (1.0,))


def _block(x2, positions, norm_w, inv_freq_row, w_in_t, sinks, gate_up, gate_bias, b_norm_w, wa, wb, wo,
           final_w):
    bsz, t = positions.shape
    tiles_per_seq = t // TILE
    m = x2.shape[0]
    n = m // TILE
    assert w_in_t.shape == (IN_WIDTH, D_MODEL) and w_in_t.dtype == jnp.float32
    bf16 = jnp.bfloat16
    cur = lambda i: (jnp.minimum(i, n - 1), 0)
    lag = lambda i: (jnp.maximum(i - 1, 0), 0)
    const = lambda i: (0, 0)
    n_chunks = TILE // CHUNK
    return pl.pallas_call(
        functools.partial(_block_kernel, tiles_per_seq),
        grid=(n + 1,),
        in_specs=[
            pl.BlockSpec(memory_space=pltpu.SMEM),
            pl.BlockSpec((TILE, D_MODEL), cur),
            pl.BlockSpec((TILE, D_MODEL), lag),
            pl.BlockSpec((bsz, TILE), lambda i: (0, jnp.minimum(i, n - 1) % tiles_per_seq)),
            pl.BlockSpec((1, D_MODEL), const),
            pl.BlockSpec((1, LANES), const),
            pl.BlockSpec(memory_space=pl.ANY),
            pl.BlockSpec((GATE_RANK, B_QK_WIDTH), const),
            pl.BlockSpec((1, B_QK_WIDTH), const),
            pl.BlockSpec((1, B_VAL_DIM), const),
            pl.BlockSpec(memory_space=pl.ANY),
            pl.BlockSpec(memory_space=pl.ANY),
            pl.BlockSpec(memory_space=pl.ANY),
            pl.BlockSpec((1, D_MODEL), const),
        ],
        out_specs=pl.BlockSpec((TILE, D_MODEL), lag),
        out_shape=jax.ShapeDtypeStruct((m, D_MODEL), jnp.float32),
        scratch_shapes=[
            pltpu.VMEM((D_MODEL, PROJ_WIDTH), bf16),
            pltpu.VMEM((A_WIDTH, D_MODEL), bf16),
            pltpu.VMEM((B_V_WIDTH, D_MODEL), bf16),
            pltpu.VMEM((D_MODEL, D_MODEL), bf16),
            pltpu.VMEM((W_STAGE_BUFS, W_SLAB, D_MODEL), jnp.float32),
            pltpu.SemaphoreType.DMA((W_STAGE_BUFS,)),
            pltpu.VMEM((LANES, B_QK_WIDTH), bf16),
            pltpu.VMEM((TILE, PROJ_WIDTH), bf16),
            pltpu.VMEM((TILE, PROJ_WIDTH), bf16),
            pltpu.VMEM((TILE, A_WIDTH), bf16),
            pltpu.VMEM((TILE, B_V_WIDTH), bf16),
            pltpu.VMEM((BLOCK, 2 * A_KV_WIDTH), bf16),
            pltpu.VMEM((2, A_KV_HEADS, SWA_ROWS, SWA_COLS), jnp.float32),
            pltpu.VMEM((B_HEADS, B_KEY_DIM, B_VAL_DIM), jnp.float32),
            pltpu.VMEM((TILE, B_QK_WIDTH), bf16),
            pltpu.VMEM((TILE, B_QK_WIDTH), bf16),
            pltpu.VMEM((TILE, B_QK_WIDTH), bf16),
            pltpu.VMEM((B_QK_WIDTH, LANES), jnp.float32),
            pltpu.VMEM((n_chunks, B_HEADS, B_KEY_DIM, B_VAL_DIM), jnp.float32),
            pltpu.VMEM((TILE, B_V_WIDTH), jnp.float32),
            pltpu.VMEM((TILE, TILE), bf16),
        ],
        compiler_params=pltpu.CompilerParams(dimension_semantics=("arbitrary",),
                                             vmem_limit_bytes=VMEM_LIMIT_BYTES),
        name="block",
    )(sinks, x2, x2, positions, norm_w, inv_freq_row, w_in_t, gate_up, gate_bias, b_norm_w, wa, wb, wo,
      final_w)


def kernel(x, positions, norm_w, w_in, a_sinks, b_gate_up, b_gate_bias, b_out_norm_w,
           w_a_proj, w_b_proj, w_out, final_norm_w):
    bsz, t, d = x.shape
    assert d == D_MODEL and t % (2 * TILE) == 0
    depth = norm_w.shape[0]
    assert depth == 1

    half = A_HEAD_DIM // 2
    inv_freq = ROPE_THETA ** (-jnp.arange(half, dtype=jnp.float32) / half)
    inv_freq_row = jnp.tile(inv_freq, LANES // half)[None, :]

    x2 = x.reshape(bsz * t, d)
    for layer in range(depth):
        x2 = _block(x2, positions, norm_w[layer][None, :], inv_freq_row, w_in[layer].T, a_sinks[layer],
                    b_gate_up[layer], b_gate_bias[layer][None, :], b_out_norm_w[layer][None, :],
                    w_a_proj[layer], w_b_proj[layer], w_out[layer], final_norm_w[None, :])
    return x2.reshape(bsz, t, d)
```

```python
import functools

import jax
import jax.numpy as jnp
from jax import lax
from jax.experimental import pallas as pl
from jax.experimental.pallas import tpu as pltpu

D_MODEL = 1024
A_HEADS = 16
A_KV_HEADS = 2
A_HEAD_DIM = 64
A_GROUP = A_HEADS // A_KV_HEADS
A_WIDTH = A_HEADS * A_HEAD_DIM
A_KV_WIDTH = A_KV_HEADS * A_HEAD_DIM
WINDOW = 128
BLOCK = 128
ROPE_THETA = 10000.0

B_HEADS = 4
B_QK_WIDTH = D_MODEL // 2
B_V_WIDTH = D_MODEL
B_KEY_DIM = B_QK_WIDTH // B_HEADS
B_VAL_DIM = B_V_WIDTH // B_HEADS
GATE_RANK = 16
GATE_TAU = 16.0
CHUNK = 64

EPS = 1e-5
NEG_INF = -1e30
LOG2E = 1.4426950408889634

LANES = 128

COL_A_Q = 0
COL_A_GATE = 1024
COL_B_V = 2048
COL_B_GATE = 3072
COL_B_Q = 4096
COL_B_K = 4608
COL_A_K = 5120
COL_A_V = 5248
COL_B_LOW = 5376
MIX_COLS = 5504
COL_M_A = MIX_COLS
COL_M_B = MIX_COLS + D_MODEL
PROJ_WIDTH = MIX_COLS + 2 * D_MODEL

VMEM_LIMIT_BYTES = 60 * 1024 * 1024


IN_WIDTH = 2 * A_WIDTH + 2 * A_KV_WIDTH + 2 * B_QK_WIDTH + 2 * B_V_WIDTH + GATE_RANK + 2 * D_MODEL
_W_IN_PIECES = ((A_WIDTH, COL_A_Q), (A_KV_WIDTH, COL_A_K), (A_KV_WIDTH, COL_A_V), (A_WIDTH, COL_A_GATE),
                (B_QK_WIDTH, COL_B_Q), (B_QK_WIDTH, COL_B_K), (B_V_WIDTH, COL_B_V), (B_V_WIDTH, COL_B_GATE),
                (GATE_RANK, COL_B_LOW), (D_MODEL, COL_M_A), (D_MODEL, COL_M_B))
W_SLAB = 128
W_STAGE_BUFS = 4


def _stream_weights(w_in_hbm, wa_hbm, wb_hbm, wo_hbm, w_ref, wa_ref, wb_ref, wo_ref, stage_ref, sem_ref):
    jobs = []
    src = 0
    for width, dst in _W_IN_PIECES:
        for r in range(0, width, W_SLAB):
            jobs.append((w_in_hbm, src + r, True, w_ref, dst + r, min(W_SLAB, width - r)))
        src += width
    jobs.sort(key=lambda j: j[4])
    for hbm, dst_ref in ((wa_hbm, wa_ref), (wb_hbm, wb_ref), (wo_hbm, wo_ref)):
        for r in range(0, D_MODEL, W_SLAB):
            jobs.append((hbm, r, False, dst_ref, r, W_SLAB))

    def copy(j):
        hbm, row = jobs[j][0], jobs[j][1]
        slot = j % W_STAGE_BUFS
        return pltpu.make_async_copy(hbm.at[pl.ds(row, W_SLAB), :], stage_ref.at[slot], sem_ref.at[slot])

    for j in range(min(W_STAGE_BUFS, len(jobs))):
        copy(j).start()
    for j, (_, _, transpose, dst_ref, off, valid) in enumerate(jobs):
        copy(j).wait()
        slab = stage_ref[j % W_STAGE_BUFS]
        if transpose:
            slab = slab.T
            if valid < W_SLAB:
                lane = lax.broadcasted_iota(jnp.int32, slab.shape, 1)
                slab = jnp.where(lane < valid, slab, 0.0)
            dst_ref[:, off:off + W_SLAB] = slab.astype(dst_ref.dtype)
        else:
            dst_ref[off:off + W_SLAB, :] = slab.astype(dst_ref.dtype)
        if j + W_STAGE_BUFS < len(jobs):
            copy(j + W_STAGE_BUFS).start()
        yield (dst_ref, off + W_SLAB)


TILE = 256
IN_CHUNK = 512


def _silu(x):
    return x * (0.5 * jnp.tanh(0.5 * x) + 0.5)


def _rope_slab(xs, cos, sin_signed, first_half):
    partner = jnp.where(first_half, pltpu.roll(xs, LANES - A_HEAD_DIM // 2, 1),
                        pltpu.roll(xs, A_HEAD_DIM // 2, 1))
    return xs * cos + partner * sin_signed


def _project_tile(x_ref, pos_ref, batch_row, nw_ref, invf_ref, w_ref, mix_ref):
    x = x_ref[...]
    var = jnp.mean(x * x, axis=-1, keepdims=True)
    h = (x * lax.rsqrt(var + EPS) * nw_ref[...]).astype(jnp.bfloat16)
    yield 300

    half = A_HEAD_DIM // 2
    n_grp = LANES // half
    qr = x.shape[0] // n_grp
    lane = lax.broadcasted_iota(jnp.int32, (1, LANES), 1)
    grp = lane // half
    assert x.shape[0] % LANES == 0 and LANES % qr == 0
    prow = pos_ref[pl.ds(batch_row, 1), :].astype(jnp.float32)
    n_cols = x.shape[0] // LANES
    pos_t = jnp.concatenate([prow[:, k * LANES:(k + 1) * LANES] for k in range(n_cols)]
                            + [jnp.zeros((LANES - n_cols, LANES), jnp.float32)], axis=0).T

    def pos_col(j):
        first = j * qr
        return pos_t[first % LANES:first % LANES + qr, first // LANES:first // LANES + 1]

    pos_dense = pos_col(0)
    for j in range(1, n_grp):
        pos_dense = jnp.where(grp == j, pos_col(j), pos_dense)
    ang = pos_dense * invf_ref[...]
    cos_dense = jnp.cos(ang)
    sin_dense = jnp.sin(ang)

    def replicate(dense, j):
        only_j = jnp.where(grp == j, dense, 0.0)
        out = only_j
        for r in range(1, n_grp):
            out = out + pltpu.roll(only_j, r * half, 1)
        return out

    cos = jnp.concatenate([replicate(cos_dense, j) for j in range(n_grp)], axis=0)
    sin = jnp.concatenate([replicate(sin_dense, j) for j in range(n_grp)], axis=0)
    first_half = (lane % A_HEAD_DIM) < half
    sin_signed = jnp.where(first_half, -sin, sin)
    a_q_scale = (A_HEAD_DIM ** -0.5) * LOG2E
    b_q_scale = B_KEY_DIM ** -0.5
    yield 300

    for c0 in range(0, PROJ_WIDTH, IN_CHUNK):
        cw = min(IN_CHUNK, PROJ_WIDTH - c0)
        acc = jnp.dot(h, w_ref[:, c0:c0 + cw], preferred_element_type=jnp.float32)
        for s0 in range(0, cw, LANES):
            col = c0 + s0
            slab = acc[:, s0:s0 + LANES]
            if COL_A_Q <= col < COL_A_Q + A_WIDTH:
                slab = _rope_slab(slab, cos, sin_signed, first_half) * a_q_scale
            elif COL_A_K <= col < COL_A_K + A_KV_WIDTH:
                slab = _rope_slab(slab, cos, sin_signed, first_half)
            elif COL_B_Q <= col < COL_B_Q + B_QK_WIDTH:
                slab = slab * b_q_scale
            mix_ref[:, col:col + LANES] = slab.astype(mix_ref.dtype)
        yield cw


SWA_ROWS = 2 * BLOCK
SWA_COLS = 4 * 2 * BLOCK


def _swa_bias(sink_ref, g, first_block):
    row = lax.broadcasted_iota(jnp.int32, (SWA_ROWS, SWA_COLS), 0)
    col = lax.broadcasted_iota(jnp.int32, (SWA_ROWS, SWA_COLS), 1)
    qi, sub = row % BLOCK, row // BLOCK
    ki, hi = col % (2 * BLOCK), col // (2 * BLOCK)
    rel = qi + BLOCK - ki
    valid = (rel >= 0) & (rel < WINDOW)
    if first_block:
        valid = valid & (ki >= BLOCK)
    sink = jnp.zeros((SWA_ROWS, SWA_COLS), jnp.float32)
    for s in range(2):
        for i in range(4):
            sink = jnp.where((sub == s) & (hi == i), sink_ref[g * A_GROUP + s * 4 + i] * LOG2E, sink)
    return jnp.where(ki == 0, sink, jnp.where(valid, 0.0, NEG_INF))


def _swa_block(mix_ref, r0, k_prev, v_prev, first, bias_ref, oa_ref):
    bf16 = jnp.bfloat16
    rows = slice(r0, r0 + BLOCK)
    krow = lax.broadcasted_iota(jnp.int32, (2 * BLOCK, A_KV_WIDTH), 0)
    lane = lax.broadcasted_iota(jnp.int32, (2 * BLOCK, A_KV_WIDTH), 1)
    k = jnp.concatenate([k_prev, mix_ref[rows, COL_A_K:COL_A_K + A_KV_WIDTH]], axis=0)
    v = jnp.concatenate([v_prev, mix_ref[rows, COL_A_V:COL_A_V + A_KV_WIDTH]], axis=0)
    k = jnp.where(krow == 0, jnp.zeros_like(k), k)
    v = jnp.where(krow == 0, jnp.zeros_like(v), v)
    k_t = k.T
    v_swapped = pltpu.roll(v, A_HEAD_DIM, 1)
    lo = lane < A_HEAD_DIM
    ones_lo = jnp.where(lo, 1.0, 0.0).astype(bf16)
    ones_hi = jnp.where(lo, 0.0, 1.0).astype(bf16)
    zero_kt = jnp.zeros((A_HEAD_DIM, 2 * BLOCK), bf16)
    zero_v = jnp.zeros_like(v)

    for g in range(A_KV_HEADS):
        k_tg = k_t[g * A_HEAD_DIM:(g + 1) * A_HEAD_DIM, :]
        w_k = jnp.concatenate(
            [jnp.concatenate([k_tg if c == i else zero_kt for c in range(4)], axis=1) for i in range(4)],
            axis=0)
        c0 = COL_A_Q + g * A_GROUP * A_HEAD_DIM
        q_g = jnp.concatenate([mix_ref[rows, c0:c0 + 256], mix_ref[rows, c0 + 256:c0 + 512]], axis=0)
        s = jnp.dot(q_g, w_k, preferred_element_type=jnp.float32) + bias_ref[first, g]
        ps = []
        for i in range(4):
            s_i = s[:, i * 2 * BLOCK:(i + 1) * 2 * BLOCK]
            ps.append(jnp.exp2(s_i - jnp.max(s_i, axis=-1, keepdims=True)).astype(bf16))
        yield 550
        v_lo = jnp.where(lo, v if g == 0 else v_swapped, zero_v)
        v_hi = jnp.where(lo, zero_v, v_swapped if g == 0 else v)
        w_v = jnp.concatenate([jnp.concatenate([v_lo, ones_lo], axis=1),
                               jnp.concatenate([v_hi, ones_hi], axis=1)], axis=0)
        p = jnp.concatenate([jnp.concatenate([ps[0], ps[1]], axis=1),
                             jnp.concatenate([ps[2], ps[3]], axis=1)], axis=0)
        r = jnp.dot(p, w_v, preferred_element_type=jnp.float32)
        o_n = r[:, :LANES] / r[:, LANES:]
        for pair in range(2):
            for sub in range(2):
                off = g * A_GROUP * A_HEAD_DIM + sub * 256 + pair * LANES
                piece = o_n[pair * SWA_ROWS + sub * BLOCK:pair * SWA_ROWS + (sub + 1) * BLOCK, :]
                gate = mix_ref[rows, COL_A_GATE + off:COL_A_GATE + off + LANES].astype(jnp.float32)
                oa_ref[rows, off:off + LANES] = (piece * _silu(gate)).astype(oa_ref.dtype)
        yield 400


def _swa_tile(mix_ref, first_tile, kv_prev_ref, bias_ref, oa_ref):
    first = first_tile.astype(jnp.int32)
    k_prev = kv_prev_ref[:, :A_KV_WIDTH]
    v_prev = kv_prev_ref[:, A_KV_WIDTH:]
    for blk in range(TILE // BLOCK):
        r0 = blk * BLOCK
        yield from _swa_block(mix_ref, r0, k_prev, v_prev, first if blk == 0 else 0, bias_ref, oa_ref)
        k_prev = mix_ref[r0:r0 + BLOCK, COL_A_K:COL_A_K + A_KV_WIDTH]
        v_prev = mix_ref[r0:r0 + BLOCK, COL_A_V:COL_A_V + A_KV_WIDTH]
    kv_prev_ref[:, :A_KV_WIDTH] = k_prev
    kv_prev_ref[:, A_KV_WIDTH:] = v_prev


def _chunk_tril():
    ti = lax.broadcasted_iota(jnp.int32, (TILE, TILE), 0)
    tj = lax.broadcasted_iota(jnp.int32, (TILE, TILE), 1)
    return jnp.where((ti // CHUNK == tj // CHUNK) & (ti >= tj), 1.0, 0.0).astype(jnp.bfloat16)


def _chunk_cumsum_rows(x, tril_ref):
    hi = x.astype(jnp.bfloat16)
    lo = (x - hi.astype(jnp.float32)).astype(jnp.bfloat16)
    both = jnp.dot(tril_ref[...], jnp.concatenate([hi, lo], axis=1), preferred_element_type=jnp.float32)
    return both[:, :x.shape[1]] + both[:, x.shape[1]:]


def _gla_tile(mix_ref, first_tile, up_ref, bias_ref, nw_ref, ob_ref,
              state_ref, qe_ref, ke_ref, ks_ref, dcol_ref, inc_ref, oacc_ref, tril_ref):
    bf16 = jnp.bfloat16
    n_chunks = TILE // CHUNK

    gk = jnp.dot(mix_ref[:, COL_B_LOW:COL_B_LOW + LANES], up_ref[...],
                 preferred_element_type=jnp.float32) + bias_ref[...]
    log_a = (jnp.minimum(gk, 0.0) - jnp.log(1.0 + jnp.exp(-jnp.abs(gk)))) * (1.0 / GATE_TAU)
    b = _chunk_cumsum_rows(log_a, tril_ref)
    yield 700
    b_last_rows = [b[(c + 1) * CHUNK - 1:(c + 1) * CHUNK, :] for c in range(n_chunks)]
    b_last = jnp.concatenate([jnp.broadcast_to(r, (CHUNK, B_QK_WIDTH)) for r in b_last_rows], axis=0)
    qf = mix_ref[:, COL_B_Q:COL_B_Q + B_QK_WIDTH].astype(jnp.float32)
    kf = mix_ref[:, COL_B_K:COL_B_K + B_QK_WIDTH].astype(jnp.float32)
    qe_ref[...] = (qf * jnp.exp(b)).astype(bf16)
    ke_ref[...] = (kf * jnp.exp(-b)).astype(bf16)
    ks_ref[...] = (kf * jnp.exp(b_last - b)).astype(bf16)
    pad = jnp.zeros((LANES - n_chunks, B_QK_WIDTH), jnp.float32)
    dcol_ref[...] = jnp.exp(jnp.concatenate(b_last_rows + [pad], axis=0)).T
    yield 700

    ti = lax.broadcasted_iota(jnp.int32, (TILE, TILE), 0)
    tj = lax.broadcasted_iota(jnp.int32, (TILE, TILE), 1)
    keep = (ti // CHUNK == tj // CHUNK) & (ti >= tj)
    for h in range(B_HEADS):
        ks = slice(h * B_KEY_DIM, (h + 1) * B_KEY_DIM)
        vs = slice(COL_B_V + h * B_VAL_DIM, COL_B_V + (h + 1) * B_VAL_DIM)
        os_ = slice(h * B_VAL_DIM, (h + 1) * B_VAL_DIM)
        att = lax.dot_general(qe_ref[:, ks], ke_ref[:, ks], (((1,), (1,)), ((), ())),
                              preferred_element_type=jnp.float32)
        att = jnp.where(keep, att, 0.0).astype(bf16)
        oacc_ref[:, os_] = jnp.dot(att, mix_ref[:, vs], preferred_element_type=jnp.float32)
        for c in range(n_chunks):
            rows = slice(c * CHUNK, (c + 1) * CHUNK)
            inc_ref[c, h] = lax.dot_general(ks_ref[rows, ks], mix_ref[rows, vs], (((0,), (0,)), ((), ())),
                                            preferred_element_type=jnp.float32)
        yield 350

    for h in range(B_HEADS):
        ks = slice(h * B_KEY_DIM, (h + 1) * B_KEY_DIM)
        os_ = slice(h * B_VAL_DIM, (h + 1) * B_VAL_DIM)
        st = jnp.where(first_tile, 0.0, state_ref[h])
        for c in range(n_chunks):
            rows = slice(c * CHUNK, (c + 1) * CHUNK)
            oacc_ref[rows, os_] += jnp.dot(qe_ref[rows, ks], st.astype(bf16),
                                           preferred_element_type=jnp.float32)
            st = dcol_ref[ks, c:c + 1] * st + inc_ref[c, h]
        state_ref[h] = st
        yield 150

    for h in range(B_HEADS):
        os_ = slice(h * B_VAL_DIM, (h + 1) * B_VAL_DIM)
        o_h = oacc_ref[:, os_]
        var = jnp.mean(o_h * o_h, axis=-1, keepdims=True)
        o_n = o_h * lax.rsqrt(var + EPS) * nw_ref[...]
        gate = mix_ref[:, COL_B_GATE + h * B_VAL_DIM:COL_B_GATE + (h + 1) * B_VAL_DIM].astype(jnp.float32)
        ob_ref[:, os_] = (o_n * _silu(gate)).astype(ob_ref.dtype)
        yield 225


def _emit_interleaved(streams, weights):
    spent = [0.0] * len(streams)
    live = list(range(len(streams)))
    while live:
        k = min(live, key=lambda j: spent[j])
        try:
            spent[k] += next(streams[k]) * weights[k]
        except StopIteration:
            live.remove(k)


def _sigmoid(x):
    return 0.5 * jnp.tanh(0.5 * x) + 0.5


def _branch_proj(o_ref, w_ref):
    return jnp.dot(o_ref[...], w_ref[...], preferred_element_type=jnp.float32)


def _merge_out_tile(mix_ref, y_a, y_b, xlag_ref, wo_ref, fw_ref, out_ref):
    m_a = mix_ref[:, COL_M_A:COL_M_A + D_MODEL].astype(jnp.float32)
    m_b = mix_ref[:, COL_M_B:COL_M_B + D_MODEL].astype(jnp.float32)
    merged = (_sigmoid(m_a) * y_a + _sigmoid(m_b) * y_b).astype(jnp.bfloat16)
    yield 500
    z = xlag_ref[...] + jnp.dot(merged, wo_ref[...], preferred_element_type=jnp.float32)
    var = jnp.mean(z * z, axis=-1, keepdims=True)
    out_ref[...] = z * lax.rsqrt(var + EPS) * fw_ref[...]
    yield 1100


PROJECT_COST = 600 + PROJ_WIDTH
FINISH_COST = 3800 + 4100 + 1200 + 1600


def _block_kernel(tiles_per_seq, sink_ref, x_ref, xlag_ref, pos_ref, nw_ref, invf_ref, w_in_hbm, up_f32_ref,
                  gbias_ref, bnw_ref, wa_hbm, wb_hbm, wo_hbm, fw_ref, out_ref,
                  w_ref, wa_ref, wb_ref, wo_ref, stage_ref, wsem_ref, up_ref,
                  mix_a, mix_b, oa_ref, ob_ref, kv_prev_ref, sbias_ref,
                  state_ref, qe_ref, ke_ref, ks_ref, dcol_ref, inc_ref, oacc_ref, tril_ref):
    i = pl.program_id(0)
    n = pl.num_programs(0) - 1

    def project(dst):
        yield from _project_tile(x_ref, pos_ref, i // tiles_per_seq, nw_ref, invf_ref, w_ref, dst)

    def finish(src):
        first_tile = ((i - 1) % tiles_per_seq) == 0
        yield from _swa_tile(src, first_tile, kv_prev_ref, sbias_ref, oa_ref)
        y_a = _branch_proj(oa_ref, wa_ref)
        yield 600
        yield from _gla_tile(src, first_tile, up_ref, gbias_ref, bnw_ref, ob_ref,
                             state_ref, qe_ref, ke_ref, ks_ref, dcol_ref, inc_ref, oacc_ref, tril_ref)
        y_b = _branch_proj(ob_ref, wb_ref)
        yield 600
        yield from _merge_out_tile(src, y_a, y_b, xlag_ref, wo_ref, fw_ref, out_ref)

    both = (1.0, FINISH_COST / PROJECT_COST)

    @pl.when(i == 0)
    def _():
        for g in range(A_KV_HEADS):
            sbias_ref[0, g] = _swa_bias(sink_ref, g, False)
            sbias_ref[1, g] = _swa_bias(sink_ref, g, True)
        up_ref[...] = jnp.zeros_like(up_ref)
        up_ref[0:GATE_RANK, :] = up_f32_ref[...].astype(up_ref.dtype)
        kv_prev_ref[...] = jnp.zeros_like(kv_prev_ref)
        state_ref[...] = jnp.zeros_like(state_ref)
        tril_ref[...] = _chunk_tril()
        weights = _stream_weights(w_in_hbm, wa_hbm, wb_hbm, wo_hbm, w_ref, wa_ref, wb_ref, wo_ref,
                                  stage_ref, wsem_ref)
        for _ in weights:
            pass
        _emit_interleaved([project(mix_a)], (1.0,))

    @pl.when((i > 0) & (i < n) & (i % 2 == 1))
    def _():
        _emit_interleaved([finish(mix_a), project(mix_b)], both)

    @pl.when((i > 0) & (i < n) & (i % 2 == 0))
    def _():
        _emit_interleaved([finish(mix_b), project(mix_a)], both)

    @pl.when(i == n)
    def _():
        _emit_interleaved([finish(mix_b)], (1.0,))


def _block(x2, positions, norm_w, inv_freq_row, w_in_t, sinks, gate_up, gate_bias, b_norm_w, wa, wb, wo,
           final_w):
    bsz, t = positions.shape
    tiles_per_seq = t // TILE
    m = x2.shape[0]
    n = m // TILE
    assert w_in_t.shape == (IN_WIDTH, D_MODEL) and w_in_t.dtype == jnp.float32
    bf16 = jnp.bfloat16
    cur = lambda i: (jnp.minimum(i, n - 1), 0)
    lag = lambda i: (jnp.maximum(i - 1, 0), 0)
    const = lambda i: (0, 0)
    n_chunks = TILE // CHUNK
    return pl.pallas_call(
        functools.partial(_block_kernel, tiles_per_seq),
        grid=(n + 1,),
        in_specs=[
            pl.BlockSpec(memory_space=pltpu.SMEM),
            pl.BlockSpec((TILE, D_MODEL), cur),
            pl.BlockSpec((TILE, D_MODEL), lag),
            pl.BlockSpec((bsz, TILE), lambda i: (0, jnp.minimum(i, n - 1) % tiles_per_seq)),
            pl.BlockSpec((1, D_MODEL), const),
            pl.BlockSpec((1, LANES), const),
            pl.BlockSpec(memory_space=pl.ANY),
            pl.BlockSpec((GATE_RANK, B_QK_WIDTH), const),
            pl.BlockSpec((1, B_QK_WIDTH), const),
            pl.BlockSpec((1, B_VAL_DIM), const),
            pl.BlockSpec(memory_space=pl.ANY),
            pl.BlockSpec(memory_space=pl.ANY),
            pl.BlockSpec(memory_space=pl.ANY),
            pl.BlockSpec((1, D_MODEL), const),
        ],
        out_specs=pl.BlockSpec((TILE, D_MODEL), lag),
        out_shape=jax.ShapeDtypeStruct((m, D_MODEL), jnp.float32),
        scratch_shapes=[
            pltpu.VMEM((D_MODEL, PROJ_WIDTH), bf16),
            pltpu.VMEM((A_WIDTH, D_MODEL), bf16),
            pltpu.VMEM((B_V_WIDTH, D_MODEL), bf16),
            pltpu.VMEM((D_MODEL, D_MODEL), bf16),
            pltpu.VMEM((W_STAGE_BUFS, W_SLAB, D_MODEL), jnp.float32),
            pltpu.SemaphoreType.DMA((W_STAGE_BUFS,)),
            pltpu.VMEM((LANES, B_QK_WIDTH), bf16),
            pltpu.VMEM((TILE, PROJ_WIDTH), bf16),
            pltpu.VMEM((TILE, PROJ_WIDTH), bf16),
            pltpu.VMEM((TILE, A_WIDTH), bf16),
            pltpu.VMEM((TILE, B_V_WIDTH), bf16),
            pltpu.VMEM((BLOCK, 2 * A_KV_WIDTH), bf16),
            pltpu.VMEM((2, A_KV_HEADS, SWA_ROWS, SWA_COLS), jnp.float32),
            pltpu.VMEM((B_HEADS, B_KEY_DIM, B_VAL_DIM), jnp.float32),
            pltpu.VMEM((TILE, B_QK_WIDTH), bf16),
            pltpu.VMEM((TILE, B_QK_WIDTH), bf16),
            pltpu.VMEM((TILE, B_QK_WIDTH), bf16),
            pltpu.VMEM((B_QK_WIDTH, LANES), jnp.float32),
            pltpu.VMEM((n_chunks, B_HEADS, B_KEY_DIM, B_VAL_DIM), jnp.float32),
            pltpu.VMEM((TILE, B_V_WIDTH), jnp.float32),
            pltpu.VMEM((TILE, TILE), bf16),
        ],
        compiler_params=pltpu.CompilerParams(dimension_semantics=("arbitrary",),
                                             vmem_limit_bytes=VMEM_LIMIT_BYTES),
        name="block",
    )(sinks, x2, x2, positions, norm_w, inv_freq_row, w_in_t, gate_up, gate_bias, b_norm_w, wa, wb, wo,
      final_w)


def kernel(x, positions, norm_w, w_in, a_sinks, b_gate_up, b_gate_bias, b_out_norm_w,
           w_a_proj, w_b_proj, w_out, final_norm_w):
    bsz, t, d = x.shape
    assert d == D_MODEL and t % (2 * TILE) == 0
    depth = norm_w.shape[0]
    assert depth == 1

    half = A_HEAD_DIM // 2
    inv_freq = ROPE_THETA ** (-jnp.arange(half, dtype=jnp.float32) / half)
    inv_freq_row = jnp.tile(inv_freq, LANES // half)[None, :]

    x2 = x.reshape(bsz * t, d)
    for layer in range(depth):
        x2 = _block(x2, positions, norm_w[layer][None, :], inv_freq_row, w_in[layer].T, a_sinks[layer],
                    b_gate_up[layer], b_gate_bias[layer][None, :], b_out_norm_w[layer][None, :],
                    w_a_proj[layer], w_b_proj[layer], w_out[layer], final_norm_w[None, :])
    return x2.reshape(bsz, t, d)
```

```python
import functools

import jax
import jax.numpy as jnp
from jax import lax
from jax.experimental import pallas as pl
from jax.experimental.pallas import tpu as pltpu

D_MODEL = 1024
A_HEADS = 16
A_KV_HEADS = 2
A_HEAD_DIM = 64
A_GROUP = A_HEADS // A_KV_HEADS
A_WIDTH = A_HEADS * A_HEAD_DIM
A_KV_WIDTH = A_KV_HEADS * A_HEAD_DIM
WINDOW = 128
BLOCK = 128
ROPE_THETA = 10000.0

B_HEADS = 4
B_QK_WIDTH = D_MODEL // 2
B_V_WIDTH = D_MODEL
B_KEY_DIM = B_QK_WIDTH // B_HEADS
B_VAL_DIM = B_V_WIDTH // B_HEADS
GATE_RANK = 16
GATE_TAU = 16.0
CHUNK = 64

EPS = 1e-5
NEG_INF = -1e30
LOG2E = 1.4426950408889634
LN2 = 0.6931471805599453

LANES = 128

COL_A_Q = 0
COL_A_GATE = 1024
COL_B_V = 2048
COL_B_GATE = 3072
COL_B_Q = 4096
COL_B_K = 4608
COL_A_K = 5120
COL_A_V = 5248
COL_B_LOW = 5376
MIX_COLS = 5504
COL_M_A = MIX_COLS
COL_M_B = MIX_COLS + D_MODEL
PROJ_WIDTH = MIX_COLS + 2 * D_MODEL

VMEM_LIMIT_BYTES = 60 * 1024 * 1024


IN_WIDTH = 2 * A_WIDTH + 2 * A_KV_WIDTH + 2 * B_QK_WIDTH + 2 * B_V_WIDTH + GATE_RANK + 2 * D_MODEL
_W_IN_PIECES = ((A_WIDTH, COL_A_Q), (A_KV_WIDTH, COL_A_K), (A_KV_WIDTH, COL_A_V), (A_WIDTH, COL_A_GATE),
                (B_QK_WIDTH, COL_B_Q), (B_QK_WIDTH, COL_B_K), (B_V_WIDTH, COL_B_V), (B_V_WIDTH, COL_B_GATE),
                (GATE_RANK, COL_B_LOW), (D_MODEL, COL_M_A), (D_MODEL, COL_M_B))
W_SLAB = 128
W_STAGE_BUFS = 4


def _stream_weights(w_in_hbm, wa_hbm, wb_hbm, wo_hbm, w_ref, wa_ref, wb_ref, wo_ref, stage_ref, sem_ref):
    jobs = []
    src = 0
    for width, dst in _W_IN_PIECES:
        for r in range(0, width, W_SLAB):
            jobs.append((w_in_hbm, src + r, True, w_ref, dst + r, min(W_SLAB, width - r)))
        src += width
    jobs.sort(key=lambda j: j[4])
    for hbm, dst_ref in ((wa_hbm, wa_ref), (wb_hbm, wb_ref), (wo_hbm, wo_ref)):
        for r in range(0, D_MODEL, W_SLAB):
            jobs.append((hbm, r, False, dst_ref, r, W_SLAB))

    def copy(j):
        hbm, row = jobs[j][0], jobs[j][1]
        slot = j % W_STAGE_BUFS
        return pltpu.make_async_copy(hbm.at[pl.ds(row, W_SLAB), :], stage_ref.at[slot], sem_ref.at[slot])

    for j in range(min(W_STAGE_BUFS, len(jobs))):
        copy(j).start()
    for j, (_, _, transpose, dst_ref, off, valid) in enumerate(jobs):
        copy(j).wait()
        slab = stage_ref[j % W_STAGE_BUFS]
        halved_cols = any(c <= off < c + D_MODEL for c in (COL_A_GATE, COL_B_GATE, COL_M_A, COL_M_B))
        if (transpose and halved_cols) or dst_ref is wo_ref:
            slab = slab * 0.5
        if transpose:
            slab = slab.T
            if valid < W_SLAB:
                lane = lax.broadcasted_iota(jnp.int32, slab.shape, 1)
                slab = jnp.where(lane < valid, slab, 0.0)
            dst_ref[:, off:off + W_SLAB] = slab.astype(dst_ref.dtype)
        else:
            dst_ref[off:off + W_SLAB, :] = slab.astype(dst_ref.dtype)
        if j + W_STAGE_BUFS < len(jobs):
            copy(j + W_STAGE_BUFS).start()
        yield (dst_ref, off + W_SLAB)


TILE = 256
IN_CHUNK = 512


def _silu_of_half(hx):
    return hx * jnp.tanh(hx) + hx


def _rope_slab(xs, cos, sin_signed, first_half):
    partner = jnp.where(first_half, pltpu.roll(xs, LANES - A_HEAD_DIM // 2, 1),
                        pltpu.roll(xs, A_HEAD_DIM // 2, 1))
    return xs * cos + partner * sin_signed


def _project_tile(x_ref, pos_ref, batch_row, nw_ref, invf_ref, w_ref, mix_ref):
    x = x_ref[...]
    var = jnp.mean(x * x, axis=-1, keepdims=True)
    h = (x * lax.rsqrt(var + EPS) * nw_ref[...]).astype(jnp.bfloat16)
    yield 300

    half = A_HEAD_DIM // 2
    n_grp = LANES // half
    qr = x.shape[0] // n_grp
    lane = lax.broadcasted_iota(jnp.int32, (1, LANES), 1)
    grp = lane // half
    assert x.shape[0] % LANES == 0 and LANES % qr == 0
    prow = pos_ref[pl.ds(batch_row, 1), :].astype(jnp.float32)
    n_cols = x.shape[0] // LANES
    pos_t = jnp.concatenate([prow[:, k * LANES:(k + 1) * LANES] for k in range(n_cols)]
                            + [jnp.zeros((LANES - n_cols, LANES), jnp.float32)], axis=0).T

    def pos_col(j):
        first = j * qr
        return pos_t[first % LANES:first % LANES + qr, first // LANES:first // LANES + 1]

    pos_dense = pos_col(0)
    for j in range(1, n_grp):
        pos_dense = jnp.where(grp == j, pos_col(j), pos_dense)
    ang = pos_dense * invf_ref[...]
    cos_dense = jnp.cos(ang)
    sin_dense = jnp.sin(ang)

    def replicate(dense, j):
        only_j = jnp.where(grp == j, dense, 0.0)
        out = only_j
        for r in range(1, n_grp):
            out = out + pltpu.roll(only_j, r * half, 1)
        return out

    cos = jnp.concatenate([replicate(cos_dense, j) for j in range(n_grp)], axis=0)
    sin = jnp.concatenate([replicate(sin_dense, j) for j in range(n_grp)], axis=0)
    first_half = (lane % A_HEAD_DIM) < half
    sin_signed = jnp.where(first_half, -sin, sin)
    a_q_scale = (A_HEAD_DIM ** -0.5) * LOG2E
    b_q_scale = B_KEY_DIM ** -0.5
    yield 300

    for c0 in range(0, PROJ_WIDTH, IN_CHUNK):
        cw = min(IN_CHUNK, PROJ_WIDTH - c0)
        acc = jnp.dot(h, w_ref[:, c0:c0 + cw], preferred_element_type=jnp.float32)
        for s0 in range(0, cw, LANES):
            col = c0 + s0
            slab = acc[:, s0:s0 + LANES]
            if COL_A_Q <= col < COL_A_Q + A_WIDTH:
                slab = _rope_slab(slab, cos, sin_signed, first_half) * a_q_scale
            elif COL_A_K <= col < COL_A_K + A_KV_WIDTH:
                slab = _rope_slab(slab, cos, sin_signed, first_half)
            elif COL_B_Q <= col < COL_B_Q + B_QK_WIDTH:
                slab = slab * b_q_scale
            mix_ref[:, col:col + LANES] = slab.astype(mix_ref.dtype)
        yield cw


SWA_ROWS = 2 * BLOCK
SWA_COLS = 4 * 2 * BLOCK


def _swa_bias(sink_ref, g, first_block):
    row = lax.broadcasted_iota(jnp.int32, (SWA_ROWS, SWA_COLS), 0)
    col = lax.broadcasted_iota(jnp.int32, (SWA_ROWS, SWA_COLS), 1)
    qi, sub = row % BLOCK, row // BLOCK
    ki, hi = col % (2 * BLOCK), col // (2 * BLOCK)
    rel = qi + BLOCK - ki
    valid = (rel >= 0) & (rel < WINDOW)
    if first_block:
        valid = valid & (ki >= BLOCK)
    sink = jnp.zeros((SWA_ROWS, SWA_COLS), jnp.float32)
    for s in range(2):
        for i in range(4):
            sink = jnp.where((sub == s) & (hi == i), sink_ref[g * A_GROUP + s * 4 + i] * LOG2E, sink)
    return jnp.where(ki == 0, sink, jnp.where(valid, 0.0, NEG_INF))


def _swa_block(mix_ref, r0, k_prev, v_prev, first, bias_ref, oa_ref):
    bf16 = jnp.bfloat16
    rows = slice(r0, r0 + BLOCK)
    krow = lax.broadcasted_iota(jnp.int32, (2 * BLOCK, A_KV_WIDTH), 0)
    lane = lax.broadcasted_iota(jnp.int32, (2 * BLOCK, A_KV_WIDTH), 1)
    k = jnp.concatenate([k_prev, mix_ref[rows, COL_A_K:COL_A_K + A_KV_WIDTH]], axis=0)
    v = jnp.concatenate([v_prev, mix_ref[rows, COL_A_V:COL_A_V + A_KV_WIDTH]], axis=0)
    k = jnp.where(krow == 0, jnp.zeros_like(k), k)
    v = jnp.where(krow == 0, jnp.zeros_like(v), v)
    k_t = k.T
    v_swapped = pltpu.roll(v, A_HEAD_DIM, 1)
    lo = lane < A_HEAD_DIM
    ones_lo = jnp.where(lo, 1.0, 0.0).astype(bf16)
    ones_hi = jnp.where(lo, 0.0, 1.0).astype(bf16)
    zero_kt = jnp.zeros((A_HEAD_DIM, 2 * BLOCK), bf16)
    zero_v = jnp.zeros_like(v)

    for g in range(A_KV_HEADS):
        k_tg = k_t[g * A_HEAD_DIM:(g + 1) * A_HEAD_DIM, :]
        w_k = jnp.concatenate(
            [jnp.concatenate([k_tg if c == i else zero_kt for c in range(4)], axis=1) for i in range(4)],
            axis=0)
        c0 = COL_A_Q + g * A_GROUP * A_HEAD_DIM
        q_g = jnp.concatenate([mix_ref[rows, c0:c0 + 256], mix_ref[rows, c0 + 256:c0 + 512]], axis=0)
        s = jnp.dot(q_g, w_k, preferred_element_type=jnp.float32) + bias_ref[first, g]
        ps = []
        for i in range(4):
            s_i = s[:, i * 2 * BLOCK:(i + 1) * 2 * BLOCK]
            ps.append(jnp.exp2(s_i - jnp.max(s_i, axis=-1, keepdims=True)).astype(bf16))
        yield 550
        v_lo = jnp.where(lo, v if g == 0 else v_swapped, zero_v)
        v_hi = jnp.where(lo, zero_v, v_swapped if g == 0 else v)
        w_v = jnp.concatenate([jnp.concatenate([v_lo, ones_lo], axis=1),
                               jnp.concatenate([v_hi, ones_hi], axis=1)], axis=0)
        p = jnp.concatenate([jnp.concatenate([ps[0], ps[1]], axis=1),
                             jnp.concatenate([ps[2], ps[3]], axis=1)], axis=0)
        r = jnp.dot(p, w_v, preferred_element_type=jnp.float32)
        o_n = r[:, :LANES] / r[:, LANES:]
        for pair in range(2):
            for sub in range(2):
                off = g * A_GROUP * A_HEAD_DIM + sub * 256 + pair * LANES
                piece = o_n[pair * SWA_ROWS + sub * BLOCK:pair * SWA_ROWS + (sub + 1) * BLOCK, :]
                gate = mix_ref[rows, COL_A_GATE + off:COL_A_GATE + off + LANES].astype(jnp.float32)
                oa_ref[rows, off:off + LANES] = (piece * _silu_of_half(gate)).astype(oa_ref.dtype)
        yield 400


def _swa_tile(mix_ref, first_tile, kv_prev_ref, bias_ref, oa_ref):
    first = first_tile.astype(jnp.int32)
    k_prev = kv_prev_ref[:, :A_KV_WIDTH]
    v_prev = kv_prev_ref[:, A_KV_WIDTH:]
    for blk in range(TILE // BLOCK):
        r0 = blk * BLOCK
        yield from _swa_block(mix_ref, r0, k_prev, v_prev, first if blk == 0 else 0, bias_ref, oa_ref)
        k_prev = mix_ref[r0:r0 + BLOCK, COL_A_K:COL_A_K + A_KV_WIDTH]
        v_prev = mix_ref[r0:r0 + BLOCK, COL_A_V:COL_A_V + A_KV_WIDTH]
    kv_prev_ref[:, :A_KV_WIDTH] = k_prev
    kv_prev_ref[:, A_KV_WIDTH:] = v_prev


def _chunk_tril():
    ti = lax.broadcasted_iota(jnp.int32, (TILE, TILE), 0)
    tj = lax.broadcasted_iota(jnp.int32, (TILE, TILE), 1)
    return jnp.where((ti // CHUNK == tj // CHUNK) & (ti >= tj), 1.0, 0.0).astype(jnp.bfloat16)


def _chunk_cumsum_rows(x, tril_ref):
    hi = x.astype(jnp.bfloat16)
    lo = (x - hi.astype(jnp.float32)).astype(jnp.bfloat16)
    both = jnp.dot(tril_ref[...], jnp.concatenate([hi, lo], axis=1), preferred_element_type=jnp.float32)
    return both[:, :x.shape[1]] + both[:, x.shape[1]:]


def _gla_tile(mix_ref, first_tile, up_ref, bias_ref, nw_ref, ob_ref,
              state_ref, qe_ref, ke_ref, ks_ref, dcol_ref, inc_ref, oacc_ref, tril_ref):
    bf16 = jnp.bfloat16
    n_chunks = TILE // CHUNK

    gk = jnp.dot(mix_ref[:, COL_B_LOW:COL_B_LOW + LANES], up_ref[...],
                 preferred_element_type=jnp.float32) + bias_ref[...]
    soft = jnp.log2(1.0 + jnp.exp2(jnp.minimum(gk, -gk) * LOG2E))
    log_a = jnp.minimum(gk, 0.0) * (1.0 / GATE_TAU) - soft * (LN2 / GATE_TAU)
    b = _chunk_cumsum_rows(log_a, tril_ref)
    yield 700
    b_last_rows = [b[(c + 1) * CHUNK - 1:(c + 1) * CHUNK, :] for c in range(n_chunks)]
    b_last = jnp.concatenate([jnp.broadcast_to(r, (CHUNK, B_QK_WIDTH)) for r in b_last_rows], axis=0)
    qf = mix_ref[:, COL_B_Q:COL_B_Q + B_QK_WIDTH].astype(jnp.float32)
    kf = mix_ref[:, COL_B_K:COL_B_K + B_QK_WIDTH].astype(jnp.float32)
    qe_ref[...] = (qf * jnp.exp(b)).astype(bf16)
    ke_ref[...] = (kf * jnp.exp(-b)).astype(bf16)
    ks_ref[...] = (kf * jnp.exp(b_last - b)).astype(bf16)
    pad = jnp.zeros((LANES - n_chunks, B_QK_WIDTH), jnp.float32)
    dcol_ref[...] = jnp.exp(jnp.concatenate(b_last_rows + [pad], axis=0)).T
    yield 700

    ti = lax.broadcasted_iota(jnp.int32, (TILE, TILE), 0)
    tj = lax.broadcasted_iota(jnp.int32, (TILE, TILE), 1)
    keep = (ti // CHUNK == tj // CHUNK) & (ti >= tj)
    for h in range(B_HEADS):
        ks = slice(h * B_KEY_DIM, (h + 1) * B_KEY_DIM)
        vs = slice(COL_B_V + h * B_VAL_DIM, COL_B_V + (h + 1) * B_VAL_DIM)
        os_ = slice(h * B_VAL_DIM, (h + 1) * B_VAL_DIM)
        att = lax.dot_general(qe_ref[:, ks], ke_ref[:, ks], (((1,), (1,)), ((), ())),
                              preferred_element_type=jnp.float32)
        att = jnp.where(keep, att, 0.0).astype(bf16)
        oacc_ref[:, os_] = jnp.dot(att, mix_ref[:, vs], preferred_element_type=jnp.float32)
        for c in range(n_chunks):
            rows = slice(c * CHUNK, (c + 1) * CHUNK)
            inc_ref[c, h] = lax.dot_general(ks_ref[rows, ks], mix_ref[rows, vs], (((0,), (0,)), ((), ())),
                                            preferred_element_type=jnp.float32)
        yield 350

    for h in range(B_HEADS):
        ks = slice(h * B_KEY_DIM, (h + 1) * B_KEY_DIM)
        os_ = slice(h * B_VAL_DIM, (h + 1) * B_VAL_DIM)
        st = jnp.where(first_tile, 0.0, state_ref[h])
        for c in range(n_chunks):
            rows = slice(c * CHUNK, (c + 1) * CHUNK)
            oacc_ref[rows, os_] += jnp.dot(qe_ref[rows, ks], st.astype(bf16),
                                           preferred_element_type=jnp.float32)
            st = dcol_ref[ks, c:c + 1] * st + inc_ref[c, h]
        state_ref[h] = st
        yield 150

    for h in range(B_HEADS):
        os_ = slice(h * B_VAL_DIM, (h + 1) * B_VAL_DIM)
        o_h = oacc_ref[:, os_]
        var = jnp.mean(o_h * o_h, axis=-1, keepdims=True)
        o_n = o_h * lax.rsqrt(var + EPS) * nw_ref[...]
        gate = mix_ref[:, COL_B_GATE + h * B_VAL_DIM:COL_B_GATE + (h + 1) * B_VAL_DIM].astype(jnp.float32)
        ob_ref[:, os_] = (o_n * _silu_of_half(gate)).astype(ob_ref.dtype)
        yield 225


def _emit_interleaved(streams, weights):
    spent = [0.0] * len(streams)
    live = list(range(len(streams)))
    while live:
        k = min(live, key=lambda j: spent[j])
        try:
            spent[k] += next(streams[k]) * weights[k]
        except StopIteration:
            live.remove(k)


def _branch_proj(o_ref, w_ref):
    return jnp.dot(o_ref[...], w_ref[...], preferred_element_type=jnp.float32)


def _merge_out_tile(mix_ref, y_a, y_b, xlag_ref, wo_ref, fw_ref, out_ref):
    hm_a = mix_ref[:, COL_M_A:COL_M_A + D_MODEL].astype(jnp.float32)
    hm_b = mix_ref[:, COL_M_B:COL_M_B + D_MODEL].astype(jnp.float32)
    merged = ((jnp.tanh(hm_a) + 1.0) * y_a + (jnp.tanh(hm_b) + 1.0) * y_b).astype(jnp.bfloat16)
    yield 500
    z = xlag_ref[...] + jnp.dot(merged, wo_ref[...], preferred_element_type=jnp.float32)
    var = jnp.mean(z * z, axis=-1, keepdims=True)
    out_ref[...] = z * lax.rsqrt(var + EPS) * fw_ref[...]
    yield 1100


PROJECT_COST = 600 + PROJ_WIDTH
FINISH_COST = 3800 + 4100 + 1200 + 1600


def _block_kernel(tiles_per_seq, sink_ref, x_ref, xlag_ref, pos_ref, nw_ref, invf_ref, w_in_hbm, up_f32_ref,
                  gbias_ref, bnw_ref, wa_hbm, wb_hbm, wo_hbm, fw_ref, out_ref,
                  w_ref, wa_ref, wb_ref, wo_ref, stage_ref, wsem_ref, up_ref,
                  mix_a, mix_b, oa_ref, ob_ref, kv_prev_ref, sbias_ref,
                  state_ref, qe_ref, ke_ref, ks_ref, dcol_ref, inc_ref, oacc_ref, tril_ref):
    i = pl.program_id(0)
    n = pl.num_programs(0) - 1

    def project(dst):
        yield from _project_tile(x_ref, pos_ref, i // tiles_per_seq, nw_ref, invf_ref, w_ref, dst)

    def finish(src):
        first_tile = ((i - 1) % tiles_per_seq) == 0
        yield from _swa_tile(src, first_tile, kv_prev_ref, sbias_ref, oa_ref)
        y_a = _branch_proj(oa_ref, wa_ref)
        yield 600
        yield from _gla_tile(src, first_tile, up_ref, gbias_ref, bnw_ref, ob_ref,
                             state_ref, qe_ref, ke_ref, ks_ref, dcol_ref, inc_ref, oacc_ref, tril_ref)
        y_b = _branch_proj(ob_ref, wb_ref)
        yield 600
        yield from _merge_out_tile(src, y_a, y_b, xlag_ref, wo_ref, fw_ref, out_ref)

    both = (1.0, FINISH_COST / PROJECT_COST)

    @pl.when(i == 0)
    def _():
        for g in range(A_KV_HEADS):
            sbias_ref[0, g] = _swa_bias(sink_ref, g, False)
            sbias_ref[1, g] = _swa_bias(sink_ref, g, True)
        up_ref[...] = jnp.zeros_like(up_ref)
        up_ref[0:GATE_RANK, :] = up_f32_ref[...].astype(up_ref.dtype)
        kv_prev_ref[...] = jnp.zeros_like(kv_prev_ref)
        state_ref[...] = jnp.zeros_like(state_ref)
        tril_ref[...] = _chunk_tril()
        weights = _stream_weights(w_in_hbm, wa_hbm, wb_hbm, wo_hbm, w_ref, wa_ref, wb_ref, wo_ref,
                                  stage_ref, wsem_ref)
        for _ in weights:
            pass
        _emit_interleaved([project(mix_a)], (1.0,))

    @pl.when((i > 0) & (i < n) & (i % 2 == 1))
    def _():
        _emit_interleaved([finish(mix_a), project(mix_b)], both)

    @pl.when((i > 0) & (i < n) & (i % 2 == 0))
    def _():
        _emit_interleaved([finish(mix_b), project(mix_a)], both)

    @pl.when(i == n)
    def _():
        _emit_interleaved([finish(mix_b)], (1.0,))


def _block(x2, positions, norm_w, inv_freq_row, w_in_t, sinks, gate_up, gate_bias, b_norm_w, wa, wb, wo,
           final_w):
    bsz, t = positions.shape
    tiles_per_seq = t // TILE
    m = x2.shape[0]
    n = m // TILE
    assert w_in_t.shape == (IN_WIDTH, D_MODEL) and w_in_t.dtype == jnp.float32
    bf16 = jnp.bfloat16
    cur = lambda i: (jnp.minimum(i, n - 1), 0)
    lag = lambda i: (jnp.maximum(i - 1, 0), 0)
    const = lambda i: (0, 0)
    n_chunks = TILE // CHUNK
    return pl.pallas_call(
        functools.partial(_block_kernel, tiles_per_seq),
        grid=(n + 1,),
        in_specs=[
            pl.BlockSpec(memory_space=pltpu.SMEM),
            pl.BlockSpec((TILE, D_MODEL), cur),
            pl.BlockSpec((TILE, D_MODEL), lag),
            pl.BlockSpec((bsz, TILE), lambda i: (0, jnp.minimum(i, n - 1) % tiles_per_seq)),
            pl.BlockSpec((1, D_MODEL), const),
            pl.BlockSpec((1, LANES), const),
            pl.BlockSpec(memory_space=pl.ANY),
            pl.BlockSpec((GATE_RANK, B_QK_WIDTH), const),
            pl.BlockSpec((1, B_QK_WIDTH), const),
            pl.BlockSpec((1, B_VAL_DIM), const),
            pl.BlockSpec(memory_space=pl.ANY),
            pl.BlockSpec(memory_space=pl.ANY),
            pl.BlockSpec(memory_space=pl.ANY),
            pl.BlockSpec((1, D_MODEL), const),
        ],
        out_specs=pl.BlockSpec((TILE, D_MODEL), lag),
        out_shape=jax.ShapeDtypeStruct((m, D_MODEL), jnp.float32),
        scratch_shapes=[
            pltpu.VMEM((D_MODEL, PROJ_WIDTH), bf16),
            pltpu.VMEM((A_WIDTH, D_MODEL), bf16),
            pltpu.VMEM((B_V_WIDTH, D_MODEL), bf16),
            pltpu.VMEM((D_MODEL, D_MODEL), bf16),
            pltpu.VMEM((W_STAGE_BUFS, W_SLAB, D_MODEL), jnp.float32),
            pltpu.SemaphoreType.DMA((W_STAGE_BUFS,)),
            pltpu.VMEM((LANES, B_QK_WIDTH), bf16),
            pltpu.VMEM((TILE, PROJ_WIDTH), bf16),
            pltpu.VMEM((TILE, PROJ_WIDTH), bf16),
            pltpu.VMEM((TILE, A_WIDTH), bf16),
            pltpu.VMEM((TILE, B_V_WIDTH), bf16),
            pltpu.VMEM((BLOCK, 2 * A_KV_WIDTH), bf16),
            pltpu.VMEM((2, A_KV_HEADS, SWA_ROWS, SWA_COLS), jnp.float32),
            pltpu.VMEM((B_HEADS, B_KEY_DIM, B_VAL_DIM), jnp.float32),
            pltpu.VMEM((TILE, B_QK_WIDTH), bf16),
            pltpu.VMEM((TILE, B_QK_WIDTH), bf16),
            pltpu.VMEM((TILE, B_QK_WIDTH), bf16),
            pltpu.VMEM((B_QK_WIDTH, LANES), jnp.float32),
            pltpu.VMEM((n_chunks, B_HEADS, B_KEY_DIM, B_VAL_DIM), jnp.float32),
            pltpu.VMEM((TILE, B_V_WIDTH), jnp.float32),
            pltpu.VMEM((TILE, TILE), bf16),
        ],
        compiler_params=pltpu.CompilerParams(dimension_semantics=("arbitrary",),
                                             vmem_limit_bytes=VMEM_LIMIT_BYTES),
        name="block",
    )(sinks, x2, x2, positions, norm_w, inv_freq_row, w_in_t, gate_up, gate_bias, b_norm_w, wa, wb, wo,
      final_w)


def kernel(x, positions, norm_w, w_in, a_sinks, b_gate_up, b_gate_bias, b_out_norm_w,
           w_a_proj, w_b_proj, w_out, final_norm_w):
    bsz, t, d = x.shape
    assert d == D_MODEL and t % (2 * TILE) == 0
    depth = norm_w.shape[0]
    assert depth == 1

    half = A_HEAD_DIM // 2
    inv_freq = ROPE_THETA ** (-jnp.arange(half, dtype=jnp.float32) / half)
    inv_freq_row = jnp.tile(inv_freq, LANES // half)[None, :]

    x2 = x.reshape(bsz * t, d)
    for layer in range(depth):
        x2 = _block(x2, positions, norm_w[layer][None, :], inv_freq_row, w_in[layer].T, a_sinks[layer],
                    b_gate_up[layer], b_gate_bias[layer][None, :], b_out_norm_w[layer][None, :],
                    w_a_proj[layer], w_b_proj[layer], w_out[layer], final_norm_w[None, :])
    return x2.reshape(bsz, t, d)
```

```python
import functools

import jax
import jax.numpy as jnp
from jax import lax
from jax.experimental import pallas as pl
from jax.experimental.pallas import tpu as pltpu

D_MODEL = 1024
A_HEADS = 16
A_KV_HEADS = 2
A_HEAD_DIM = 64
A_GROUP = A_HEADS // A_KV_HEADS
A_WIDTH = A_HEADS * A_HEAD_DIM
A_KV_WIDTH = A_KV_HEADS * A_HEAD_DIM
WINDOW = 128
BLOCK = 128
ROPE_THETA = 10000.0

B_HEADS = 4
B_QK_WIDTH = D_MODEL // 2
B_V_WIDTH = D_MODEL
B_KEY_DIM = B_QK_WIDTH // B_HEADS
B_VAL_DIM = B_V_WIDTH // B_HEADS
GATE_RANK = 16
GATE_TAU = 16.0
CHUNK = 64

EPS = 1e-5
NEG_INF = -1e30
LOG2E = 1.4426950408889634
LN2 = 0.6931471805599453

LANES = 128

COL_A_Q = 0
COL_A_GATE = 1024
COL_B_V = 2048
COL_B_GATE = 3072
COL_B_Q = 4096
COL_B_K = 4608
COL_A_K = 5120
COL_A_V = 5248
COL_B_LOW = 5376
MIX_COLS = 5504
COL_M_A = MIX_COLS
COL_M_B = MIX_COLS + D_MODEL
PROJ_WIDTH = MIX_COLS + 2 * D_MODEL

VMEM_LIMIT_BYTES = 60 * 1024 * 1024


IN_WIDTH = 2 * A_WIDTH + 2 * A_KV_WIDTH + 2 * B_QK_WIDTH + 2 * B_V_WIDTH + GATE_RANK + 2 * D_MODEL
_W_IN_PIECES = ((A_WIDTH, COL_A_Q), (A_KV_WIDTH, COL_A_K), (A_KV_WIDTH, COL_A_V), (A_WIDTH, COL_A_GATE),
                (B_QK_WIDTH, COL_B_Q), (B_QK_WIDTH, COL_B_K), (B_V_WIDTH, COL_B_V), (B_V_WIDTH, COL_B_GATE),
                (GATE_RANK, COL_B_LOW), (D_MODEL, COL_M_A), (D_MODEL, COL_M_B))
W_SLAB = 128
W_STAGE_BUFS = 12


def _stream_weights(w_in_hbm, wa_hbm, wb_hbm, wo_hbm, w_ref, wa_ref, wb_ref, wo_ref, stage_ref, sem_ref):
    jobs = []
    src = 0
    for width, dst in _W_IN_PIECES:
        for r in range(0, width, W_SLAB):
            jobs.append((w_in_hbm, src + r, True, w_ref, dst + r, min(W_SLAB, width - r)))
        src += width
    jobs.sort(key=lambda j: j[4])
    for hbm, dst_ref in ((wa_hbm, wa_ref), (wb_hbm, wb_ref), (wo_hbm, wo_ref)):
        for r in range(0, D_MODEL, W_SLAB):
            jobs.append((hbm, r, False, dst_ref, r, W_SLAB))

    def copy(j):
        hbm, row = jobs[j][0], jobs[j][1]
        slot = j % W_STAGE_BUFS
        return pltpu.make_async_copy(hbm.at[pl.ds(row, W_SLAB), :], stage_ref.at[slot], sem_ref.at[slot])

    for j in range(min(W_STAGE_BUFS, len(jobs))):
        copy(j).start()
    for j, (_, _, transpose, dst_ref, off, valid) in enumerate(jobs):
        copy(j).wait()
        slab = stage_ref[j % W_STAGE_BUFS]
        halved_cols = any(c <= off < c + D_MODEL for c in (COL_A_GATE, COL_B_GATE, COL_M_A, COL_M_B))
        if (transpose and halved_cols) or dst_ref is wo_ref:
            slab = slab * 0.5
        if transpose:
            slab = slab.T
            if valid < W_SLAB:
                lane = lax.broadcasted_iota(jnp.int32, slab.shape, 1)
                slab = jnp.where(lane < valid, slab, 0.0)
            dst_ref[:, off:off + W_SLAB] = slab.astype(dst_ref.dtype)
        else:
            dst_ref[off:off + W_SLAB, :] = slab.astype(dst_ref.dtype)
        if j + W_STAGE_BUFS < len(jobs):
            copy(j + W_STAGE_BUFS).start()
        yield (dst_ref, off + W_SLAB)


TILE = 256
IN_CHUNK = 512


def _silu_of_half(hx):
    return hx * jnp.tanh(hx) + hx


def _rope_slab(xs, cos, sin_signed, first_half):
    partner = jnp.where(first_half, pltpu.roll(xs, LANES - A_HEAD_DIM // 2, 1),
                        pltpu.roll(xs, A_HEAD_DIM // 2, 1))
    return xs * cos + partner * sin_signed


def _project_tile(x_ref, pos_ref, batch_row, nw_ref, invf_ref, w_ref, mix_ref):
    x = x_ref[...]
    var = jnp.mean(x * x, axis=-1, keepdims=True)
    h = (x * lax.rsqrt(var + EPS) * nw_ref[...]).astype(jnp.bfloat16)
    yield 300

    half = A_HEAD_DIM // 2
    n_grp = LANES // half
    qr = x.shape[0] // n_grp
    lane = lax.broadcasted_iota(jnp.int32, (1, LANES), 1)
    grp = lane // half
    assert x.shape[0] % LANES == 0 and LANES % qr == 0
    prow = pos_ref[pl.ds(batch_row, 1), :].astype(jnp.float32)
    n_cols = x.shape[0] // LANES
    pos_t = jnp.concatenate([prow[:, k * LANES:(k + 1) * LANES] for k in range(n_cols)]
                            + [jnp.zeros((LANES - n_cols, LANES), jnp.float32)], axis=0).T

    def pos_col(j):
        first = j * qr
        return pos_t[first % LANES:first % LANES + qr, first // LANES:first // LANES + 1]

    pos_dense = pos_col(0)
    for j in range(1, n_grp):
        pos_dense = jnp.where(grp == j, pos_col(j), pos_dense)
    ang = pos_dense * invf_ref[...]
    cos_dense = jnp.cos(ang)
    sin_dense = jnp.sin(ang)

    def replicate(dense, j):
        only_j = jnp.where(grp == j, dense, 0.0)
        out = only_j
        for r in range(1, n_grp):
            out = out + pltpu.roll(only_j, r * half, 1)
        return out

    cos = jnp.concatenate([replicate(cos_dense, j) for j in range(n_grp)], axis=0)
    sin = jnp.concatenate([replicate(sin_dense, j) for j in range(n_grp)], axis=0)
    first_half = (lane % A_HEAD_DIM) < half
    sin_signed = jnp.where(first_half, -sin, sin)
    a_q_scale = (A_HEAD_DIM ** -0.5) * LOG2E
    b_q_scale = B_KEY_DIM ** -0.5
    yield 300

    for c0 in range(0, PROJ_WIDTH, IN_CHUNK):
        cw = min(IN_CHUNK, PROJ_WIDTH - c0)
        acc = jnp.dot(h, w_ref[:, c0:c0 + cw], preferred_element_type=jnp.float32)
        for s0 in range(0, cw, LANES):
            col = c0 + s0
            slab = acc[:, s0:s0 + LANES]
            if COL_A_Q <= col < COL_A_Q + A_WIDTH:
                slab = _rope_slab(slab, cos, sin_signed, first_half) * a_q_scale
            elif COL_A_K <= col < COL_A_K + A_KV_WIDTH:
                slab = _rope_slab(slab, cos, sin_signed, first_half)
            elif COL_B_Q <= col < COL_B_Q + B_QK_WIDTH:
                slab = slab * b_q_scale
            mix_ref[:, col:col + LANES] = slab.astype(mix_ref.dtype)
        yield cw


SWA_ROWS = 2 * BLOCK
SWA_COLS = 4 * 2 * BLOCK


def _swa_bias(sink_ref, g, first_block):
    row = lax.broadcasted_iota(jnp.int32, (SWA_ROWS, SWA_COLS), 0)
    col = lax.broadcasted_iota(jnp.int32, (SWA_ROWS, SWA_COLS), 1)
    qi, sub = row % BLOCK, row // BLOCK
    ki, hi = col % (2 * BLOCK), col // (2 * BLOCK)
    rel = qi + BLOCK - ki
    valid = (rel >= 0) & (rel < WINDOW)
    if first_block:
        valid = valid & (ki >= BLOCK)
    sink = jnp.zeros((SWA_ROWS, SWA_COLS), jnp.float32)
    for s in range(2):
        for i in range(4):
            sink = jnp.where((sub == s) & (hi == i), sink_ref[g * A_GROUP + s * 4 + i] * LOG2E, sink)
    return jnp.where(ki == 0, sink, jnp.where(valid, 0.0, NEG_INF))


def _swa_block(mix_ref, r0, k_prev, v_prev, first, bias_ref, oa_ref):
    bf16 = jnp.bfloat16
    rows = slice(r0, r0 + BLOCK)
    krow = lax.broadcasted_iota(jnp.int32, (2 * BLOCK, A_KV_WIDTH), 0)
    lane = lax.broadcasted_iota(jnp.int32, (2 * BLOCK, A_KV_WIDTH), 1)
    k = jnp.concatenate([k_prev, mix_ref[rows, COL_A_K:COL_A_K + A_KV_WIDTH]], axis=0)
    v = jnp.concatenate([v_prev, mix_ref[rows, COL_A_V:COL_A_V + A_KV_WIDTH]], axis=0)
    k = jnp.where(krow == 0, jnp.zeros_like(k), k)
    v = jnp.where(krow == 0, jnp.zeros_like(v), v)
    k_t = k.T
    v_swapped = pltpu.roll(v, A_HEAD_DIM, 1)
    lo = lane < A_HEAD_DIM
    ones_lo = jnp.where(lo, 1.0, 0.0).astype(bf16)
    ones_hi = jnp.where(lo, 0.0, 1.0).astype(bf16)
    zero_kt = jnp.zeros((A_HEAD_DIM, 2 * BLOCK), bf16)
    zero_v = jnp.zeros_like(v)

    for g in range(A_KV_HEADS):
        k_tg = k_t[g * A_HEAD_DIM:(g + 1) * A_HEAD_DIM, :]
        w_k = jnp.concatenate(
            [jnp.concatenate([k_tg if c == i else zero_kt for c in range(4)], axis=1) for i in range(4)],
            axis=0)
        c0 = COL_A_Q + g * A_GROUP * A_HEAD_DIM
        q_g = jnp.concatenate([mix_ref[rows, c0:c0 + 256], mix_ref[rows, c0 + 256:c0 + 512]], axis=0)
        s = jnp.dot(q_g, w_k, preferred_element_type=jnp.float32) + bias_ref[first, g]
        ps = []
        for i in range(4):
            s_i = s[:, i * 2 * BLOCK:(i + 1) * 2 * BLOCK]
            ps.append(jnp.exp2(s_i - jnp.max(s_i, axis=-1, keepdims=True)).astype(bf16))
        yield 550
        v_lo = jnp.where(lo, v if g == 0 else v_swapped, zero_v)
        v_hi = jnp.where(lo, zero_v, v_swapped if g == 0 else v)
        w_v = jnp.concatenate([jnp.concatenate([v_lo, ones_lo], axis=1),
                               jnp.concatenate([v_hi, ones_hi], axis=1)], axis=0)
        p = jnp.concatenate([jnp.concatenate([ps[0], ps[1]], axis=1),
                             jnp.concatenate([ps[2], ps[3]], axis=1)], axis=0)
        r = jnp.dot(p, w_v, preferred_element_type=jnp.float32)
        o_n = r[:, :LANES] / r[:, LANES:]
        for pair in range(2):
            for sub in range(2):
                off = g * A_GROUP * A_HEAD_DIM + sub * 256 + pair * LANES
                piece = o_n[pair * SWA_ROWS + sub * BLOCK:pair * SWA_ROWS + (sub + 1) * BLOCK, :]
                gate = mix_ref[rows, COL_A_GATE + off:COL_A_GATE + off + LANES].astype(jnp.float32)
                oa_ref[rows, off:off + LANES] = (piece * _silu_of_half(gate)).astype(oa_ref.dtype)
        yield 400


def _swa_tile(mix_ref, first_tile, kv_prev_ref, bias_ref, oa_ref):
    first = first_tile.astype(jnp.int32)
    k_prev = kv_prev_ref[:, :A_KV_WIDTH]
    v_prev = kv_prev_ref[:, A_KV_WIDTH:]
    for blk in range(TILE // BLOCK):
        r0 = blk * BLOCK
        yield from _swa_block(mix_ref, r0, k_prev, v_prev, first if blk == 0 else 0, bias_ref, oa_ref)
        k_prev = mix_ref[r0:r0 + BLOCK, COL_A_K:COL_A_K + A_KV_WIDTH]
        v_prev = mix_ref[r0:r0 + BLOCK, COL_A_V:COL_A_V + A_KV_WIDTH]
    kv_prev_ref[:, :A_KV_WIDTH] = k_prev
    kv_prev_ref[:, A_KV_WIDTH:] = v_prev


def _chunk_tril():
    ti = lax.broadcasted_iota(jnp.int32, (TILE, TILE), 0)
    tj = lax.broadcasted_iota(jnp.int32, (TILE, TILE), 1)
    return jnp.where((ti // CHUNK == tj // CHUNK) & (ti >= tj), 1.0, 0.0).astype(jnp.bfloat16)


def _chunk_cumsum_rows(x, tril_ref):
    hi = x.astype(jnp.bfloat16)
    lo = (x - hi.astype(jnp.float32)).astype(jnp.bfloat16)
    both = jnp.dot(tril_ref[...], jnp.concatenate([hi, lo], axis=1), preferred_element_type=jnp.float32)
    return both[:, :x.shape[1]] + both[:, x.shape[1]:]


def _gla_tile(mix_ref, first_tile, up_ref, bias_ref, nw_ref, ob_ref,
              state_ref, qe_ref, ke_ref, ks_ref, dcol_ref, inc_ref, oacc_ref, tril_ref):
    bf16 = jnp.bfloat16
    n_chunks = TILE // CHUNK

    gk = jnp.dot(mix_ref[:, COL_B_LOW:COL_B_LOW + LANES], up_ref[...],
                 preferred_element_type=jnp.float32) + bias_ref[...]
    soft = jnp.log2(1.0 + jnp.exp2(jnp.minimum(gk, -gk) * LOG2E))
    log_a = jnp.minimum(gk, 0.0) * (1.0 / GATE_TAU) - soft * (LN2 / GATE_TAU)
    b = _chunk_cumsum_rows(log_a, tril_ref)
    yield 700
    b_last_rows = [b[(c + 1) * CHUNK - 1:(c + 1) * CHUNK, :] for c in range(n_chunks)]
    b_last = jnp.concatenate([jnp.broadcast_to(r, (CHUNK, B_QK_WIDTH)) for r in b_last_rows], axis=0)
    qf = mix_ref[:, COL_B_Q:COL_B_Q + B_QK_WIDTH].astype(jnp.float32)
    kf = mix_ref[:, COL_B_K:COL_B_K + B_QK_WIDTH].astype(jnp.float32)
    qe_ref[...] = (qf * jnp.exp(b)).astype(bf16)
    ke_ref[...] = (kf * jnp.exp(-b)).astype(bf16)
    ks_ref[...] = (kf * jnp.exp(b_last - b)).astype(bf16)
    pad = jnp.zeros((LANES - n_chunks, B_QK_WIDTH), jnp.float32)
    dcol_ref[...] = jnp.exp(jnp.concatenate(b_last_rows + [pad], axis=0)).T
    yield 700

    ti = lax.broadcasted_iota(jnp.int32, (TILE, TILE), 0)
    tj = lax.broadcasted_iota(jnp.int32, (TILE, TILE), 1)
    keep = (ti // CHUNK == tj // CHUNK) & (ti >= tj)
    for h in range(B_HEADS):
        ks = slice(h * B_KEY_DIM, (h + 1) * B_KEY_DIM)
        vs = slice(COL_B_V + h * B_VAL_DIM, COL_B_V + (h + 1) * B_VAL_DIM)
        os_ = slice(h * B_VAL_DIM, (h + 1) * B_VAL_DIM)
        att = lax.dot_general(qe_ref[:, ks], ke_ref[:, ks], (((1,), (1,)), ((), ())),
                              preferred_element_type=jnp.float32)
        att = jnp.where(keep, att, 0.0).astype(bf16)
        oacc_ref[:, os_] = jnp.dot(att, mix_ref[:, vs], preferred_element_type=jnp.float32)
        for c in range(n_chunks):
            rows = slice(c * CHUNK, (c + 1) * CHUNK)
            inc_ref[c, h] = lax.dot_general(ks_ref[rows, ks], mix_ref[rows, vs], (((0,), (0,)), ((), ())),
                                            preferred_element_type=jnp.float32)
        yield 350

    for h in range(B_HEADS):
        ks = slice(h * B_KEY_DIM, (h + 1) * B_KEY_DIM)
        os_ = slice(h * B_VAL_DIM, (h + 1) * B_VAL_DIM)
        st = jnp.where(first_tile, 0.0, state_ref[h])
        for c in range(n_chunks):
            rows = slice(c * CHUNK, (c + 1) * CHUNK)
            oacc_ref[rows, os_] += jnp.dot(qe_ref[rows, ks], st.astype(bf16),
                                           preferred_element_type=jnp.float32)
            st = dcol_ref[ks, c:c + 1] * st + inc_ref[c, h]
        state_ref[h] = st
        yield 150

    for h in range(B_HEADS):
        os_ = slice(h * B_VAL_DIM, (h + 1) * B_VAL_DIM)
        o_h = oacc_ref[:, os_]
        var = jnp.mean(o_h * o_h, axis=-1, keepdims=True)
        o_n = o_h * lax.rsqrt(var + EPS) * nw_ref[...]
        gate = mix_ref[:, COL_B_GATE + h * B_VAL_DIM:COL_B_GATE + (h + 1) * B_VAL_DIM].astype(jnp.float32)
        ob_ref[:, os_] = (o_n * _silu_of_half(gate)).astype(ob_ref.dtype)
        yield 225


def _emit_interleaved(streams, weights):
    spent = [0.0] * len(streams)
    live = list(range(len(streams)))
    while live:
        k = min(live, key=lambda j: spent[j])
        try:
            spent[k] += next(streams[k]) * weights[k]
        except StopIteration:
            live.remove(k)


def _branch_proj(o_ref, w_ref):
    return jnp.dot(o_ref[...], w_ref[...], preferred_element_type=jnp.float32)


def _merge_out_tile(mix_ref, y_a, y_b, xlag_ref, wo_ref, fw_ref, out_ref):
    hm_a = mix_ref[:, COL_M_A:COL_M_A + D_MODEL].astype(jnp.float32)
    hm_b = mix_ref[:, COL_M_B:COL_M_B + D_MODEL].astype(jnp.float32)
    merged = ((jnp.tanh(hm_a) + 1.0) * y_a + (jnp.tanh(hm_b) + 1.0) * y_b).astype(jnp.bfloat16)
    yield 500
    z = xlag_ref[...] + jnp.dot(merged, wo_ref[...], preferred_element_type=jnp.float32)
    var = jnp.mean(z * z, axis=-1, keepdims=True)
    out_ref[...] = z * lax.rsqrt(var + EPS) * fw_ref[...]
    yield 1100


PROJECT_COST = 600 + PROJ_WIDTH
FINISH_COST = 3800 + 4100 + 1200 + 1600


def _block_kernel(tiles_per_seq, sink_ref, x_ref, xlag_ref, pos_ref, nw_ref, invf_ref, w_in_hbm, up_f32_ref,
                  gbias_ref, bnw_ref, wa_hbm, wb_hbm, wo_hbm, fw_ref, out_ref,
                  w_ref, wa_ref, wb_ref, wo_ref, stage_ref, wsem_ref, up_ref,
                  mix_a, mix_b, oa_ref, ob_ref, kv_prev_ref, sbias_ref,
                  state_ref, qe_ref, ke_ref, ks_ref, dcol_ref, inc_ref, oacc_ref, tril_ref):
    i = pl.program_id(0)
    n = pl.num_programs(0) - 1

    def project(dst):
        yield from _project_tile(x_ref, pos_ref, i // tiles_per_seq, nw_ref, invf_ref, w_ref, dst)

    def finish(src):
        first_tile = ((i - 1) % tiles_per_seq) == 0
        yield from _swa_tile(src, first_tile, kv_prev_ref, sbias_ref, oa_ref)
        y_a = _branch_proj(oa_ref, wa_ref)
        yield 600
        yield from _gla_tile(src, first_tile, up_ref, gbias_ref, bnw_ref, ob_ref,
                             state_ref, qe_ref, ke_ref, ks_ref, dcol_ref, inc_ref, oacc_ref, tril_ref)
        y_b = _branch_proj(ob_ref, wb_ref)
        yield 600
        yield from _merge_out_tile(src, y_a, y_b, xlag_ref, wo_ref, fw_ref, out_ref)

    both = (1.0, FINISH_COST / PROJECT_COST)

    @pl.when(i == 0)
    def _():
        for g in range(A_KV_HEADS):
            sbias_ref[0, g] = _swa_bias(sink_ref, g, False)
            sbias_ref[1, g] = _swa_bias(sink_ref, g, True)
        up_ref[...] = jnp.zeros_like(up_ref)
        up_ref[0:GATE_RANK, :] = up_f32_ref[...].astype(up_ref.dtype)
        kv_prev_ref[...] = jnp.zeros_like(kv_prev_ref)
        state_ref[...] = jnp.zeros_like(state_ref)
        tril_ref[...] = _chunk_tril()
        weights = _stream_weights(w_in_hbm, wa_hbm, wb_hbm, wo_hbm, w_ref, wa_ref, wb_ref, wo_ref,
                                  stage_ref, wsem_ref)
        for _ in weights:
            pass
        _emit_interleaved([project(mix_a)], (1.0,))

    @pl.when((i > 0) & (i < n) & (i % 2 == 1))
    def _():
        _emit_interleaved([finish(mix_a), project(mix_b)], both)

    @pl.when((i > 0) & (i < n) & (i % 2 == 0))
    def _():
        _emit_interleaved([finish(mix_b), project(mix_a)], both)

    @pl.when(i == n)
    def _():
        _emit_interleaved([finish(mix_b)], (1.0,))


def _block(x2, positions, norm_w, inv_freq_row, w_in_t, sinks, gate_up, gate_bias, b_norm_w, wa, wb, wo,
           final_w):
    bsz, t = positions.shape
    tiles_per_seq = t // TILE
    m = x2.shape[0]
    n = m // TILE
    assert w_in_t.shape == (IN_WIDTH, D_MODEL) and w_in_t.dtype == jnp.float32
    bf16 = jnp.bfloat16
    cur = lambda i: (jnp.minimum(i, n - 1), 0)
    lag = lambda i: (jnp.maximum(i - 1, 0), 0)
    const = lambda i: (0, 0)
    n_chunks = TILE // CHUNK
    return pl.pallas_call(
        functools.partial(_block_kernel, tiles_per_seq),
        grid=(n + 1,),
        in_specs=[
            pl.BlockSpec(memory_space=pltpu.SMEM),
            pl.BlockSpec((TILE, D_MODEL), cur),
            pl.BlockSpec((TILE, D_MODEL), lag),
            pl.BlockSpec((bsz, TILE), lambda i: (0, jnp.minimum(i, n - 1) % tiles_per_seq)),
            pl.BlockSpec((1, D_MODEL), const),
            pl.BlockSpec((1, LANES), const),
            pl.BlockSpec(memory_space=pl.ANY),
            pl.BlockSpec((GATE_RANK, B_QK_WIDTH), const),
            pl.BlockSpec((1, B_QK_WIDTH), const),
            pl.BlockSpec((1, B_VAL_DIM), const),
            pl.BlockSpec(memory_space=pl.ANY),
            pl.BlockSpec(memory_space=pl.ANY),
            pl.BlockSpec(memory_space=pl.ANY),
            pl.BlockSpec((1, D_MODEL), const),
        ],
        out_specs=pl.BlockSpec((TILE, D_MODEL), lag),
        out_shape=jax.ShapeDtypeStruct((m, D_MODEL), jnp.float32),
        scratch_shapes=[
            pltpu.VMEM((D_MODEL, PROJ_WIDTH), bf16),
            pltpu.VMEM((A_WIDTH, D_MODEL), bf16),
            pltpu.VMEM((B_V_WIDTH, D_MODEL), bf16),
            pltpu.VMEM((D_MODEL, D_MODEL), bf16),
            pltpu.VMEM((W_STAGE_BUFS, W_SLAB, D_MODEL), jnp.float32),
            pltpu.SemaphoreType.DMA((W_STAGE_BUFS,)),
            pltpu.VMEM((LANES, B_QK_WIDTH), bf16),
            pltpu.VMEM((TILE, PROJ_WIDTH), bf16),
            pltpu.VMEM((TILE, PROJ_WIDTH), bf16),
            pltpu.VMEM((TILE, A_WIDTH), bf16),
            pltpu.VMEM((TILE, B_V_WIDTH), bf16),
            pltpu.VMEM((BLOCK, 2 * A_KV_WIDTH), bf16),
            pltpu.VMEM((2, A_KV_HEADS, SWA_ROWS, SWA_COLS), jnp.float32),
            pltpu.VMEM((B_HEADS, B_KEY_DIM, B_VAL_DIM), jnp.float32),
            pltpu.VMEM((TILE, B_QK_WIDTH), bf16),
            pltpu.VMEM((TILE, B_QK_WIDTH), bf16),
            pltpu.VMEM((TILE, B_QK_WIDTH), bf16),
            pltpu.VMEM((B_QK_WIDTH, LANES), jnp.float32),
            pltpu.VMEM((n_chunks, B_HEADS, B_KEY_DIM, B_VAL_DIM), jnp.float32),
            pltpu.VMEM((TILE, B_V_WIDTH), jnp.float32),
            pltpu.VMEM((TILE, TILE), bf16),
        ],
        compiler_params=pltpu.CompilerParams(dimension_semantics=("arbitrary",),
                                             vmem_limit_bytes=VMEM_LIMIT_BYTES),
        name="block",
    )(sinks, x2, x2, positions, norm_w, inv_freq_row, w_in_t, gate_up, gate_bias, b_norm_w, wa, wb, wo,
      final_w)


def kernel(x, positions, norm_w, w_in, a_sinks, b_gate_up, b_gate_bias, b_out_norm_w,
           w_a_proj, w_b_proj, w_out, final_norm_w):
    bsz, t, d = x.shape
    assert d == D_MODEL and t % (2 * TILE) == 0
    depth = norm_w.shape[0]
    assert depth == 1

    half = A_HEAD_DIM // 2
    inv_freq = ROPE_THETA ** (-jnp.arange(half, dtype=jnp.float32) / half)
    inv_freq_row = jnp.tile(inv_freq, LANES // half)[None, :]

    x2 = x.reshape(bsz * t, d)
    for layer in range(depth):
        x2 = _block(x2, positions, norm_w[layer][None, :], inv_freq_row, w_in[layer].T, a_sinks[layer],
                    b_gate_up[layer], b_gate_bias[layer][None, :], b_out_norm_w[layer][None, :],
                    w_a_proj[layer], w_b_proj[layer], w_out[layer], final_norm_w[None, :])
    return x2.reshape(bsz, t, d)
```

```python
import functools

import jax
import jax.numpy as jnp
from jax import lax
from jax.experimental import pallas as pl
from jax.experimental.pallas import tpu as pltpu

D_MODEL = 1024
A_HEADS = 16
A_KV_HEADS = 2
A_HEAD_DIM = 64
A_GROUP = A_HEADS // A_KV_HEADS
A_WIDTH = A_HEADS * A_HEAD_DIM
A_KV_WIDTH = A_KV_HEADS * A_HEAD_DIM
WINDOW = 128
BLOCK = 128
ROPE_THETA = 10000.0

B_HEADS = 4
B_QK_WIDTH = D_MODEL // 2
B_V_WIDTH = D_MODEL
B_KEY_DIM = B_QK_WIDTH // B_HEADS
B_VAL_DIM = B_V_WIDTH // B_HEADS
GATE_RANK = 16
GATE_TAU = 16.0
CHUNK = 64

EPS = 1e-5
NEG_INF = -1e30
LOG2E = 1.4426950408889634
LN2 = 0.6931471805599453

LANES = 128

COL_A_Q = 0
COL_A_GATE = 1024
COL_B_V = 2048
COL_B_GATE = 3072
COL_B_Q = 4096
COL_B_K = 4608
COL_A_K = 5120
COL_A_V = 5248
COL_B_LOW = 5376
MIX_COLS = 5504
COL_M_A = MIX_COLS
COL_M_B = MIX_COLS + D_MODEL
PROJ_WIDTH = MIX_COLS + 2 * D_MODEL

VMEM_LIMIT_BYTES = 60 * 1024 * 1024


IN_WIDTH = 2 * A_WIDTH + 2 * A_KV_WIDTH + 2 * B_QK_WIDTH + 2 * B_V_WIDTH + GATE_RANK + 2 * D_MODEL
_W_IN_PIECES = ((A_WIDTH, COL_A_Q), (A_KV_WIDTH, COL_A_K), (A_KV_WIDTH, COL_A_V), (A_WIDTH, COL_A_GATE),
                (B_QK_WIDTH, COL_B_Q), (B_QK_WIDTH, COL_B_K), (B_V_WIDTH, COL_B_V), (B_V_WIDTH, COL_B_GATE),
                (GATE_RANK, COL_B_LOW), (D_MODEL, COL_M_A), (D_MODEL, COL_M_B))
W_SLAB = 128
W_STAGE_BUFS = 12


def _stream_weights(w_in_hbm, wa_hbm, wb_hbm, wo_hbm, w_ref, wa_ref, wb_ref, wo_ref, stage_ref, sem_ref):
    jobs = []
    src = 0
    for width, dst in _W_IN_PIECES:
        for r in range(0, width, W_SLAB):
            jobs.append((w_in_hbm, src + r, True, w_ref, dst + r, min(W_SLAB, width - r)))
        src += width
    jobs.sort(key=lambda j: j[4])
    for hbm, dst_ref in ((wa_hbm, wa_ref), (wb_hbm, wb_ref), (wo_hbm, wo_ref)):
        for r in range(0, D_MODEL, W_SLAB):
            jobs.append((hbm, r, False, dst_ref, r, W_SLAB))

    def copy(j):
        hbm, row = jobs[j][0], jobs[j][1]
        slot = j % W_STAGE_BUFS
        return pltpu.make_async_copy(hbm.at[pl.ds(row, W_SLAB), :], stage_ref.at[slot], sem_ref.at[slot])

    for j in range(min(W_STAGE_BUFS, len(jobs))):
        copy(j).start()
    for j, (_, _, transpose, dst_ref, off, valid) in enumerate(jobs):
        copy(j).wait()
        slab = stage_ref[j % W_STAGE_BUFS]
        halved_cols = any(c <= off < c + D_MODEL for c in (COL_A_GATE, COL_B_GATE, COL_M_A, COL_M_B))
        if (transpose and halved_cols) or dst_ref is wo_ref:
            slab = slab * 0.5
        if transpose:
            slab = slab.T
            if valid < W_SLAB:
                lane = lax.broadcasted_iota(jnp.int32, slab.shape, 1)
                slab = jnp.where(lane < valid, slab, 0.0)
            dst_ref[:, off:off + W_SLAB] = slab.astype(dst_ref.dtype)
        else:
            dst_ref[off:off + W_SLAB, :] = slab.astype(dst_ref.dtype)
        if j + W_STAGE_BUFS < len(jobs):
            copy(j + W_STAGE_BUFS).start()
        yield (dst_ref, off + W_SLAB)


TILE = 256
IN_CHUNK = 512


def _silu_of_half(hx):
    return hx * jnp.tanh(hx) + hx


def _rope_slab(xs, cos, sin_signed, first_half):
    partner = jnp.where(first_half, pltpu.roll(xs, LANES - A_HEAD_DIM // 2, 1),
                        pltpu.roll(xs, A_HEAD_DIM // 2, 1))
    return xs * cos + partner * sin_signed


def _project_tile(x_ref, pos_ref, row0, batch_row, nw_ref, invf_ref, w_ref, mix_ref):
    x = x_ref[row0:row0 + TILE, :]
    var = jnp.mean(x * x, axis=-1, keepdims=True)
    h = (x * lax.rsqrt(var + EPS) * nw_ref[...]).astype(jnp.bfloat16)
    yield 300

    half = A_HEAD_DIM // 2
    n_grp = LANES // half
    qr = x.shape[0] // n_grp
    lane = lax.broadcasted_iota(jnp.int32, (1, LANES), 1)
    grp = lane // half
    assert x.shape[0] % LANES == 0 and LANES % qr == 0
    prow = pos_ref[pl.ds(batch_row, 1), row0:row0 + TILE].astype(jnp.float32)
    n_cols = x.shape[0] // LANES
    pos_t = jnp.concatenate([prow[:, k * LANES:(k + 1) * LANES] for k in range(n_cols)]
                            + [jnp.zeros((LANES - n_cols, LANES), jnp.float32)], axis=0).T

    def pos_col(j):
        first = j * qr
        return pos_t[first % LANES:first % LANES + qr, first // LANES:first // LANES + 1]

    pos_dense = pos_col(0)
    for j in range(1, n_grp):
        pos_dense = jnp.where(grp == j, pos_col(j), pos_dense)
    ang = pos_dense * invf_ref[...]
    cos_dense = jnp.cos(ang)
    sin_dense = jnp.sin(ang)

    def replicate(dense, j):
        only_j = jnp.where(grp == j, dense, 0.0)
        out = only_j
        for r in range(1, n_grp):
            out = out + pltpu.roll(only_j, r * half, 1)
        return out

    cos = jnp.concatenate([replicate(cos_dense, j) for j in range(n_grp)], axis=0)
    sin = jnp.concatenate([replicate(sin_dense, j) for j in range(n_grp)], axis=0)
    first_half = (lane % A_HEAD_DIM) < half
    sin_signed = jnp.where(first_half, -sin, sin)
    a_q_scale = (A_HEAD_DIM ** -0.5) * LOG2E
    b_q_scale = B_KEY_DIM ** -0.5
    yield 300

    for c0 in range(0, PROJ_WIDTH, IN_CHUNK):
        cw = min(IN_CHUNK, PROJ_WIDTH - c0)
        acc = jnp.dot(h, w_ref[:, c0:c0 + cw], preferred_element_type=jnp.float32)
        for s0 in range(0, cw, LANES):
            col = c0 + s0
            slab = acc[:, s0:s0 + LANES]
            if COL_A_Q <= col < COL_A_Q + A_WIDTH:
                slab = _rope_slab(slab, cos, sin_signed, first_half) * a_q_scale
            elif COL_A_K <= col < COL_A_K + A_KV_WIDTH:
                slab = _rope_slab(slab, cos, sin_signed, first_half)
            elif COL_B_Q <= col < COL_B_Q + B_QK_WIDTH:
                slab = slab * b_q_scale
            mix_ref[:, col:col + LANES] = slab.astype(mix_ref.dtype)
        yield cw


SWA_ROWS = 2 * BLOCK
SWA_COLS = 4 * 2 * BLOCK


def _swa_bias(sink_ref, g, first_block):
    row = lax.broadcasted_iota(jnp.int32, (SWA_ROWS, SWA_COLS), 0)
    col = lax.broadcasted_iota(jnp.int32, (SWA_ROWS, SWA_COLS), 1)
    qi, sub = row % BLOCK, row // BLOCK
    ki, hi = col % (2 * BLOCK), col // (2 * BLOCK)
    rel = qi + BLOCK - ki
    valid = (rel >= 0) & (rel < WINDOW)
    if first_block:
        valid = valid & (ki >= BLOCK)
    sink = jnp.zeros((SWA_ROWS, SWA_COLS), jnp.float32)
    for s in range(2):
        for i in range(4):
            sink = jnp.where((sub == s) & (hi == i), sink_ref[g * A_GROUP + s * 4 + i] * LOG2E, sink)
    return jnp.where(ki == 0, sink, jnp.where(valid, 0.0, NEG_INF))


def _swa_block(mix_ref, r0, k_prev, v_prev, first, bias_ref, oa_ref):
    bf16 = jnp.bfloat16
    rows = slice(r0, r0 + BLOCK)
    krow = lax.broadcasted_iota(jnp.int32, (2 * BLOCK, A_KV_WIDTH), 0)
    lane = lax.broadcasted_iota(jnp.int32, (2 * BLOCK, A_KV_WIDTH), 1)
    k = jnp.concatenate([k_prev, mix_ref[rows, COL_A_K:COL_A_K + A_KV_WIDTH]], axis=0)
    v = jnp.concatenate([v_prev, mix_ref[rows, COL_A_V:COL_A_V + A_KV_WIDTH]], axis=0)
    k = jnp.where(krow == 0, jnp.zeros_like(k), k)
    v = jnp.where(krow == 0, jnp.zeros_like(v), v)
    k_t = k.T
    v_swapped = pltpu.roll(v, A_HEAD_DIM, 1)
    lo = lane < A_HEAD_DIM
    ones_lo = jnp.where(lo, 1.0, 0.0).astype(bf16)
    ones_hi = jnp.where(lo, 0.0, 1.0).astype(bf16)
    zero_kt = jnp.zeros((A_HEAD_DIM, 2 * BLOCK), bf16)
    zero_v = jnp.zeros_like(v)

    for g in range(A_KV_HEADS):
        k_tg = k_t[g * A_HEAD_DIM:(g + 1) * A_HEAD_DIM, :]
        w_k = jnp.concatenate(
            [jnp.concatenate([k_tg if c == i else zero_kt for c in range(4)], axis=1) for i in range(4)],
            axis=0)
        c0 = COL_A_Q + g * A_GROUP * A_HEAD_DIM
        q_g = jnp.concatenate([mix_ref[rows, c0:c0 + 256], mix_ref[rows, c0 + 256:c0 + 512]], axis=0)
        s = jnp.dot(q_g, w_k, preferred_element_type=jnp.float32) + bias_ref[first, g]
        ps = []
        for i in range(4):
            s_i = s[:, i * 2 * BLOCK:(i + 1) * 2 * BLOCK]
            ps.append(jnp.exp2(s_i - jnp.max(s_i, axis=-1, keepdims=True)).astype(bf16))
        yield 550
        v_lo = jnp.where(lo, v if g == 0 else v_swapped, zero_v)
        v_hi = jnp.where(lo, zero_v, v_swapped if g == 0 else v)
        w_v = jnp.concatenate([jnp.concatenate([v_lo, ones_lo], axis=1),
                               jnp.concatenate([v_hi, ones_hi], axis=1)], axis=0)
        p = jnp.concatenate([jnp.concatenate([ps[0], ps[1]], axis=1),
                             jnp.concatenate([ps[2], ps[3]], axis=1)], axis=0)
        r = jnp.dot(p, w_v, preferred_element_type=jnp.float32)
        o_n = r[:, :LANES] / r[:, LANES:]
        for pair in range(2):
            for sub in range(2):
                off = g * A_GROUP * A_HEAD_DIM + sub * 256 + pair * LANES
                piece = o_n[pair * SWA_ROWS + sub * BLOCK:pair * SWA_ROWS + (sub + 1) * BLOCK, :]
                gate = mix_ref[rows, COL_A_GATE + off:COL_A_GATE + off + LANES].astype(jnp.float32)
                oa_ref[rows, off:off + LANES] = (piece * _silu_of_half(gate)).astype(oa_ref.dtype)
        yield 400


def _swa_tile(mix_ref, first_tile, kv_prev_ref, bias_ref, oa_ref):
    first = first_tile.astype(jnp.int32)
    k_prev = kv_prev_ref[:, :A_KV_WIDTH]
    v_prev = kv_prev_ref[:, A_KV_WIDTH:]
    for blk in range(TILE // BLOCK):
        r0 = blk * BLOCK
        yield from _swa_block(mix_ref, r0, k_prev, v_prev, first if blk == 0 else 0, bias_ref, oa_ref)
        k_prev = mix_ref[r0:r0 + BLOCK, COL_A_K:COL_A_K + A_KV_WIDTH]
        v_prev = mix_ref[r0:r0 + BLOCK, COL_A_V:COL_A_V + A_KV_WIDTH]
    kv_prev_ref[:, :A_KV_WIDTH] = k_prev
    kv_prev_ref[:, A_KV_WIDTH:] = v_prev


def _chunk_tril():
    ti = lax.broadcasted_iota(jnp.int32, (TILE, TILE), 0)
    tj = lax.broadcasted_iota(jnp.int32, (TILE, TILE), 1)
    return jnp.where((ti // CHUNK == tj // CHUNK) & (ti >= tj), 1.0, 0.0).astype(jnp.bfloat16)


def _chunk_cumsum_rows(x, tril_ref):
    hi = x.astype(jnp.bfloat16)
    lo = (x - hi.astype(jnp.float32)).astype(jnp.bfloat16)
    both = jnp.dot(tril_ref[...], jnp.concatenate([hi, lo], axis=1), preferred_element_type=jnp.float32)
    return both[:, :x.shape[1]] + both[:, x.shape[1]:]


def _gla_tile(mix_ref, first_tile, up_ref, bias_ref, nw_ref, ob_ref,
              state_ref, qe_ref, ke_ref, ks_ref, dcol_ref, inc_ref, oacc_ref, tril_ref):
    bf16 = jnp.bfloat16
    n_chunks = TILE // CHUNK

    gk = jnp.dot(mix_ref[:, COL_B_LOW:COL_B_LOW + LANES], up_ref[...],
                 preferred_element_type=jnp.float32) + bias_ref[...]
    soft = jnp.log2(1.0 + jnp.exp2(jnp.minimum(gk, -gk) * LOG2E))
    log_a = jnp.minimum(gk, 0.0) * (1.0 / GATE_TAU) - soft * (LN2 / GATE_TAU)
    b = _chunk_cumsum_rows(log_a, tril_ref)
    yield 700
    b_last_rows = [b[(c + 1) * CHUNK - 1:(c + 1) * CHUNK, :] for c in range(n_chunks)]
    b_last = jnp.concatenate([jnp.broadcast_to(r, (CHUNK, B_QK_WIDTH)) for r in b_last_rows], axis=0)
    qf = mix_ref[:, COL_B_Q:COL_B_Q + B_QK_WIDTH].astype(jnp.float32)
    kf = mix_ref[:, COL_B_K:COL_B_K + B_QK_WIDTH].astype(jnp.float32)
    qe_ref[...] = (qf * jnp.exp(b)).astype(bf16)
    ke_ref[...] = (kf * jnp.exp(-b)).astype(bf16)
    ks_ref[...] = (kf * jnp.exp(b_last - b)).astype(bf16)
    pad = jnp.zeros((LANES - n_chunks, B_QK_WIDTH), jnp.float32)
    dcol_ref[...] = jnp.exp(jnp.concatenate(b_last_rows + [pad], axis=0)).T
    yield 700

    ti = lax.broadcasted_iota(jnp.int32, (TILE, TILE), 0)
    tj = lax.broadcasted_iota(jnp.int32, (TILE, TILE), 1)
    keep = (ti // CHUNK == tj // CHUNK) & (ti >= tj)
    for h in range(B_HEADS):
        ks = slice(h * B_KEY_DIM, (h + 1) * B_KEY_DIM)
        vs = slice(COL_B_V + h * B_VAL_DIM, COL_B_V + (h + 1) * B_VAL_DIM)
        os_ = slice(h * B_VAL_DIM, (h + 1) * B_VAL_DIM)
        att = lax.dot_general(qe_ref[:, ks], ke_ref[:, ks], (((1,), (1,)), ((), ())),
                              preferred_element_type=jnp.float32)
        att = jnp.where(keep, att, 0.0).astype(bf16)
        oacc_ref[:, os_] = jnp.dot(att, mix_ref[:, vs], preferred_element_type=jnp.float32)
        for c in range(n_chunks):
            rows = slice(c * CHUNK, (c + 1) * CHUNK)
            inc_ref[c, h] = lax.dot_general(ks_ref[rows, ks], mix_ref[rows, vs], (((0,), (0,)), ((), ())),
                                            preferred_element_type=jnp.float32)
        yield 350

    for h in range(B_HEADS):
        ks = slice(h * B_KEY_DIM, (h + 1) * B_KEY_DIM)
        os_ = slice(h * B_VAL_DIM, (h + 1) * B_VAL_DIM)
        st = jnp.where(first_tile, 0.0, state_ref[h])
        for c in range(n_chunks):
            rows = slice(c * CHUNK, (c + 1) * CHUNK)
            oacc_ref[rows, os_] += jnp.dot(qe_ref[rows, ks], st.astype(bf16),
                                           preferred_element_type=jnp.float32)
            st = dcol_ref[ks, c:c + 1] * st + inc_ref[c, h]
        state_ref[h] = st
        yield 150

    for h in range(B_HEADS):
        os_ = slice(h * B_VAL_DIM, (h + 1) * B_VAL_DIM)
        o_h = oacc_ref[:, os_]
        var = jnp.mean(o_h * o_h, axis=-1, keepdims=True)
        o_n = o_h * lax.rsqrt(var + EPS) * nw_ref[...]
        gate = mix_ref[:, COL_B_GATE + h * B_VAL_DIM:COL_B_GATE + (h + 1) * B_VAL_DIM].astype(jnp.float32)
        ob_ref[:, os_] = (o_n * _silu_of_half(gate)).astype(ob_ref.dtype)
        yield 225


def _emit_interleaved(streams, weights):
    spent = [0.0] * len(streams)
    live = list(range(len(streams)))
    while live:
        k = min(live, key=lambda j: spent[j])
        try:
            spent[k] += next(streams[k]) * weights[k]
        except StopIteration:
            live.remove(k)


def _branch_proj(o_ref, w_ref):
    return jnp.dot(o_ref[...], w_ref[...], preferred_element_type=jnp.float32)


def _merge_out_tile(mix_ref, y_a, y_b, x_res, wo_ref, fw_ref, out_ref):
    hm_a = mix_ref[:, COL_M_A:COL_M_A + D_MODEL].astype(jnp.float32)
    hm_b = mix_ref[:, COL_M_B:COL_M_B + D_MODEL].astype(jnp.float32)
    merged = ((jnp.tanh(hm_a) + 1.0) * y_a + (jnp.tanh(hm_b) + 1.0) * y_b).astype(jnp.bfloat16)
    yield 500
    z = x_res() + jnp.dot(merged, wo_ref[...], preferred_element_type=jnp.float32)
    var = jnp.mean(z * z, axis=-1, keepdims=True)
    out_ref[...] = z * lax.rsqrt(var + EPS) * fw_ref[...]
    yield 1100


PROJECT_COST = 600 + PROJ_WIDTH
FINISH_COST = 3800 + 4100 + 1200 + 1600


def _block_kernel(tiles_per_seq, sink_ref, x_ref, pos_ref, nw_ref, invf_ref, w_in_hbm, up_f32_ref,
                  gbias_ref, bnw_ref, wa_hbm, wb_hbm, wo_hbm, fw_ref, out_hbm,
                  w_ref, wa_ref, wb_ref, wo_ref, stage_ref, wsem_ref, up_ref,
                  mix_a, mix_b, oa_ref, ob_ref, xprev_ref, ostage_ref, osem_ref, kv_prev_ref, sbias_ref,
                  state_ref, qe_ref, ke_ref, ks_ref, dcol_ref, inc_ref, oacc_ref, tril_ref):
    s = pl.program_id(0)
    last = pl.num_programs(0) - 1

    def project(k, dst):
        yield from _project_tile(x_ref, pos_ref, k * TILE, (2 * s + k) // tiles_per_seq,
                                 nw_ref, invf_ref, w_ref, dst)

    def finish(t, src, x_res, slot):
        first_tile = jnp.asarray((t % tiles_per_seq) == 0)
        yield from _swa_tile(src, first_tile, kv_prev_ref, sbias_ref, oa_ref)
        y_a = _branch_proj(oa_ref, wa_ref)
        yield 600
        yield from _gla_tile(src, first_tile, up_ref, gbias_ref, bnw_ref, ob_ref,
                             state_ref, qe_ref, ke_ref, ks_ref, dcol_ref, inc_ref, oacc_ref, tril_ref)
        y_b = _branch_proj(ob_ref, wb_ref)
        yield 600
        yield from _merge_out_tile(src, y_a, y_b, x_res, wo_ref, fw_ref, ostage_ref.at[slot])

    def out_copy(t, slot):
        row = pl.multiple_of(t * TILE, TILE)
        return pltpu.make_async_copy(ostage_ref.at[slot], out_hbm.at[pl.ds(row, TILE), :], osem_ref.at[slot])

    x_first = lambda: x_ref[0:TILE, :]
    x_saved = lambda: xprev_ref[...]
    both = (1.0, FINISH_COST / PROJECT_COST)

    @pl.when(s == 0)
    def _():
        for g in range(A_KV_HEADS):
            sbias_ref[0, g] = _swa_bias(sink_ref, g, False)
            sbias_ref[1, g] = _swa_bias(sink_ref, g, True)
        up_ref[...] = jnp.zeros_like(up_ref)
        up_ref[0:GATE_RANK, :] = up_f32_ref[...].astype(up_ref.dtype)
        kv_prev_ref[...] = jnp.zeros_like(kv_prev_ref)
        state_ref[...] = jnp.zeros_like(state_ref)
        tril_ref[...] = _chunk_tril()
        for _ in _stream_weights(w_in_hbm, wa_hbm, wb_hbm, wo_hbm, w_ref, wa_ref, wb_ref, wo_ref,
                                 stage_ref, wsem_ref):
            pass
        _emit_interleaved([project(0, mix_a)], (1.0,))
        _emit_interleaved([finish(0, mix_a, x_first, 0), project(1, mix_b)], both)
        out_copy(0, 0).start()
        xprev_ref[...] = x_ref[TILE:2 * TILE, :]

    @pl.when((s > 0) & (s < last))
    def _():
        _emit_interleaved([finish(2 * s - 1, mix_b, x_saved, 1), project(0, mix_a)], both)
        out_copy(2 * s - 1, 1).start()
        out_copy(2 * s - 2, 0).wait()
        _emit_interleaved([finish(2 * s, mix_a, x_first, 0), project(1, mix_b)], both)
        out_copy(2 * s, 0).start()
        xprev_ref[...] = x_ref[TILE:2 * TILE, :]
        out_copy(2 * s - 1, 1).wait()

    @pl.when(s == last)
    def _():
        _emit_interleaved([finish(2 * s - 1, mix_b, x_saved, 1)], (1.0,))
        out_copy(2 * s - 1, 1).start()
        out_copy(2 * s - 2, 0).wait()
        out_copy(2 * s - 1, 1).wait()


def _block(x2, positions, norm_w, inv_freq_row, w_in_t, sinks, gate_up, gate_bias, b_norm_w, wa, wb, wo,
           final_w):
    bsz, t = positions.shape
    tiles_per_seq = t // TILE
    m = x2.shape[0]
    assert tiles_per_seq % 2 == 0 and m % (2 * TILE) == 0
    pairs = m // (2 * TILE)
    pairs_per_seq = tiles_per_seq // 2
    assert w_in_t.shape == (IN_WIDTH, D_MODEL) and w_in_t.dtype == jnp.float32
    bf16 = jnp.bfloat16
    const = lambda i: (0, 0)
    n_chunks = TILE // CHUNK
    return pl.pallas_call(
        functools.partial(_block_kernel, tiles_per_seq),
        grid=(pairs + 1,),
        in_specs=[
            pl.BlockSpec(memory_space=pltpu.SMEM),
            pl.BlockSpec((2 * TILE, D_MODEL), lambda i: (jnp.minimum(i, pairs - 1), 0)),
            pl.BlockSpec((bsz, 2 * TILE), lambda i: (0, jnp.minimum(i, pairs - 1) % pairs_per_seq)),
            pl.BlockSpec((1, D_MODEL), const),
            pl.BlockSpec((1, LANES), const),
            pl.BlockSpec(memory_space=pl.ANY),
            pl.BlockSpec((GATE_RANK, B_QK_WIDTH), const),
            pl.BlockSpec((1, B_QK_WIDTH), const),
            pl.BlockSpec((1, B_VAL_DIM), const),
            pl.BlockSpec(memory_space=pl.ANY),
            pl.BlockSpec(memory_space=pl.ANY),
            pl.BlockSpec(memory_space=pl.ANY),
            pl.BlockSpec((1, D_MODEL), const),
        ],
        out_specs=pl.BlockSpec(memory_space=pl.ANY),
        out_shape=jax.ShapeDtypeStruct((m, D_MODEL), jnp.float32),
        scratch_shapes=[
            pltpu.VMEM((D_MODEL, PROJ_WIDTH), bf16),
            pltpu.VMEM((A_WIDTH, D_MODEL), bf16),
            pltpu.VMEM((B_V_WIDTH, D_MODEL), bf16),
            pltpu.VMEM((D_MODEL, D_MODEL), bf16),
            pltpu.VMEM((W_STAGE_BUFS, W_SLAB, D_MODEL), jnp.float32),
            pltpu.SemaphoreType.DMA((W_STAGE_BUFS,)),
            pltpu.VMEM((LANES, B_QK_WIDTH), bf16),
            pltpu.VMEM((TILE, PROJ_WIDTH), bf16),
            pltpu.VMEM((TILE, PROJ_WIDTH), bf16),
            pltpu.VMEM((TILE, A_WIDTH), bf16),
            pltpu.VMEM((TILE, B_V_WIDTH), bf16),
            pltpu.VMEM((TILE, D_MODEL), jnp.float32),
            pltpu.VMEM((2, TILE, D_MODEL), jnp.float32),
            pltpu.SemaphoreType.DMA((2,)),
            pltpu.VMEM((BLOCK, 2 * A_KV_WIDTH), bf16),
            pltpu.VMEM((2, A_KV_HEADS, SWA_ROWS, SWA_COLS), jnp.float32),
            pltpu.VMEM((B_HEADS, B_KEY_DIM, B_VAL_DIM), jnp.float32),
            pltpu.VMEM((TILE, B_QK_WIDTH), bf16),
            pltpu.VMEM((TILE, B_QK_WIDTH), bf16),
            pltpu.VMEM((TILE, B_QK_WIDTH), bf16),
            pltpu.VMEM((B_QK_WIDTH, LANES), jnp.float32),
            pltpu.VMEM((n_chunks, B_HEADS, B_KEY_DIM, B_VAL_DIM), jnp.float32),
            pltpu.VMEM((TILE, B_V_WIDTH), jnp.float32),
            pltpu.VMEM((TILE, TILE), bf16),
        ],
        compiler_params=pltpu.CompilerParams(dimension_semantics=("arbitrary",),
                                             vmem_limit_bytes=VMEM_LIMIT_BYTES),
        name="block",
    )(sinks, x2, positions, norm_w, inv_freq_row, w_in_t, gate_up, gate_bias, b_norm_w, wa, wb, wo,
      final_w)


def kernel(x, positions, norm_w, w_in, a_sinks, b_gate_up, b_gate_bias, b_out_norm_w,
           w_a_proj, w_b_proj, w_out, final_norm_w):
    bsz, t, d = x.shape
    assert d == D_MODEL and t % (2 * TILE) == 0
    depth = norm_w.shape[0]
    assert depth == 1

    half = A_HEAD_DIM // 2
    inv_freq = ROPE_THETA ** (-jnp.arange(half, dtype=jnp.float32) / half)
    inv_freq_row = jnp.tile(inv_freq, LANES // half)[None, :]

    x2 = x.reshape(bsz * t, d)
    for layer in range(depth):
        x2 = _block(x2, positions, norm_w[layer][None, :], inv_freq_row, w_in[layer].T, a_sinks[layer],
                    b_gate_up[layer], b_gate_bias[layer][None, :], b_out_norm_w[layer][None, :],
                    w_a_proj[layer], w_b_proj[layer], w_out[layer], final_norm_w[None, :])
    return x2.reshape(bsz, t, d)
```

```python
import functools

import jax
import jax.numpy as jnp
from jax import lax
from jax.experimental import pallas as pl
from jax.experimental.pallas import tpu as pltpu

D_MODEL = 1024
A_HEADS = 16
A_KV_HEADS = 2
A_HEAD_DIM = 64
A_GROUP = A_HEADS // A_KV_HEADS
A_WIDTH = A_HEADS * A_HEAD_DIM
A_KV_WIDTH = A_KV_HEADS * A_HEAD_DIM
WINDOW = 128
BLOCK = 128
ROPE_THETA = 10000.0

B_HEADS = 4
B_QK_WIDTH = D_MODEL // 2
B_V_WIDTH = D_MODEL
B_KEY_DIM = B_QK_WIDTH // B_HEADS
B_VAL_DIM = B_V_WIDTH // B_HEADS
GATE_RANK = 16
GATE_TAU = 16.0
CHUNK = 64

EPS = 1e-5
NEG_INF = -1e30
LOG2E = 1.4426950408889634

LANES = 128

COL_A_Q = 0
COL_A_GATE = 1024
COL_B_V = 2048
COL_B_GATE = 3072
COL_B_Q = 4096
COL_B_K = 4608
COL_A_K = 5120
COL_A_V = 5248
COL_B_LOW = 5376
MIX_COLS = 5504
COL_M_A = MIX_COLS
COL_M_B = MIX_COLS + D_MODEL
PROJ_WIDTH = MIX_COLS + 2 * D_MODEL

VMEM_LIMIT_BYTES = 60 * 1024 * 1024


IN_WIDTH = 2 * A_WIDTH + 2 * A_KV_WIDTH + 2 * B_QK_WIDTH + 2 * B_V_WIDTH + GATE_RANK + 2 * D_MODEL
_W_IN_PIECES = ((A_WIDTH, COL_A_Q), (A_KV_WIDTH, COL_A_K), (A_KV_WIDTH, COL_A_V), (A_WIDTH, COL_A_GATE),
                (B_QK_WIDTH, COL_B_Q), (B_QK_WIDTH, COL_B_K), (B_V_WIDTH, COL_B_V), (B_V_WIDTH, COL_B_GATE),
                (GATE_RANK, COL_B_LOW), (D_MODEL, COL_M_A), (D_MODEL, COL_M_B))
W_SLAB = 128
W_STAGE_BUFS = 12


def _stream_weights(w_in_hbm, wa_hbm, wb_hbm, wo_hbm, w_ref, wa_ref, wb_ref, wo_ref, stage_ref, sem_ref):
    jobs = []
    src = 0
    for width, dst in _W_IN_PIECES:
        for r in range(0, width, W_SLAB):
            jobs.append((w_in_hbm, src + r, True, w_ref, dst + r, min(W_SLAB, width - r)))
        src += width
    for hbm, dst_ref in ((wa_hbm, wa_ref), (wb_hbm, wb_ref), (wo_hbm, wo_ref)):
        for r in range(0, D_MODEL, W_SLAB):
            jobs.append((hbm, r, False, dst_ref, r, W_SLAB))

    def copy(j):
        hbm, row = jobs[j][0], jobs[j][1]
        slot = j % W_STAGE_BUFS
        return pltpu.make_async_copy(hbm.at[pl.ds(row, W_SLAB), :], stage_ref.at[slot], sem_ref.at[slot])

    for j in range(min(W_STAGE_BUFS, len(jobs))):
        copy(j).start()
    for j, (_, _, transpose, dst_ref, off, valid) in enumerate(jobs):
        copy(j).wait()
        slab = stage_ref[j % W_STAGE_BUFS]
        if transpose:
            slab = slab.T
            if valid < W_SLAB:
                lane = lax.broadcasted_iota(jnp.int32, slab.shape, 1)
                slab = jnp.where(lane < valid, slab, 0.0)
            dst_ref[:, off:off + W_SLAB] = slab.astype(dst_ref.dtype)
        else:
            dst_ref[off:off + W_SLAB, :] = slab.astype(dst_ref.dtype)
        if j + W_STAGE_BUFS < len(jobs):
            copy(j + W_STAGE_BUFS).start()


TILE = 256
IN_CHUNK = 512


def _sigmoid(x):
    return 0.5 * jnp.tanh(0.5 * x) + 0.5


def _silu(x):
    return x * _sigmoid(x)


def _rope_slab(xs, cos, sin_signed, first_half):
    partner = jnp.where(first_half, pltpu.roll(xs, LANES - A_HEAD_DIM // 2, 1),
                        pltpu.roll(xs, A_HEAD_DIM // 2, 1))
    return xs * cos + partner * sin_signed


def _project_tile(x_ref, pos_ref, batch_row, nw_ref, invf_ref, w_ref, mix_ref):
    x = x_ref[...]
    var = jnp.mean(x * x, axis=-1, keepdims=True)
    h = (x * lax.rsqrt(var + EPS) * nw_ref[...]).astype(jnp.bfloat16)
    yield 300

    half = A_HEAD_DIM // 2
    n_grp = LANES // half
    qr = x.shape[0] // n_grp
    lane = lax.broadcasted_iota(jnp.int32, (1, LANES), 1)
    grp = lane // half
    assert x.shape[0] % LANES == 0 and LANES % qr == 0
    prow = pos_ref[pl.ds(batch_row, 1), :].astype(jnp.float32)
    n_cols = x.shape[0] // LANES
    pos_t = jnp.concatenate([prow[:, k * LANES:(k + 1) * LANES] for k in range(n_cols)]
                            + [jnp.zeros((LANES - n_cols, LANES), jnp.float32)], axis=0).T

    def pos_col(j):
        first = j * qr
        return pos_t[first % LANES:first % LANES + qr, first // LANES:first // LANES + 1]

    pos_dense = pos_col(0)
    for j in range(1, n_grp):
        pos_dense = jnp.where(grp == j, pos_col(j), pos_dense)
    ang = pos_dense * invf_ref[...]
    cos_dense = jnp.cos(ang)
    sin_dense = jnp.sin(ang)

    def replicate(dense, j):
        only_j = jnp.where(grp == j, dense, 0.0)
        out = only_j
        for r in range(1, n_grp):
            out = out + pltpu.roll(only_j, r * half, 1)
        return out

    cos = jnp.concatenate([replicate(cos_dense, j) for j in range(n_grp)], axis=0)
    sin = jnp.concatenate([replicate(sin_dense, j) for j in range(n_grp)], axis=0)
    first_half = (lane % A_HEAD_DIM) < half
    sin_signed = jnp.where(first_half, -sin, sin)
    a_q_scale = (A_HEAD_DIM ** -0.5) * LOG2E
    b_q_scale = B_KEY_DIM ** -0.5
    yield 300

    for c0 in range(0, PROJ_WIDTH, IN_CHUNK):
        cw = min(IN_CHUNK, PROJ_WIDTH - c0)
        acc = jnp.dot(h, w_ref[:, c0:c0 + cw], preferred_element_type=jnp.float32)
        for s0 in range(0, cw, LANES):
            col = c0 + s0
            slab = acc[:, s0:s0 + LANES]
            if COL_A_Q <= col < COL_A_Q + A_WIDTH:
                slab = _rope_slab(slab, cos, sin_signed, first_half) * a_q_scale
            elif COL_A_K <= col < COL_A_K + A_KV_WIDTH:
                slab = _rope_slab(slab, cos, sin_signed, first_half)
            elif COL_B_Q <= col < COL_B_Q + B_QK_WIDTH:
                slab = slab * b_q_scale
            mix_ref[:, col:col + LANES] = slab.astype(mix_ref.dtype)
        yield cw


SWA_ROWS = 2 * BLOCK
SWA_COLS = 4 * 2 * BLOCK


def _swa_bias(sink_ref, g, first_block):
    row = lax.broadcasted_iota(jnp.int32, (SWA_ROWS, SWA_COLS), 0)
    col = lax.broadcasted_iota(jnp.int32, (SWA_ROWS, SWA_COLS), 1)
    qi, sub = row % BLOCK, row // BLOCK
    ki, hi = col % (2 * BLOCK), col // (2 * BLOCK)
    rel = qi + BLOCK - ki
    valid = (rel >= 0) & (rel < WINDOW)
    if first_block:
        valid = valid & (ki >= BLOCK)
    sink = jnp.zeros((SWA_ROWS, SWA_COLS), jnp.float32)
    for s in range(2):
        for i in range(4):
            sink = jnp.where((sub == s) & (hi == i), sink_ref[g * A_GROUP + s * 4 + i] * LOG2E, sink)
    return jnp.where(ki == 0, sink, jnp.where(valid, 0.0, NEG_INF))


def _swa_block(mix_ref, r0, k_prev, v_prev, first, bias_ref, oa_ref):
    bf16 = jnp.bfloat16
    rows = slice(r0, r0 + BLOCK)
    krow = lax.broadcasted_iota(jnp.int32, (2 * BLOCK, A_KV_WIDTH), 0)
    lane = lax.broadcasted_iota(jnp.int32, (2 * BLOCK, A_KV_WIDTH), 1)
    k = jnp.concatenate([k_prev, mix_ref[rows, COL_A_K:COL_A_K + A_KV_WIDTH]], axis=0)
    v = jnp.concatenate([v_prev, mix_ref[rows, COL_A_V:COL_A_V + A_KV_WIDTH]], axis=0)
    k = jnp.where(krow == 0, jnp.zeros_like(k), k)
    v = jnp.where(krow == 0, jnp.zeros_like(v), v)
    k_t = k.T
    v_swapped = pltpu.roll(v, A_HEAD_DIM, 1)
    lo = lane < A_HEAD_DIM
    ones_lo = jnp.where(lo, 1.0, 0.0).astype(bf16)
    ones_hi = jnp.where(lo, 0.0, 1.0).astype(bf16)
    zero_kt = jnp.zeros((A_HEAD_DIM, 2 * BLOCK), bf16)
    zero_v = jnp.zeros_like(v)

    for g in range(A_KV_HEADS):
        k_tg = k_t[g * A_HEAD_DIM:(g + 1) * A_HEAD_DIM, :]
        w_k = jnp.concatenate(
            [jnp.concatenate([k_tg if c == i else zero_kt for c in range(4)], axis=1) for i in range(4)],
            axis=0)
        c0 = COL_A_Q + g * A_GROUP * A_HEAD_DIM
        q_g = jnp.concatenate([mix_ref[rows, c0:c0 + 256], mix_ref[rows, c0 + 256:c0 + 512]], axis=0)
        s = jnp.dot(q_g, w_k, preferred_element_type=jnp.float32) + bias_ref[first, g]
        ps = []
        for i in range(4):
            s_i = s[:, i * 2 * BLOCK:(i + 1) * 2 * BLOCK]
            ps.append(jnp.exp2(s_i - jnp.max(s_i, axis=-1, keepdims=True)).astype(bf16))
        yield 550
        v_lo = jnp.where(lo, v if g == 0 else v_swapped, zero_v)
        v_hi = jnp.where(lo, zero_v, v_swapped if g == 0 else v)
        w_v = jnp.concatenate([jnp.concatenate([v_lo, ones_lo], axis=1),
                               jnp.concatenate([v_hi, ones_hi], axis=1)], axis=0)
        p = jnp.concatenate([jnp.concatenate([ps[0], ps[1]], axis=1),
                             jnp.concatenate([ps[2], ps[3]], axis=1)], axis=0)
        r = jnp.dot(p, w_v, preferred_element_type=jnp.float32)
        o_n = r[:, :LANES] / r[:, LANES:]
        for pair in range(2):
            for sub in range(2):
                off = g * A_GROUP * A_HEAD_DIM + sub * 256 + pair * LANES
                piece = o_n[pair * SWA_ROWS + sub * BLOCK:pair * SWA_ROWS + (sub + 1) * BLOCK, :]
                gate = mix_ref[rows, COL_A_GATE + off:COL_A_GATE + off + LANES].astype(jnp.float32)
                oa_ref[rows, off:off + LANES] = (piece * _silu(gate)).astype(oa_ref.dtype)
        yield 400


def _swa_tile(mix_ref, first_tile, kv_prev_ref, bias_ref, oa_ref):
    first = first_tile.astype(jnp.int32)
    k_prev = kv_prev_ref[:, :A_KV_WIDTH]
    v_prev = kv_prev_ref[:, A_KV_WIDTH:]
    for blk in range(TILE // BLOCK):
        r0 = blk * BLOCK
        yield from _swa_block(mix_ref, r0, k_prev, v_prev, first if blk == 0 else 0, bias_ref, oa_ref)
        k_prev = mix_ref[r0:r0 + BLOCK, COL_A_K:COL_A_K + A_KV_WIDTH]
        v_prev = mix_ref[r0:r0 + BLOCK, COL_A_V:COL_A_V + A_KV_WIDTH]
    kv_prev_ref[:, :A_KV_WIDTH] = k_prev
    kv_prev_ref[:, A_KV_WIDTH:] = v_prev


def _chunk_tril():
    ti = lax.broadcasted_iota(jnp.int32, (TILE, TILE), 0)
    tj = lax.broadcasted_iota(jnp.int32, (TILE, TILE), 1)
    return jnp.where((ti // CHUNK == tj // CHUNK) & (ti >= tj), 1.0, 0.0).astype(jnp.bfloat16)


def _chunk_cumsum_rows(x, tril_ref):
    hi = x.astype(jnp.bfloat16)
    lo = (x - hi.astype(jnp.float32)).astype(jnp.bfloat16)
    both = jnp.dot(tril_ref[...], jnp.concatenate([hi, lo], axis=1), preferred_element_type=jnp.float32)
    return both[:, :x.shape[1]] + both[:, x.shape[1]:]


def _gla_tile(mix_ref, first_tile, up_ref, bias_ref, nw_ref, ob_ref,
              state_ref, qe_ref, ke_ref, ks_ref, dcol_ref, inc_ref, oacc_ref, tril_ref):
    bf16 = jnp.bfloat16
    n_chunks = TILE // CHUNK

    gk = jnp.dot(mix_ref[:, COL_B_LOW:COL_B_LOW + LANES], up_ref[...],
                 preferred_element_type=jnp.float32) + bias_ref[...]
    log_a = (jnp.minimum(gk, 0.0) - jnp.log(1.0 + jnp.exp(-jnp.abs(gk)))) * (1.0 / GATE_TAU)
    b = _chunk_cumsum_rows(log_a, tril_ref)
    yield 700
    b_last_rows = [b[(c + 1) * CHUNK - 1:(c + 1) * CHUNK, :] for c in range(n_chunks)]
    b_last = jnp.concatenate([jnp.broadcast_to(r, (CHUNK, B_QK_WIDTH)) for r in b_last_rows], axis=0)
    qf = mix_ref[:, COL_B_Q:COL_B_Q + B_QK_WIDTH].astype(jnp.float32)
    kf = mix_ref[:, COL_B_K:COL_B_K + B_QK_WIDTH].astype(jnp.float32)
    qe_ref[...] = (qf * jnp.exp(b)).astype(bf16)
    ke_ref[...] = (kf * jnp.exp(-b)).astype(bf16)
    ks_ref[...] = (kf * jnp.exp(b_last - b)).astype(bf16)
    pad = jnp.zeros((LANES - n_chunks, B_QK_WIDTH), jnp.float32)
    dcol_ref[...] = jnp.exp(jnp.concatenate(b_last_rows + [pad], axis=0)).T
    yield 700

    ti = lax.broadcasted_iota(jnp.int32, (TILE, TILE), 0)
    tj = lax.broadcasted_iota(jnp.int32, (TILE, TILE), 1)
    keep = (ti // CHUNK == tj // CHUNK) & (ti >= tj)
    for h in range(B_HEADS):
        ks = slice(h * B_KEY_DIM, (h + 1) * B_KEY_DIM)
        vs = slice(COL_B_V + h * B_VAL_DIM, COL_B_V + (h + 1) * B_VAL_DIM)
        os_ = slice(h * B_VAL_DIM, (h + 1) * B_VAL_DIM)
        att = lax.dot_general(qe_ref[:, ks], ke_ref[:, ks], (((1,), (1,)), ((), ())),
                              preferred_element_type=jnp.float32)
        att = jnp.where(keep, att, 0.0).astype(bf16)
        oacc_ref[:, os_] = jnp.dot(att, mix_ref[:, vs], preferred_element_type=jnp.float32)
        for c in range(n_chunks):
            rows = slice(c * CHUNK, (c + 1) * CHUNK)
            inc_ref[c, h] = lax.dot_general(ks_ref[rows, ks], mix_ref[rows, vs], (((0,), (0,)), ((), ())),
                                            preferred_element_type=jnp.float32)
        yield 350

    for h in range(B_HEADS):
        ks = slice(h * B_KEY_DIM, (h + 1) * B_KEY_DIM)
        os_ = slice(h * B_VAL_DIM, (h + 1) * B_VAL_DIM)
        st = jnp.where(first_tile, 0.0, state_ref[h])
        for c in range(n_chunks):
            rows = slice(c * CHUNK, (c + 1) * CHUNK)
            oacc_ref[rows, os_] += jnp.dot(qe_ref[rows, ks], st.astype(bf16),
                                           preferred_element_type=jnp.float32)
            st = dcol_ref[ks, c:c + 1] * st + inc_ref[c, h]
        state_ref[h] = st
        yield 150

    for h in range(B_HEADS):
        os_ = slice(h * B_VAL_DIM, (h + 1) * B_VAL_DIM)
        o_h = oacc_ref[:, os_]
        var = jnp.mean(o_h * o_h, axis=-1, keepdims=True)
        o_n = o_h * lax.rsqrt(var + EPS) * nw_ref[...]
        gate = mix_ref[:, COL_B_GATE + h * B_VAL_DIM:COL_B_GATE + (h + 1) * B_VAL_DIM].astype(jnp.float32)
        ob_ref[:, os_] = (o_n * _silu(gate)).astype(ob_ref.dtype)
        yield 225


def _emit_interleaved(streams, weights):
    spent = [0.0] * len(streams)
    live = list(range(len(streams)))
    while live:
        k = min(live, key=lambda j: spent[j])
        try:
            spent[k] += next(streams[k]) * weights[k]
        except StopIteration:
            live.remove(k)


def _branch_proj(o_ref, w_ref):
    return jnp.dot(o_ref[...], w_ref[...], preferred_element_type=jnp.float32)


def _merge_out_tile(mix_ref, y_a, y_b, xlag_ref, wo_ref, fw_ref, out_ref):
    m_a = mix_ref[:, COL_M_A:COL_M_A + D_MODEL].astype(jnp.float32)
    m_b = mix_ref[:, COL_M_B:COL_M_B + D_MODEL].astype(jnp.float32)
    merged = (_sigmoid(m_a) * y_a + _sigmoid(m_b) * y_b).astype(jnp.bfloat16)
    yield 500
    z = xlag_ref[...] + jnp.dot(merged, wo_ref[...], preferred_element_type=jnp.float32)
    var = jnp.mean(z * z, axis=-1, keepdims=True)
    out_ref[...] = z * lax.rsqrt(var + EPS) * fw_ref[...]
    yield 1100


PROJECT_COST = 600 + PROJ_WIDTH
FINISH_COST = 3800 + 4100 + 1200 + 1600


def _block_kernel(tiles_per_seq, sink_ref, x_ref, xlag_ref, pos_ref, nw_ref, invf_ref, w_in_hbm, up_f32_ref,
                  gbias_ref, bnw_ref, wa_hbm, wb_hbm, wo_hbm, fw_ref, out_ref,
                  w_ref, wa_ref, wb_ref, wo_ref, stage_ref, wsem_ref, up_ref,
                  mix_a, mix_b, oa_ref, ob_ref, kv_prev_ref, sbias_ref,
                  state_ref, qe_ref, ke_ref, ks_ref, dcol_ref, inc_ref, oacc_ref, tril_ref):
    i = pl.program_id(0)
    n = pl.num_programs(0) - 1

    def project(dst):
        yield from _project_tile(x_ref, pos_ref, i // tiles_per_seq, nw_ref, invf_ref, w_ref, dst)

    def finish(src):
        first_tile = ((i - 1) % tiles_per_seq) == 0
        yield from _swa_tile(src, first_tile, kv_prev_ref, sbias_ref, oa_ref)
        y_a = _branch_proj(oa_ref, wa_ref)
        yield 600
        yield from _gla_tile(src, first_tile, up_ref, gbias_ref, bnw_ref, ob_ref,
                             state_ref, qe_ref, ke_ref, ks_ref, dcol_ref, inc_ref, oacc_ref, tril_ref)
        y_b = _branch_proj(ob_ref, wb_ref)
        yield 600
        yield from _merge_out_tile(src, y_a, y_b, xlag_ref, wo_ref, fw_ref, out_ref)

    both = (1.0, FINISH_COST / PROJECT_COST)

    @pl.when(i == 0)
    def _():
        for g in range(A_KV_HEADS):
            sbias_ref[0, g] = _swa_bias(sink_ref, g, False)
            sbias_ref[1, g] = _swa_bias(sink_ref, g, True)
        up_ref[...] = jnp.zeros_like(up_ref)
        up_ref[0:GATE_RANK, :] = up_f32_ref[...].astype(up_ref.dtype)
        kv_prev_ref[...] = jnp.zeros_like(kv_prev_ref)
        state_ref[...] = jnp.zeros_like(state_ref)
        tril_ref[...] = _chunk_tril()
        _stream_weights(w_in_hbm, wa_hbm, wb_hbm, wo_hbm, w_ref, wa_ref, wb_ref, wo_ref, stage_ref, wsem_ref)
        _emit_interleaved([project(mix_a)], (1.0,))

    @pl.when((i > 0) & (i < n) & (i % 2 == 1))
    def _():
        _emit_interleaved([finish(mix_a), project(mix_b)], both)

    @pl.when((i > 0) & (i < n) & (i % 2 == 0))
    def _():
        _emit_interleaved([finish(mix_b), project(mix_a)], both)

    @pl.when(i == n)
    def _():
        _emit_interleaved([finish(mix_b)], (1.0,))


def _block(x2, positions, norm_w, inv_freq_row, w_in_t, sinks, gate_up, gate_bias, b_norm_w, wa, wb, wo,
           final_w):
    bsz, t = positions.shape
    tiles_per_seq = t // TILE
    m = x2.shape[0]
    n = m // TILE
    assert w_in_t.shape == (IN_WIDTH, D_MODEL) and w_in_t.dtype == jnp.float32
    bf16 = jnp.bfloat16
    cur = lambda i: (jnp.minimum(i, n - 1), 0)
    lag = lambda i: (jnp.maximum(i - 1, 0), 0)
    const = lambda i: (0, 0)
    n_chunks = TILE // CHUNK
    return pl.pallas_call(
        functools.partial(_block_kernel, tiles_per_seq),
        grid=(n + 1,),
        in_specs=[
            pl.BlockSpec(memory_space=pltpu.SMEM),
            pl.BlockSpec((TILE, D_MODEL), cur),
            pl.BlockSpec((TILE, D_MODEL), lag),
            pl.BlockSpec((bsz, TILE), lambda i: (0, jnp.minimum(i, n - 1) % tiles_per_seq)),
            pl.BlockSpec((1, D_MODEL), const),
            pl.BlockSpec((1, LANES), const),
            pl.BlockSpec(memory_space=pl.ANY),
            pl.BlockSpec((GATE_RANK, B_QK_WIDTH), const),
            pl.BlockSpec((1, B_QK_WIDTH), const),
            pl.BlockSpec((1, B_VAL_DIM), const),
            pl.BlockSpec(memory_space=pl.ANY),
            pl.BlockSpec(memory_space=pl.ANY),
            pl.BlockSpec(memory_space=pl.ANY),
            pl.BlockSpec((1, D_MODEL), const),
        ],
        out_specs=pl.BlockSpec((TILE, D_MODEL), lag),
        out_shape=jax.ShapeDtypeStruct((m, D_MODEL), jnp.float32),
        scratch_shapes=[
            pltpu.VMEM((D_MODEL, PROJ_WIDTH), bf16),
            pltpu.VMEM((A_WIDTH, D_MODEL), bf16),
            pltpu.VMEM((B_V_WIDTH, D_MODEL), bf16),
            pltpu.VMEM((D_MODEL, D_MODEL), bf16),
            pltpu.VMEM((W_STAGE_BUFS, W_SLAB, D_MODEL), jnp.float32),
            pltpu.SemaphoreType.DMA((W_STAGE_BUFS,)),
            pltpu.VMEM((LANES, B_QK_WIDTH), bf16),
            pltpu.VMEM((TILE, PROJ_WIDTH), bf16),
            pltpu.VMEM((TILE, PROJ_WIDTH), bf16),
            pltpu.VMEM((TILE, A_WIDTH), bf16),
            pltpu.VMEM((TILE, B_V_WIDTH), bf16),
            pltpu.VMEM((BLOCK, 2 * A_KV_WIDTH), bf16),
            pltpu.VMEM((2, A_KV_HEADS, SWA_ROWS, SWA_COLS), jnp.float32),
            pltpu.VMEM((B_HEADS, B_KEY_DIM, B_VAL_DIM), jnp.float32),
            pltpu.VMEM((TILE, B_QK_WIDTH), bf16),
            pltpu.VMEM((TILE, B_QK_WIDTH), bf16),
            pltpu.VMEM((TILE, B_QK_WIDTH), bf16),
            pltpu.VMEM((B_QK_WIDTH, LANES), jnp.float32),
            pltpu.VMEM((n_chunks, B_HEADS, B_KEY_DIM, B_VAL_DIM), jnp.float32),
            pltpu.VMEM((TILE, B_V_WIDTH), jnp.float32),
            pltpu.VMEM((TILE, TILE), bf16),
        ],
        compiler_params=pltpu.CompilerParams(dimension_semantics=("arbitrary",),
                                             vmem_limit_bytes=VMEM_LIMIT_BYTES),
        name="block",
    )(sinks, x2, x2, positions, norm_w, inv_freq_row, w_in_t, gate_up, gate_bias, b_norm_w, wa, wb, wo,
      final_w)


def kernel(x, positions, norm_w, w_in, a_sinks, b_gate_up, b_gate_bias, b_out_norm_w,
           w_a_proj, w_b_proj, w_out, final_norm_w):
    bsz, t, d = x.shape
    assert d == D_MODEL and t % (2 * TILE) == 0
    depth = norm_w.shape[0]
    assert depth == 1

    half = A_HEAD_DIM // 2
    inv_freq = ROPE_THETA ** (-jnp.arange(half, dtype=jnp.float32) / half)
    inv_freq_row = jnp.tile(inv_freq, LANES // half)[None, :]

    x2 = x.reshape(bsz * t, d)
    for layer in range(depth):
        x2 = _block(x2, positions, norm_w[layer][None, :], inv_freq_row, w_in[layer].T, a_sinks[layer],
                    b_gate_up[layer], b_gate_bias[layer][None, :], b_out_norm_w[layer][None, :],
                    w_a_proj[layer], w_b_proj[layer], w_out[layer], final_norm_w[None, :])
    return x2.reshape(bsz, t, d)
```

```python
import functools

import jax
import jax.numpy as jnp
from jax import lax
from jax.experimental import pallas as pl
from jax.experimental.pallas import tpu as pltpu

D_MODEL = 1024
A_HEADS = 16
A_KV_HEADS = 2
A_HEAD_DIM = 64
A_GROUP = A_HEADS // A_KV_HEADS
A_WIDTH = A_HEADS * A_HEAD_DIM
A_KV_WIDTH = A_KV_HEADS * A_HEAD_DIM
WINDOW = 128
BLOCK = 128
ROPE_THETA = 10000.0

B_HEADS = 4
B_QK_WIDTH = D_MODEL // 2
B_V_WIDTH = D_MODEL
B_KEY_DIM = B_QK_WIDTH // B_HEADS
B_VAL_DIM = B_V_WIDTH // B_HEADS
GATE_RANK = 16
GATE_TAU = 16.0
CHUNK = 64

EPS = 1e-5
NEG_INF = -1e30
LOG2E = 1.4426950408889634

LANES = 128

COL_A_Q = 0
COL_A_GATE = 1024
COL_B_V = 2048
COL_B_GATE = 3072
COL_B_Q = 4096
COL_B_K = 4608
COL_A_K = 5120
COL_A_V = 5248
COL_B_LOW = 5376
MIX_COLS = 5504
COL_M_A = MIX_COLS
COL_M_B = MIX_COLS + D_MODEL
PROJ_WIDTH = MIX_COLS + 2 * D_MODEL

VMEM_LIMIT_BYTES = 60 * 1024 * 1024


IN_WIDTH = 2 * A_WIDTH + 2 * A_KV_WIDTH + 2 * B_QK_WIDTH + 2 * B_V_WIDTH + GATE_RANK + 2 * D_MODEL
_W_IN_PIECES = ((A_WIDTH, COL_A_Q), (A_KV_WIDTH, COL_A_K), (A_KV_WIDTH, COL_A_V), (A_WIDTH, COL_A_GATE),
                (B_QK_WIDTH, COL_B_Q), (B_QK_WIDTH, COL_B_K), (B_V_WIDTH, COL_B_V), (B_V_WIDTH, COL_B_GATE),
                (GATE_RANK, COL_B_LOW), (D_MODEL, COL_M_A), (D_MODEL, COL_M_B))
W_SLAB = 128
W_STAGE_BUFS = 12


def _stream_weights(w_in_hbm, wa_hbm, wb_hbm, wo_hbm, w_ref, wa_ref, wb_ref, wo_ref, stage_ref, sem_ref):
    jobs = []
    src = 0
    for width, dst in _W_IN_PIECES:
        for r in range(0, width, W_SLAB):
            jobs.append((w_in_hbm, src + r, True, w_ref, dst + r, min(W_SLAB, width - r)))
        src += width
    for hbm, dst_ref in ((wa_hbm, wa_ref), (wb_hbm, wb_ref), (wo_hbm, wo_ref)):
        for r in range(0, D_MODEL, W_SLAB):
            jobs.append((hbm, r, False, dst_ref, r, W_SLAB))

    def copy(j):
        hbm, row = jobs[j][0], jobs[j][1]
        slot = j % W_STAGE_BUFS
        return pltpu.make_async_copy(hbm.at[pl.ds(row, W_SLAB), :], stage_ref.at[slot], sem_ref.at[slot])

    for j in range(min(W_STAGE_BUFS, len(jobs))):
        copy(j).start()
    for j, (_, _, transpose, dst_ref, off, valid) in enumerate(jobs):
        copy(j).wait()
        slab = stage_ref[j % W_STAGE_BUFS].astype(dst_ref.dtype)
        if transpose:
            slab = slab.T
            if valid < W_SLAB:
                lane = lax.broadcasted_iota(jnp.int32, slab.shape, 1)
                slab = jnp.where(lane < valid, slab, jnp.zeros_like(slab))
            dst_ref[:, off:off + W_SLAB] = slab
        else:
            dst_ref[off:off + W_SLAB, :] = slab
        if j + W_STAGE_BUFS < len(jobs):
            copy(j + W_STAGE_BUFS).start()


TILE = 256
IN_CHUNK = 512


def _sigmoid(x):
    return 0.5 * jnp.tanh(0.5 * x) + 0.5


def _silu(x):
    return x * _sigmoid(x)


def _rope_slab(xs, cos, sin_signed, first_half):
    partner = jnp.where(first_half, pltpu.roll(xs, LANES - A_HEAD_DIM // 2, 1),
                        pltpu.roll(xs, A_HEAD_DIM // 2, 1))
    return xs * cos + partner * sin_signed


def _project_tile(x_ref, pos_ref, batch_row, nw_ref, invf_ref, w_ref, mix_ref):
    x = x_ref[...]
    var = jnp.mean(x * x, axis=-1, keepdims=True)
    h = (x * lax.rsqrt(var + EPS) * nw_ref[...]).astype(jnp.bfloat16)
    yield 300

    half = A_HEAD_DIM // 2
    n_grp = LANES // half
    qr = x.shape[0] // n_grp
    lane = lax.broadcasted_iota(jnp.int32, (1, LANES), 1)
    grp = lane // half
    assert x.shape[0] % LANES == 0 and LANES % qr == 0
    prow = pos_ref[pl.ds(batch_row, 1), :].astype(jnp.float32)
    n_cols = x.shape[0] // LANES
    pos_t = jnp.concatenate([prow[:, k * LANES:(k + 1) * LANES] for k in range(n_cols)]
                            + [jnp.zeros((LANES - n_cols, LANES), jnp.float32)], axis=0).T

    def pos_col(j):
        first = j * qr
        return pos_t[first % LANES:first % LANES + qr, first // LANES:first // LANES + 1]

    pos_dense = pos_col(0)
    for j in range(1, n_grp):
        pos_dense = jnp.where(grp == j, pos_col(j), pos_dense)
    ang = pos_dense * invf_ref[...]
    cos_dense = jnp.cos(ang)
    sin_dense = jnp.sin(ang)

    def replicate(dense, j):
        only_j = jnp.where(grp == j, dense, 0.0)
        out = only_j
        for r in range(1, n_grp):
            out = out + pltpu.roll(only_j, r * half, 1)
        return out

    cos = jnp.concatenate([replicate(cos_dense, j) for j in range(n_grp)], axis=0)
    sin = jnp.concatenate([replicate(sin_dense, j) for j in range(n_grp)], axis=0)
    first_half = (lane % A_HEAD_DIM) < half
    sin_signed = jnp.where(first_half, -sin, sin)
    a_q_scale = (A_HEAD_DIM ** -0.5) * LOG2E
    b_q_scale = B_KEY_DIM ** -0.5
    yield 300

    for c0 in range(0, PROJ_WIDTH, IN_CHUNK):
        cw = min(IN_CHUNK, PROJ_WIDTH - c0)
        acc = jnp.dot(h, w_ref[:, c0:c0 + cw], preferred_element_type=jnp.float32)
        for s0 in range(0, cw, LANES):
            col = c0 + s0
            slab = acc[:, s0:s0 + LANES]
            if COL_A_Q <= col < COL_A_Q + A_WIDTH:
                slab = _rope_slab(slab, cos, sin_signed, first_half) * a_q_scale
            elif COL_A_K <= col < COL_A_K + A_KV_WIDTH:
                slab = _rope_slab(slab, cos, sin_signed, first_half)
            elif COL_B_Q <= col < COL_B_Q + B_QK_WIDTH:
                slab = slab * b_q_scale
            mix_ref[:, col:col + LANES] = slab.astype(mix_ref.dtype)
        yield cw


SWA_ROWS = 2 * BLOCK
SWA_COLS = 4 * 2 * BLOCK


def _swa_bias(sink_ref, g, first_block):
    row = lax.broadcasted_iota(jnp.int32, (SWA_ROWS, SWA_COLS), 0)
    col = lax.broadcasted_iota(jnp.int32, (SWA_ROWS, SWA_COLS), 1)
    qi, sub = row % BLOCK, row // BLOCK
    ki, hi = col % (2 * BLOCK), col // (2 * BLOCK)
    rel = qi + BLOCK - ki
    valid = (rel >= 0) & (rel < WINDOW)
    if first_block:
        valid = valid & (ki >= BLOCK)
    sink = jnp.zeros((SWA_ROWS, SWA_COLS), jnp.float32)
    for s in range(2):
        for i in range(4):
            sink = jnp.where((sub == s) & (hi == i), sink_ref[g * A_GROUP + s * 4 + i] * LOG2E, sink)
    return jnp.where(ki == 0, sink, jnp.where(valid, 0.0, NEG_INF))


def _swa_block(mix_ref, r0, k_prev, v_prev, first, bias_ref, oa_ref):
    bf16 = jnp.bfloat16
    rows = slice(r0, r0 + BLOCK)
    krow = lax.broadcasted_iota(jnp.int32, (2 * BLOCK, A_KV_WIDTH), 0)
    lane = lax.broadcasted_iota(jnp.int32, (2 * BLOCK, A_KV_WIDTH), 1)
    k = jnp.concatenate([k_prev, mix_ref[rows, COL_A_K:COL_A_K + A_KV_WIDTH]], axis=0)
    v = jnp.concatenate([v_prev, mix_ref[rows, COL_A_V:COL_A_V + A_KV_WIDTH]], axis=0)
    k = jnp.where(krow == 0, jnp.zeros_like(k), k)
    v = jnp.where(krow == 0, jnp.zeros_like(v), v)
    k_t = k.T
    v_swapped = pltpu.roll(v, A_HEAD_DIM, 1)
    lo = lane < A_HEAD_DIM
    ones_lo = jnp.where(lo, 1.0, 0.0).astype(bf16)
    ones_hi = jnp.where(lo, 0.0, 1.0).astype(bf16)
    zero_kt = jnp.zeros((A_HEAD_DIM, 2 * BLOCK), bf16)
    zero_v = jnp.zeros_like(v)

    for g in range(A_KV_HEADS):
        k_tg = k_t[g * A_HEAD_DIM:(g + 1) * A_HEAD_DIM, :]
        w_k = jnp.concatenate(
            [jnp.concatenate([k_tg if c == i else zero_kt for c in range(4)], axis=1) for i in range(4)],
            axis=0)
        c0 = COL_A_Q + g * A_GROUP * A_HEAD_DIM
        q_g = jnp.concatenate([mix_ref[rows, c0:c0 + 256], mix_ref[rows, c0 + 256:c0 + 512]], axis=0)
        s = jnp.dot(q_g, w_k, preferred_element_type=jnp.float32) + bias_ref[first, g]
        ps = []
        for i in range(4):
            s_i = s[:, i * 2 * BLOCK:(i + 1) * 2 * BLOCK]
            ps.append(jnp.exp2(s_i - jnp.max(s_i, axis=-1, keepdims=True)).astype(bf16))
        yield 550
        v_lo = jnp.where(lo, v if g == 0 else v_swapped, zero_v)
        v_hi = jnp.where(lo, zero_v, v_swapped if g == 0 else v)
        w_v = jnp.concatenate([jnp.concatenate([v_lo, ones_lo], axis=1),
                               jnp.concatenate([v_hi, ones_hi], axis=1)], axis=0)
        p = jnp.concatenate([jnp.concatenate([ps[0], ps[1]], axis=1),
                             jnp.concatenate([ps[2], ps[3]], axis=1)], axis=0)
        r = jnp.dot(p, w_v, preferred_element_type=jnp.float32)
        o_n = r[:, :LANES] / r[:, LANES:]
        for pair in range(2):
            for sub in range(2):
                off = g * A_GROUP * A_HEAD_DIM + sub * 256 + pair * LANES
                piece = o_n[pair * SWA_ROWS + sub * BLOCK:pair * SWA_ROWS + (sub + 1) * BLOCK, :]
                gate = mix_ref[rows, COL_A_GATE + off:COL_A_GATE + off + LANES].astype(jnp.float32)
                oa_ref[rows, off:off + LANES] = (piece * _silu(gate)).astype(oa_ref.dtype)
        yield 400


def _swa_tile(mix_ref, first_tile, kv_prev_ref, bias_ref, oa_ref):
    first = first_tile.astype(jnp.int32)
    k_prev = kv_prev_ref[:, :A_KV_WIDTH]
    v_prev = kv_prev_ref[:, A_KV_WIDTH:]
    for blk in range(TILE // BLOCK):
        r0 = blk * BLOCK
        yield from _swa_block(mix_ref, r0, k_prev, v_prev, first if blk == 0 else 0, bias_ref, oa_ref)
        k_prev = mix_ref[r0:r0 + BLOCK, COL_A_K:COL_A_K + A_KV_WIDTH]
        v_prev = mix_ref[r0:r0 + BLOCK, COL_A_V:COL_A_V + A_KV_WIDTH]
    kv_prev_ref[:, :A_KV_WIDTH] = k_prev
    kv_prev_ref[:, A_KV_WIDTH:] = v_prev


def _chunk_tril():
    ti = lax.broadcasted_iota(jnp.int32, (TILE, TILE), 0)
    tj = lax.broadcasted_iota(jnp.int32, (TILE, TILE), 1)
    return jnp.where((ti // CHUNK == tj // CHUNK) & (ti >= tj), 1.0, 0.0).astype(jnp.bfloat16)


def _chunk_cumsum_rows(x, tril_ref):
    hi = x.astype(jnp.bfloat16)
    lo = (x - hi.astype(jnp.float32)).astype(jnp.bfloat16)
    both = jnp.dot(tril_ref[...], jnp.concatenate([hi, lo], axis=1), preferred_element_type=jnp.float32)
    return both[:, :x.shape[1]] + both[:, x.shape[1]:]


def _gla_tile(mix_ref, first_tile, up_ref, bias_ref, nw_ref, ob_ref,
              state_ref, qe_ref, ke_ref, ks_ref, dcol_ref, inc_ref, oacc_ref, tril_ref):
    bf16 = jnp.bfloat16
    n_chunks = TILE // CHUNK

    gk = jnp.dot(mix_ref[:, COL_B_LOW:COL_B_LOW + LANES], up_ref[...],
                 preferred_element_type=jnp.float32) + bias_ref[...]
    log_a = (jnp.minimum(gk, 0.0) - jnp.log(1.0 + jnp.exp(-jnp.abs(gk)))) * (1.0 / GATE_TAU)
    b = _chunk_cumsum_rows(log_a, tril_ref)
    yield 700
    b_last_rows = [b[(c + 1) * CHUNK - 1:(c + 1) * CHUNK, :] for c in range(n_chunks)]
    b_last = jnp.concatenate([jnp.broadcast_to(r, (CHUNK, B_QK_WIDTH)) for r in b_last_rows], axis=0)
    qf = mix_ref[:, COL_B_Q:COL_B_Q + B_QK_WIDTH].astype(jnp.float32)
    kf = mix_ref[:, COL_B_K:COL_B_K + B_QK_WIDTH].astype(jnp.float32)
    qe_ref[...] = (qf * jnp.exp(b)).astype(bf16)
    ke_ref[...] = (kf * jnp.exp(-b)).astype(bf16)
    ks_ref[...] = (kf * jnp.exp(b_last - b)).astype(bf16)
    pad = jnp.zeros((LANES - n_chunks, B_QK_WIDTH), jnp.float32)
    dcol_ref[...] = jnp.exp(jnp.concatenate(b_last_rows + [pad], axis=0)).T
    yield 700

    ti = lax.broadcasted_iota(jnp.int32, (TILE, TILE), 0)
    tj = lax.broadcasted_iota(jnp.int32, (TILE, TILE), 1)
    keep = (ti // CHUNK == tj // CHUNK) & (ti >= tj)
    for h in range(B_HEADS):
        ks = slice(h * B_KEY_DIM, (h + 1) * B_KEY_DIM)
        vs = slice(COL_B_V + h * B_VAL_DIM, COL_B_V + (h + 1) * B_VAL_DIM)
        os_ = slice(h * B_VAL_DIM, (h + 1) * B_VAL_DIM)
        att = lax.dot_general(qe_ref[:, ks], ke_ref[:, ks], (((1,), (1,)), ((), ())),
                              preferred_element_type=jnp.float32)
        att = jnp.where(keep, att, 0.0).astype(bf16)
        oacc_ref[:, os_] = jnp.dot(att, mix_ref[:, vs], preferred_element_type=jnp.float32)
        for c in range(n_chunks):
            rows = slice(c * CHUNK, (c + 1) * CHUNK)
            inc_ref[c, h] = lax.dot_general(ks_ref[rows, ks], mix_ref[rows, vs], (((0,), (0,)), ((), ())),
                                            preferred_element_type=jnp.float32)
        yield 350

    for h in range(B_HEADS):
        ks = slice(h * B_KEY_DIM, (h + 1) * B_KEY_DIM)
        os_ = slice(h * B_VAL_DIM, (h + 1) * B_VAL_DIM)
        st = jnp.where(first_tile, 0.0, state_ref[h])
        for c in range(n_chunks):
            rows = slice(c * CHUNK, (c + 1) * CHUNK)
            oacc_ref[rows, os_] += jnp.dot(qe_ref[rows, ks], st.astype(bf16),
                                           preferred_element_type=jnp.float32)
            st = dcol_ref[ks, c:c + 1] * st + inc_ref[c, h]
        state_ref[h] = st
        yield 150

    for h in range(B_HEADS):
        os_ = slice(h * B_VAL_DIM, (h + 1) * B_VAL_DIM)
        o_h = oacc_ref[:, os_]
        var = jnp.mean(o_h * o_h, axis=-1, keepdims=True)
        o_n = o_h * lax.rsqrt(var + EPS) * nw_ref[...]
        gate = mix_ref[:, COL_B_GATE + h * B_VAL_DIM:COL_B_GATE + (h + 1) * B_VAL_DIM].astype(jnp.float32)
        ob_ref[:, os_] = (o_n * _silu(gate)).astype(ob_ref.dtype)
        yield 225


def _emit_interleaved(streams, weights):
    spent = [0.0] * len(streams)
    live = list(range(len(streams)))
    while live:
        k = min(live, key=lambda j: spent[j])
        try:
            spent[k] += next(streams[k]) * weights[k]
        except StopIteration:
            live.remove(k)


def _branch_proj(o_ref, w_ref):
    return jnp.dot(o_ref[...], w_ref[...], preferred_element_type=jnp.float32)


def _merge_out_tile(mix_ref, y_a, y_b, xlag_ref, wo_ref, fw_ref, out_ref):
    m_a = mix_ref[:, COL_M_A:COL_M_A + D_MODEL].astype(jnp.float32)
    m_b = mix_ref[:, COL_M_B:COL_M_B + D_MODEL].astype(jnp.float32)
    merged = (_sigmoid(m_a) * y_a + _sigmoid(m_b) * y_b).astype(jnp.bfloat16)
    yield 500
    z = xlag_ref[...] + jnp.dot(merged, wo_ref[...], preferred_element_type=jnp.float32)
    var = jnp.mean(z * z, axis=-1, keepdims=True)
    out_ref[...] = z * lax.rsqrt(var + EPS) * fw_ref[...]
    yield 1100


PROJECT_COST = 600 + PROJ_WIDTH
FINISH_COST = 3800 + 4100 + 1200 + 1600


def _block_kernel(tiles_per_seq, sink_ref, x_ref, xlag_ref, pos_ref, nw_ref, invf_ref, w_in_hbm, up_f32_ref,
                  gbias_ref, bnw_ref, wa_hbm, wb_hbm, wo_hbm, fw_ref, out_ref,
                  w_ref, wa_ref, wb_ref, wo_ref, stage_ref, wsem_ref, up_ref,
                  mix_a, mix_b, oa_ref, ob_ref, kv_prev_ref, sbias_ref,
                  state_ref, qe_ref, ke_ref, ks_ref, dcol_ref, inc_ref, oacc_ref, tril_ref):
    i = pl.program_id(0)
    n = pl.num_programs(0) - 1

    def project(dst):
        yield from _project_tile(x_ref, pos_ref, i // tiles_per_seq, nw_ref, invf_ref, w_ref, dst)

    def finish(src):
        first_tile = ((i - 1) % tiles_per_seq) == 0
        yield from _swa_tile(src, first_tile, kv_prev_ref, sbias_ref, oa_ref)
        y_a = _branch_proj(oa_ref, wa_ref)
        yield 600
        yield from _gla_tile(src, first_tile, up_ref, gbias_ref, bnw_ref, ob_ref,
                             state_ref, qe_ref, ke_ref, ks_ref, dcol_ref, inc_ref, oacc_ref, tril_ref)
        y_b = _branch_proj(ob_ref, wb_ref)
        yield 600
        yield from _merge_out_tile(src, y_a, y_b, xlag_ref, wo_ref, fw_ref, out_ref)

    both = (1.0, FINISH_COST / PROJECT_COST)

    @pl.when(i == 0)
    def _():
        for g in range(A_KV_HEADS):
            sbias_ref[0, g] = _swa_bias(sink_ref, g, False)
            sbias_ref[1, g] = _swa_bias(sink_ref, g, True)
        up_ref[...] = jnp.zeros_like(up_ref)
        up_ref[0:GATE_RANK, :] = up_f32_ref[...].astype(up_ref.dtype)
        kv_prev_ref[...] = jnp.zeros_like(kv_prev_ref)
        state_ref[...] = jnp.zeros_like(state_ref)
        tril_ref[...] = _chunk_tril()
        _stream_weights(w_in_hbm, wa_hbm, wb_hbm, wo_hbm, w_ref, wa_ref, wb_ref, wo_ref, stage_ref, wsem_ref)
        _emit_interleaved([project(mix_a)], (1.0,))

    @pl.when((i > 0) & (i < n) & (i % 2 == 1))
    def _():
        _emit_interleaved([finish(mix_a), project(mix_b)], both)

    @pl.when((i > 0) & (i < n) & (i % 2 == 0))
    def _():
        _emit_interleaved([finish(mix_b), project(mix_a)], both)

    @pl.when(i == n)
    def _():
        _emit_interleaved([finish(mix_b)], (1.0,))


def _block(x2, positions, norm_w, inv_freq_row, w_in_t, sinks, gate_up, gate_bias, b_norm_w, wa, wb, wo,
           final_w):
    bsz, t = positions.shape
    tiles_per_seq = t // TILE
    m = x2.shape[0]
    n = m // TILE
    assert w_in_t.shape == (IN_WIDTH, D_MODEL) and w_in_t.dtype == jnp.float32
    bf16 = jnp.bfloat16
    cur = lambda i: (jnp.minimum(i, n - 1), 0)
    lag = lambda i: (jnp.maximum(i - 1, 0), 0)
    const = lambda i: (0, 0)
    n_chunks = TILE // CHUNK
    return pl.pallas_call(
        functools.partial(_block_kernel, tiles_per_seq),
        grid=(n + 1,),
        in_specs=[
            pl.BlockSpec(memory_space=pltpu.SMEM),
            pl.BlockSpec((TILE, D_MODEL), cur),
            pl.BlockSpec((TILE, D_MODEL), lag),
            pl.BlockSpec((bsz, TILE), lambda i: (0, jnp.minimum(i, n - 1) % tiles_per_seq)),
            pl.BlockSpec((1, D_MODEL), const),
            pl.BlockSpec((1, LANES), const),
            pl.BlockSpec(memory_space=pl.ANY),
            pl.BlockSpec((GATE_RANK, B_QK_WIDTH), const),
            pl.BlockSpec((1, B_QK_WIDTH), const),
            pl.BlockSpec((1, B_VAL_DIM), const),
            pl.BlockSpec(memory_space=pl.ANY),
            pl.BlockSpec(memory_space=pl.ANY),
            pl.BlockSpec(memory_space=pl.ANY),
            pl.BlockSpec((1, D_MODEL), const),
        ],
        out_specs=pl.BlockSpec((TILE, D_MODEL), lag),
        out_shape=jax.ShapeDtypeStruct((m, D_MODEL), jnp.float32),
        scratch_shapes=[
            pltpu.VMEM((D_MODEL, PROJ_WIDTH), bf16),
            pltpu.VMEM((A_WIDTH, D_MODEL), bf16),
            pltpu.VMEM((B_V_WIDTH, D_MODEL), bf16),
            pltpu.VMEM((D_MODEL, D_MODEL), bf16),
            pltpu.VMEM((W_STAGE_BUFS, W_SLAB, D_MODEL), jnp.float32),
            pltpu.SemaphoreType.DMA((W_STAGE_BUFS,)),
            pltpu.VMEM((LANES, B_QK_WIDTH), bf16),
            pltpu.VMEM((TILE, PROJ_WIDTH), bf16),
            pltpu.VMEM((TILE, PROJ_WIDTH), bf16),
            pltpu.VMEM((TILE, A_WIDTH), bf16),
            pltpu.VMEM((TILE, B_V_WIDTH), bf16),
            pltpu.VMEM((BLOCK, 2 * A_KV_WIDTH), bf16),
            pltpu.VMEM((2, A_KV_HEADS, SWA_ROWS, SWA_COLS), jnp.float32),
            pltpu.VMEM((B_HEADS, B_KEY_DIM, B_VAL_DIM), jnp.float32),
            pltpu.VMEM((TILE, B_QK_WIDTH), bf16),
            pltpu.VMEM((TILE, B_QK_WIDTH), bf16),
            pltpu.VMEM((TILE, B_QK_WIDTH), bf16),
            pltpu.VMEM((B_QK_WIDTH, LANES), jnp.float32),
            pltpu.VMEM((n_chunks, B_HEADS, B_KEY_DIM, B_VAL_DIM), jnp.float32),
            pltpu.VMEM((TILE, B_V_WIDTH), jnp.float32),
            pltpu.VMEM((TILE, TILE), bf16),
        ],
        compiler_params=pltpu.CompilerParams(dimension_semantics=("arbitrary",),
                                             vmem_limit_bytes=VMEM_LIMIT_BYTES),
        name="block",
    )(sinks, x2, x2, positions, norm_w, inv_freq_row, w_in_t, gate_up, gate_bias, b_norm_w, wa, wb, wo,
      final_w)


def kernel(x, positions, norm_w, w_in, a_sinks, b_gate_up, b_gate_bias, b_out_norm_w,
           w_a_proj, w_b_proj, w_out, final_norm_w):
    bsz, t, d = x.shape
    assert d == D_MODEL and t % (2 * TILE) == 0
    depth = norm_w.shape[0]
    assert depth == 1

    half = A_HEAD_DIM // 2
    inv_freq = ROPE_THETA ** (-jnp.arange(half, dtype=jnp.float32) / half)
    inv_freq_row = jnp.tile(inv_freq, LANES // half)[None, :]

    x2 = x.reshape(bsz * t, d)
    for layer in range(depth):
        x2 = _block(x2, positions, norm_w[layer][None, :], inv_freq_row, w_in[layer].T, a_sinks[layer],
                    b_gate_up[layer], b_gate_bias[layer][None, :], b_out_norm_w[layer][None, :],
                    w_a_proj[layer], w_b_proj[layer], w_out[layer], final_norm_w[None, :])
    return x2.reshape(bsz, t, d)
```
